```python
import jax, jax.numpy as jnp
from jax import lax
import numpy as np

D_MODEL = 2048
BATCH = 4
SEQ = 2048
DEPTH = 1

RWKV_HEADS = 16
RWKV_HEAD_DIM = 64
RWKV_WIDTH = RWKV_HEADS * RWKV_HEAD_DIM
LORA_DECAY = 96
LORA_ICLR = 96
LORA_GATE = 256
GN_EPS = 64e-5
RWKV_COLS = 3 * RWKV_WIDTH + LORA_DECAY + LORA_ICLR + LORA_GATE
NSA_HEADS = 16
NSA_KV_HEADS = 4
NSA_GROUP = NSA_HEADS // NSA_KV_HEADS
NSA_HEAD_DIM = 64
NSA_Q_WIDTH = NSA_HEADS * NSA_HEAD_DIM
NSA_KV_WIDTH = NSA_KV_HEADS * NSA_HEAD_DIM
CMP_BLOCK = 32
CMP_STRIDE = 16
CMP_HIDDEN = 256
SEL_BLOCK = 64
SEL_TOPN = 16
WINDOW = 512
SEL_QBLK = 64
WIN_QBLK = 128
NSA_COLS = NSA_Q_WIDTH + 6 * NSA_KV_WIDTH + 3 * NSA_HEADS
GATE_COLS = 2 * D_MODEL
IN_COLS = RWKV_COLS + NSA_COLS + GATE_COLS
D_FF = 4 * D_MODEL
NORM_EPS = 1e-5
NEG_INF = -1e30
TINY = 1e-30

kernel_name = "rwkv7_nsa_gated_hybrid_block"


def rmsnorm(x, g):
    x32 = x.astype(jnp.float32)
    y = x32 * lax.rsqrt(jnp.mean(x32 * x32, axis=-1, keepdims=True) + NORM_EPS)
    return (y * g.astype(jnp.float32)).astype(x.dtype)


def masked_softmax(s, mask):
    s = jnp.where(mask, s, NEG_INF)
    s = s - jnp.max(s, axis=-1, keepdims=True)
    p = jnp.exp(s) * mask.astype(jnp.float32)
    return p / jnp.maximum(jnp.sum(p, axis=-1, keepdims=True), TINY)


def alibi_slopes(n):
    return 2.0 ** (-8.0 * jnp.arange(1, n + 1, dtype=jnp.float32) / n)


def to_chunks(t, axis, size):
    shp = t.shape
    t = t.reshape(shp[:axis] + (shp[axis] // size, size) + shp[axis + 1:])
    return jnp.moveaxis(t, axis, 0)


def from_chunks(o):
    o = jnp.moveaxis(o, 0, 3)
    b, g, hg, nc, qb, d = o.shape
    return o.reshape(b, g, hg, nc * qb, d).transpose(0, 3, 1, 2, 4)


def rwkv7_time_mix(z, mu, w0, w_up, a0, a_up, g_up, k_k, k_a, r_k, lnx_w, lnx_b):
    B, T, _ = z.shape
    H, N = RWKV_HEADS, RWKV_HEAD_DIM
    z_prev = jnp.pad(z, ((0, 0), (1, 0), (0, 0)))[:, :-1]
    z = z + (z_prev - z) * mu
    offs = [int(o) for o in np.cumsum([RWKV_WIDTH, RWKV_WIDTH, RWKV_WIDTH, LORA_DECAY, LORA_ICLR])]
    r, k, v, w_lo, a_lo, g_lo = jnp.split(z, offs, axis=-1)
    w_log = -jax.nn.softplus(-(w0 + jnp.tanh(w_lo) @ w_up)) - 0.5
    a = jax.nn.sigmoid(a0 + a_lo @ a_up)
    g = jax.nn.sigmoid(g_lo) @ g_up
    heads = lambda t: t.astype(jnp.float32).reshape(B, T, H, N)
    r, k, v, a = heads(r), heads(k), heads(v), heads(a)
    decay = jnp.exp(-jnp.exp(heads(w_log)))
    kk = k * k_k
    kk = kk * lax.rsqrt(jnp.maximum(jnp.sum(kk * kk, axis=-1, keepdims=True), 1e-24))
    k = k * (1.0 + (a - 1.0) * k_a)

    def step(S, inp):
        r_t, w_t, k_t, v_t, kk_t, a_t = inp
        sa = jnp.einsum('bhij,bhj->bhi', S, -kk_t)
        S = S * w_t[:, :, None, :] + sa[..., None] * (kk_t * a_t)[:, :, None, :] + v_t[..., None] * k_t[:, :, None, :]
        return S, jnp.einsum('bhij,bhj->bhi', S, r_t)

    xs = tuple(jnp.moveaxis(t, 1, 0) for t in (r, decay, k, v, kk, a))
    _, y = lax.scan(step, jnp.zeros((B, H, N, N), jnp.float32), xs)
    y = jnp.moveaxis(y, 0, 1)
    mean = jnp.mean(y, axis=-1, keepdims=True)
    var = jnp.mean(jnp.square(y - mean), axis=-1, keepdims=True)
    y = ((y - mean) * lax.rsqrt(var + GN_EPS)).reshape(B, T, RWKV_WIDTH) * lnx_w + lnx_b
    bonus = (jnp.sum(r * k * r_k, axis=-1, keepdims=True) * v).reshape(B, T, RWKV_WIDTH)
    return ((y + bonus) * g).astype(z.dtype)


def compress(kv, pe, w1, w2):
    B, T, G, d = kv.shape
    n_cmp = (T - CMP_BLOCK) // CMP_STRIDE + 1
    idx = np.arange(n_cmp)[:, None] * CMP_STRIDE + np.arange(CMP_BLOCK)[None, :]
    blk = kv[:, idx] + pe[:, None, :]
    blk = blk.transpose(0, 1, 3, 2, 4).reshape(B, n_cmp, G, CMP_BLOCK * d)
    return jax.nn.gelu(blk @ w1) @ w2


def nsa_attention(q, kc_raw, vc_raw, ks, vs, kw, vw, gate_logits,
                  cmp_pe_k, cmp_w1_k, cmp_w2_k, cmp_pe_v, cmp_w1_v, cmp_w2_v):
    B, T, _ = q.shape
    G, Hg, d = NSA_KV_HEADS, NSA_GROUP, NSA_HEAD_DIM
    scale = d ** -0.5
    kvh = lambda t: t.reshape(B, T, G, d)
    qg = q.reshape(B, T, G, Hg, d)
    slopes = alibi_slopes(NSA_HEADS).reshape(G, Hg)[None, :, :, None, None]
    tpos = jnp.arange(T)

    kc = compress(kvh(kc_raw), cmp_pe_k, cmp_w1_k, cmp_w2_k)
    vc = compress(kvh(vc_raw), cmp_pe_v, cmp_w1_v, cmp_w2_v)
    n_cmp = kc.shape[1]
    ends = jnp.arange(n_cmp) * CMP_STRIDE + (CMP_BLOCK - 1)
    dist_c = (tpos[:, None] - ends[None, :]).astype(jnp.float32)
    s_c = jnp.einsum('btghd,bngd->bghtn', qg, kc).astype(jnp.float32) * scale - slopes * dist_c
    p_cmp = masked_softmax(s_c, dist_c >= 0)
    o_cmp = jnp.einsum('bghtn,bngd->btghd', p_cmp.astype(vc.dtype), vc)

    n_sel = T // SEL_BLOCK
    n_top = min(SEL_TOPN, n_sel)
    cs = np.arange(n_cmp)[:, None] * CMP_STRIDE
    ss = np.arange(n_sel)[None, :] * SEL_BLOCK
    overlap = np.clip(np.minimum(cs + CMP_BLOCK, ss + SEL_BLOCK) - np.maximum(cs, ss), 0, None) / CMP_BLOCK
    imp = jnp.einsum('bghtn,nj->bgtj', p_cmp, jnp.asarray(overlap, jnp.float32))
    blk = jnp.arange(n_sel)[None, :]
    cur = (tpos // SEL_BLOCK)[:, None]
    forced = (blk == 0) | (blk == cur) | (blk == cur - 1)
    score = jnp.where(blk > cur, NEG_INF, jnp.where(forced, -NEG_INF, imp))
    top_val, top_idx = lax.top_k(score, n_top)
    top_ok = top_val > 0.5 * NEG_INF

    qh = qg.transpose(0, 2, 3, 1, 4)

    k_blk = kvh(ks).transpose(0, 2, 1, 3).reshape(B, G, n_sel, SEL_BLOCK, d)
    v_blk = kvh(vs).transpose(0, 2, 1, 3).reshape(B, G, n_sel, SEL_BLOCK, d)
    bi = jnp.arange(B)[:, None, None, None]
    gi = jnp.arange(G)[None, :, None, None]

    def sel_block(args):
        q_c, idx_c, ok_c, t0 = args
        kg = k_blk[bi, gi, idx_c].reshape(B, G, SEL_QBLK, n_top * SEL_BLOCK, d)
        vg = v_blk[bi, gi, idx_c].reshape(B, G, SEL_QBLK, n_top * SEL_BLOCK, d)
        pos = (idx_c[..., None] * SEL_BLOCK + jnp.arange(SEL_BLOCK)).reshape(B, G, SEL_QBLK, n_top * SEL_BLOCK)
        tq = t0 + jnp.arange(SEL_QBLK)
        dist = (tq[:, None] - pos).astype(jnp.float32)
        mask = jnp.repeat(ok_c, SEL_BLOCK, axis=-1) & (dist >= 0)
        s = jnp.einsum('bghqd,bgqkd->bghqk', q_c, kg).astype(jnp.float32) * scale - slopes * dist[:, :, None]
        p = masked_softmax(s, mask[:, :, None])
        return jnp.einsum('bghqk,bgqkd->bghqd', p.astype(vg.dtype), vg)

    o_slc = from_chunks(lax.map(sel_block, (to_chunks(qh, 3, SEL_QBLK), to_chunks(top_idx, 2, SEL_QBLK),
                                            to_chunks(top_ok, 2, SEL_QBLK), jnp.arange(T // SEL_QBLK) * SEL_QBLK)))

    pad = ((0, 0), (0, 0), (WINDOW, 0), (0, 0))
    kwp = jnp.pad(kvh(kw).transpose(0, 2, 1, 3), pad)
    vwp = jnp.pad(kvh(vw).transpose(0, 2, 1, 3), pad)
    span = WINDOW + WIN_QBLK

    def win_block(args):
        q_c, t0 = args
        kb = lax.dynamic_slice_in_dim(kwp, t0, span, axis=2)
        vb = lax.dynamic_slice_in_dim(vwp, t0, span, axis=2)
        tq = t0 + jnp.arange(WIN_QBLK)
        pos = t0 - WINDOW + jnp.arange(span)
        dist = (tq[:, None] - pos[None, :]).astype(jnp.float32)
        mask = (pos[None, :] >= 0) & (dist >= 0) & (dist < WINDOW)
        s = jnp.einsum('bghqd,bgkd->bghqk', q_c, kb).astype(jnp.float32) * scale - slopes * dist
        p = masked_softmax(s, mask)
        return jnp.einsum('bghqk,bgkd->bghqd', p.astype(vb.dtype), vb)

    o_win = from_chunks(lax.map(win_block, (to_chunks(qh, 3, WIN_QBLK), jnp.arange(T // WIN_QBLK) * WIN_QBLK)))

    gates = jax.nn.sigmoid(gate_logits.astype(jnp.float32)).reshape(B, T, 3, G, Hg, 1)
    o = gates[:, :, 0] * o_cmp + gates[:, :, 1] * o_slc + gates[:, :, 2] * o_win
    return o.reshape(B, T, NSA_Q_WIDTH).astype(q.dtype)


def setup_inputs(seed: int = 0) -> dict:
    key = jax.random.key(seed)
    ks = jax.random.split(key, 32)
    nrm = lambda k, shape, sc: sc * jax.random.normal(k, shape, jnp.float32)
    L, H, N, d = DEPTH, RWKV_HEADS, RWKV_HEAD_DIM, NSA_HEAD_DIM
    return {
        "x": nrm(ks[0], (BATCH, SEQ, D_MODEL), 1.0),
        "norm_mix": 1.0 + nrm(ks[1], (L, D_MODEL), 0.02),
        "w_in": nrm(ks[2], (L, D_MODEL, IN_COLS), D_MODEL ** -0.5),
        "rwkv_mu": jax.random.uniform(ks[3], (L, RWKV_COLS), jnp.float32),
        "rwkv_w0": jnp.linspace(-6.0, -0.5, RWKV_WIDTH)[None, :] + nrm(ks[4], (L, RWKV_WIDTH), 0.1),
        "rwkv_w_up": nrm(ks[5], (L, LORA_DECAY, RWKV_WIDTH), 0.1 * LORA_DECAY ** -0.5),
        "rwkv_a0": nrm(ks[6], (L, RWKV_WIDTH), 0.1),
        "rwkv_a_up": nrm(ks[7], (L, LORA_ICLR, RWKV_WIDTH), LORA_ICLR ** -0.5),
        "rwkv_g_up": nrm(ks[8], (L, LORA_GATE, RWKV_WIDTH), LORA_GATE ** -0.5),
        "rwkv_k_k": 0.85 + nrm(ks[9], (L, H, N), 0.02),
        "rwkv_k_a": 1.0 + nrm(ks[10], (L, H, N), 0.02),
        "rwkv_r_k": nrm(ks[11], (L, H, N), 0.1),
        "rwkv_lnx_w": 1.0 + nrm(ks[12], (L, RWKV_WIDTH), 0.02),
        "rwkv_lnx_b": nrm(ks[13], (L, RWKV_WIDTH), 0.02),
        "cmp_pe_k": nrm(ks[14], (L, CMP_BLOCK, d), 0.02),
        "cmp_w1_k": nrm(ks[15], (L, CMP_BLOCK * d, CMP_HIDDEN), (CMP_BLOCK * d) ** -0.5),
        "cmp_w2_k": nrm(ks[16], (L, CMP_HIDDEN, d), CMP_HIDDEN ** -0.5),
        "cmp_pe_v": nrm(ks[17], (L, CMP_BLOCK, d), 0.02),
        "cmp_w1_v": nrm(ks[18], (L, CMP_BLOCK * d, CMP_HIDDEN), (CMP_BLOCK * d) ** -0.5),
        "cmp_w2_v": nrm(ks[19], (L, CMP_HIDDEN, d), CMP_HIDDEN ** -0.5),
        "w_out_rwkv": nrm(ks[20], (L, RWKV_WIDTH, D_MODEL), RWKV_WIDTH ** -0.5),
        "w_out_nsa": nrm(ks[21], (L, NSA_Q_WIDTH, D_MODEL), NSA_Q_WIDTH ** -0.5),
        "w_o": nrm(ks[22], (L, D_MODEL, D_MODEL), D_MODEL ** -0.5),
        "norm_mlp": 1.0 + nrm(ks[23], (L, D_MODEL), 0.02),
        "mlp_w_up": nrm(ks[24], (L, D_MODEL, D_FF), D_MODEL ** -0.5),
        "mlp_w_down": nrm(ks[25], (L, D_FF, D_MODEL), D_FF ** -0.5),
        "norm_final": 1.0 + nrm(ks[26], (D_MODEL,), 0.02),
    }


def reference(x, norm_mix, w_in, rwkv_mu, rwkv_w0, rwkv_w_up, rwkv_a0, rwkv_a_up, rwkv_g_up,
              rwkv_k_k, rwkv_k_a, rwkv_r_k, rwkv_lnx_w, rwkv_lnx_b,
              cmp_pe_k, cmp_w1_k, cmp_w2_k, cmp_pe_v, cmp_w1_v, cmp_w2_v,
              w_out_rwkv, w_out_nsa, w_o, norm_mlp, mlp_w_up, mlp_w_down, norm_final):
    offs = [int(o) for o in np.cumsum([RWKV_COLS, NSA_Q_WIDTH] + [NSA_KV_WIDTH] * 6 + [3 * NSA_HEADS, D_MODEL])]
    h = x
    for l in range(DEPTH):
        xn = rmsnorm(h, norm_mix[l])
        proj = xn @ w_in[l]
        z_rwkv, q, kc, vc, ks, vs, kw, vw, nsa_gate, gate_a, gate_b = jnp.split(proj, offs, axis=-1)
        y_a = rwkv7_time_mix(z_rwkv, rwkv_mu[l], rwkv_w0[l], rwkv_w_up[l], rwkv_a0[l], rwkv_a_up[l],
                             rwkv_g_up[l], rwkv_k_k[l], rwkv_k_a[l], rwkv_r_k[l], rwkv_lnx_w[l], rwkv_lnx_b[l])
        y_b = nsa_attention(q, kc, vc, ks, vs, kw, vw, nsa_gate, cmp_pe_k[l], cmp_w1_k[l], cmp_w2_k[l],
                            cmp_pe_v[l], cmp_w1_v[l], cmp_w2_v[l])
        mixed = jax.nn.sigmoid(gate_a) * (y_a @ w_out_rwkv[l]) + jax.nn.sigmoid(gate_b) * (y_b @ w_out_nsa[l])
        h = h + mixed @ w_o[l]
        hn = rmsnorm(h, norm_mlp[l])
        h = h + jnp.square(jax.nn.relu(hn @ mlp_w_up[l])) @ mlp_w_down[l]
    return rmsnorm(h, norm_final)
```

```python
import functools

import numpy as np
import jax
import jax.numpy as jnp
from jax import lax
from jax.experimental import pallas as pl
from jax.experimental.pallas import tpu as pltpu

F32 = jnp.float32
BF16 = jnp.bfloat16

D_MODEL = 2048
BATCH = 4
SEQ = 2048
ROWS = BATCH * SEQ
RWKV_HEADS = 16
HEAD_DIM = 64
RWKV_WIDTH = RWKV_HEADS * HEAD_DIM
LORA_DECAY = 96
LORA_ICLR = 96
LORA_GATE = 256
GN_EPS = 64e-5
NSA_HEADS = 16
NSA_KV_HEADS = 4
NSA_GROUP = NSA_HEADS // NSA_KV_HEADS
NSA_Q_WIDTH = NSA_HEADS * HEAD_DIM
NSA_KV_WIDTH = NSA_KV_HEADS * HEAD_DIM
CMP_BLOCK = 32
CMP_STRIDE = 16
CMP_HIDDEN = 256
N_CMP = (SEQ - CMP_BLOCK) // CMP_STRIDE + 1
N_CMP_PAD = 128
SEL_BLOCK = 64
SEL_TOPN = 16
N_SEL = SEQ // SEL_BLOCK
WINDOW = 512
D_FF = 4 * D_MODEL
NORM_EPS = 1e-5
NEG_INF = -1e30
TINY = 1e-30

LANES = 128
VMEM_LIMIT = 56 * 1024 * 1024

COL_GA = 0
COL_GB = 2048
COL_R = 4096
COL_K = 5120
COL_V = 6144
COL_LORA = 7168
COL_Q = 7680
COL_KC = 8704
COL_VC = 8960
COL_KS = 9216
COL_VS = 9472
COL_KW = 9728
COL_VW = 9984
COL_NG = 10240
PROJ_COLS = 10368
LORA_PACK = 512


def _dot(a, b):
    return jnp.dot(a.astype(BF16), b.astype(BF16), preferred_element_type=F32)


def _dot_exact_rhs(a, b_bf16):
    a1 = a.astype(BF16)
    r1 = a - a1.astype(F32)
    a2 = r1.astype(BF16)
    a3 = (r1 - a2.astype(F32)).astype(BF16)
    d = lambda p: jnp.dot(p, b_bf16, preferred_element_type=F32)
    return d(a1) + d(a2) + d(a3)


def _hdot(a, b):
    return jnp.dot(a, b, preferred_element_type=F32, precision=lax.Precision.HIGHEST)


def _sigmoid(x):
    return 1.0 / (1.0 + jnp.exp(-x))


def _params(sem, limit=VMEM_LIMIT):
    return pltpu.CompilerParams(dimension_semantics=sem, vmem_limit_bytes=limit)


def _inproj_kernel(x_ref, g_ref, w_ref, o_ref, xn_ref):
    @pl.when(pl.program_id(1) == 0)
    def _():
        x = x_ref[...]
        ms = jnp.mean(x * x, axis=-1, keepdims=True)
        xn_ref[...] = (x * lax.rsqrt(ms + NORM_EPS) * g_ref[...]).astype(BF16)

    o_ref[...] = jnp.dot(xn_ref[...], w_ref[...], preferred_element_type=F32)


def _inproj(x2, g, w_p):
    tm, tn = 512, 1152
    return pl.pallas_call(
        _inproj_kernel,
        grid=(ROWS // tm, PROJ_COLS // tn),
        in_specs=[pl.BlockSpec((tm, D_MODEL), lambda i, j: (i, 0)),
                  pl.BlockSpec((1, D_MODEL), lambda i, j: (0, 0)),
                  pl.BlockSpec((D_MODEL, tn), lambda i, j: (0, j))],
        out_specs=pl.BlockSpec((tm, tn), lambda i, j: (i, j)),
        out_shape=jax.ShapeDtypeStruct((ROWS, PROJ_COLS), F32),
        scratch_shapes=[pltpu.VMEM((tm, D_MODEL), BF16)],
        compiler_params=_params(("parallel", "arbitrary")),
        name="inproj",
    )(x2, g, w_p)


def _head_sum(x, ones_ref):
    hi = x.astype(BF16)
    lo = (x - hi.astype(F32)).astype(BF16)
    ones = ones_ref[...]
    return (jnp.dot(hi, ones, preferred_element_type=F32)
            + jnp.dot(lo, ones, preferred_element_type=F32))


def _rwkvprep_kernel(r_ref, k_ref, v_ref, lo_ref, rp_ref, kp_ref, vp_ref, lop_ref,
                     mu_ref, w0_ref, wup_ref, a0_ref, aup_ref, gup_ref, kk_ref, ka_ref, rk_ref, ones_ref,
                     ro_ref, wo_ref, ko_ref, vo_ref, kko_ref, bo_ref, go_ref, bon_ref, *, tm):
    i = pl.program_id(0)
    has_prev = jnp.where((i * tm) % SEQ == 0, 0.0, 1.0).astype(F32)

    def shift(cur_ref, prev_ref, mu):
        z = cur_ref[...]
        zp = pltpu.roll(z, 1, axis=0)
        prev_row = prev_ref[7:8, :] * has_prev
        row = lax.broadcasted_iota(jnp.int32, z.shape, 0)
        zp = jnp.where(row == 0, prev_row, zp)
        return z + (zp - z) * mu

    r = shift(r_ref, rp_ref, mu_ref[:, 0:1024])
    k = shift(k_ref, kp_ref, mu_ref[:, 1024:2048])
    v = shift(v_ref, vp_ref, mu_ref[:, 2048:3072])
    lo = shift(lo_ref, lop_ref, mu_ref[:, 3072:3584])
    w_lo, a_lo, g_lo = lo[:, 0:128], lo[:, 128:256], lo[:, 256:512]

    u = w0_ref[...] + _hdot(jnp.tanh(w_lo), wup_ref[...])
    decay = jnp.exp(-(_sigmoid(u) * float(np.exp(-0.5))))
    a = _sigmoid(a0_ref[...] + _hdot(a_lo, aup_ref[...]))
    g = _hdot(_sigmoid(g_lo), gup_ref[...])

    kk = k * kk_ref[...]
    ss = _head_sum(kk * kk, ones_ref)
    kk = kk * lax.rsqrt(jnp.maximum(ss, 1e-24))
    k = k * (1.0 + (a - 1.0) * ka_ref[...])
    coef = _head_sum(r * k * rk_ref[...], ones_ref)

    ro_ref[...] = r
    wo_ref[...] = decay
    ko_ref[...] = k
    vo_ref[...] = v
    kko_ref[...] = kk
    bo_ref[...] = kk * a
    go_ref[...] = g
    bon_ref[...] = coef * v


def _rwkvprep(proj, mu_p, w0, wup_p, a0, aup_p, gup, k_k, k_a, r_k, ones):
    tm = 256
    pb = tm // 8
    cur = lambda c: pl.BlockSpec((tm, 1024), lambda i, c=c: (i, c))
    prev = lambda c: pl.BlockSpec((8, 1024), lambda i, c=c: (jnp.maximum(i * pb - 1, 0), c))
    full = lambda s: pl.BlockSpec(s, lambda i: (0,) * len(s))
    out = pl.BlockSpec((tm, 1024), lambda i: (i, 0))
    osh = jax.ShapeDtypeStruct((ROWS, RWKV_WIDTH), F32)
    return pl.pallas_call(
        functools.partial(_rwkvprep_kernel, tm=tm),
        grid=(ROWS // tm,),
        in_specs=[cur(COL_R // 1024), cur(COL_K // 1024), cur(COL_V // 1024),
                  pl.BlockSpec((tm, LORA_PACK), lambda i: (i, COL_LORA // LORA_PACK)),
                  prev(COL_R // 1024), prev(COL_K // 1024), prev(COL_V // 1024),
                  pl.BlockSpec((8, LORA_PACK), lambda i: (jnp.maximum(i * pb - 1, 0), COL_LORA // LORA_PACK)),
                  full((1, 3584)), full((1, 1024)), full((128, 1024)), full((1, 1024)), full((128, 1024)),
                  full((256, 1024)), full((1, 1024)), full((1, 1024)), full((1, 1024)), full((1024, 1024))],
        out_specs=[out] * 8,
        out_shape=[osh] * 8,
        compiler_params=_params(("parallel",)),
        name="rwkvprep",
    )(proj, proj, proj, proj, proj, proj, proj, proj,
      mu_p, w0, wup_p, a0, aup_p, gup, k_k, k_a, r_k, ones)


N_JPAIR = HEAD_DIM // 2


def _scan_kernel(kk_ref, w_ref, b_ref, k_ref, r_ref, v_ref, y_ref, s_ref, *, tc):
    @pl.when(pl.program_id(0) == 0)
    def _():
        s_ref[...] = jnp.zeros_like(s_ref)

    def tree(parts):
        while len(parts) > 1:
            parts = [parts[n] + parts[n + 1] for n in range(0, len(parts), 2)]
        return parts[0]

    def step(t, carry):
        vt = v_ref[t]
        acc = [None] * 4
        for p in range(N_JPAIR):
            term = s_ref[p] * kk_ref[t, pl.ds(p, 1), :]
            acc[p % 4] = term if acc[p % 4] is None else acc[p % 4] + term
        sa = tree(acc)
        sa = sa + pltpu.roll(sa, 64, axis=1)
        acc = [None] * 4
        for p in range(N_JPAIR):
            sn = (s_ref[p] * w_ref[t, pl.ds(p, 1), :] - sa * b_ref[t, pl.ds(p, 1), :]
                  + vt * k_ref[t, pl.ds(p, 1), :])
            s_ref[p] = sn
            term = sn * r_ref[t, pl.ds(p, 1), :]
            acc[p % 4] = term if acc[p % 4] is None else acc[p % 4] + term
        y = tree(acc)
        y_ref[t] = y + pltpu.roll(y, 64, axis=1)
        return carry

    lax.fori_loop(0, tc, step, 0)


def _scan(kk, w, b, k, r, v):
    tc = 64
    jspec = pl.BlockSpec((tc, N_JPAIR, LANES), lambda i: (i, 0, 0))
    ispec = pl.BlockSpec((tc, HEAD_DIM, LANES), lambda i: (i, 0, 0))
    return pl.pallas_call(
        functools.partial(_scan_kernel, tc=tc),
        grid=(SEQ // tc,),
        in_specs=[jspec] * 5 + [ispec],
        out_specs=ispec,
        out_shape=jax.ShapeDtypeStruct((SEQ, HEAD_DIM, LANES), F32),
        scratch_shapes=[pltpu.VMEM((N_JPAIR, HEAD_DIM, LANES), F32)],
        compiler_params=_params(("arbitrary",)),
        name="scan",
    )(kk, w, b, k, r, v)


def _to_jop(a):
    a = a.reshape(BATCH, SEQ, RWKV_HEADS, 2, N_JPAIR).transpose(1, 4, 3, 0, 2)
    return a.reshape(SEQ, N_JPAIR, LANES)


def _to_iop(a):
    a = a.reshape(BATCH, SEQ, RWKV_HEADS, HEAD_DIM).transpose(1, 3, 0, 2).reshape(SEQ, HEAD_DIM, 64)
    return jnp.concatenate([a, a], axis=-1)


def _from_iop(y):
    y = y[:, :, :64].reshape(SEQ, HEAD_DIM, BATCH, RWKV_HEADS).transpose(2, 0, 3, 1)
    return y.reshape(ROWS, RWKV_WIDTH)


def _rwkvpost_kernel(y_ref, bon_ref, g_ref, lw_ref, lb_ref, ones_ref, o_ref):
    y = y_ref[...]
    mean = _head_sum(y, ones_ref) * (1.0 / HEAD_DIM)
    d = y - mean
    var = _head_sum(d * d, ones_ref) * (1.0 / HEAD_DIM)
    yn = d * lax.rsqrt(var + GN_EPS) * lw_ref[...] + lb_ref[...]
    o_ref[...] = (yn + bon_ref[...]) * g_ref[...]


def _rwkvpost(y, bonus, g, lnx_w, lnx_b, ones):
    tm = 512
    blk = pl.BlockSpec((tm, 1024), lambda i: (i, 0))
    vec = pl.BlockSpec((1, 1024), lambda i: (0, 0))
    return pl.pallas_call(
        _rwkvpost_kernel,
        grid=(ROWS // tm,),
        in_specs=[blk, blk, blk, vec, vec, pl.BlockSpec((1024, 1024), lambda i: (0, 0))],
        out_specs=blk,
        out_shape=jax.ShapeDtypeStruct((ROWS, RWKV_WIDTH), F32),
        compiler_params=_params(("parallel",)),
        name="rwkvpost",
    )(y, bonus, g, lnx_w, lnx_b, ones)


def _compress_kernel(c_ref, w1_ref, pe_ref, w2_ref, o_ref):
    ch = c_ref[0, 0]
    w1 = w1_ref[0]
    half = CMP_STRIDE * HEAD_DIM
    top = _hdot(ch, w1[:half])
    bot = _hdot(ch, w1[half:])
    bias = _hdot(jnp.broadcast_to(pe_ref[0], (8, CMP_BLOCK * HEAD_DIM)), w1)[0:1]
    hid = top + pltpu.roll(bot, N_CMP_PAD - 1, axis=0) + bias
    c0 = float(np.sqrt(2.0 / np.pi))
    act = 0.5 * hid * (1.0 + jnp.tanh(c0 * (hid + 0.044715 * (hid * hid * hid))))
    o_ref[0, 0] = _hdot(act, w2_ref[0])


def _compress(chunks, w1, pe, w2):
    nbg = BATCH * NSA_KV_HEADS
    return pl.pallas_call(
        _compress_kernel,
        grid=(2, nbg),
        in_specs=[pl.BlockSpec((1, 1, N_CMP_PAD, CMP_STRIDE * HEAD_DIM), lambda c, n: (c, n, 0, 0)),
                  pl.BlockSpec((1, CMP_BLOCK * HEAD_DIM, CMP_HIDDEN), lambda c, n: (c, 0, 0)),
                  pl.BlockSpec((1, 1, CMP_BLOCK * HEAD_DIM), lambda c, n: (c, 0, 0)),
                  pl.BlockSpec((1, CMP_HIDDEN, HEAD_DIM), lambda c, n: (c, 0, 0))],
        out_specs=pl.BlockSpec((1, 1, N_CMP_PAD, HEAD_DIM), lambda c, n: (c, n, 0, 0)),
        out_shape=jax.ShapeDtypeStruct((2, nbg, N_CMP_PAD, HEAD_DIM), F32),
        compiler_params=_params(("parallel", "parallel")),
        name="compress",
    )(chunks, w1, pe, w2)


def _softmax_parts(s, mask):
    s = jnp.where(mask, s, NEG_INF)
    m = jnp.max(s, axis=-1, keepdims=True)
    p = jnp.where(mask, jnp.exp(s - m), 0.0)
    den = jnp.maximum(jnp.sum(p, axis=-1, keepdims=True), TINY)
    return p, den


def _nsa_kernel(q_ref, kc_ref, vc_ref, kst_ref, vs_ref, kwt_ref, vw_ref, gate_ref, ov_ref, o_ref, *, tq):
    g = pl.program_id(1)
    t0 = pl.program_id(2) * tq
    hg = NSA_GROUP
    scale = HEAD_DIM ** -0.5

    q = q_ref[...]
    qs = jnp.concatenate([q[:, h * HEAD_DIM:(h + 1) * HEAD_DIM] for h in range(hg)], axis=0).astype(BF16)
    rows = lax.broadcasted_iota(jnp.int32, (hg * tq, 1), 0)
    head = g * hg + rows // tq
    slope = jnp.exp2(-0.5 * (head + 1).astype(F32))
    tpos = (t0 + rows % tq).astype(F32)

    def attend(s, dist, mask, v):
        p, den = _softmax_parts(s * scale - slope * dist, mask)
        return jnp.dot(p.astype(BF16), v, preferred_element_type=F32) / den, p, den

    s_c = lax.dot_general(qs, kc_ref[0, 0].astype(BF16), (((1,), (1,)), ((), ())), preferred_element_type=F32)
    n_idx = lax.broadcasted_iota(jnp.int32, (1, N_CMP_PAD), 1)
    dist_c = tpos - (n_idx * CMP_STRIDE + (CMP_BLOCK - 1)).astype(F32)
    mask_c = (dist_c >= 0) & (n_idx < N_CMP)
    o_cmp, p_c, den_c = attend(s_c, dist_c, mask_c, vc_ref[0, 0].astype(BF16))

    p_n = p_c / den_c
    p_sum = (p_n[0:tq] + p_n[tq:2 * tq]) + (p_n[2 * tq:3 * tq] + p_n[3 * tq:4 * tq])
    imp = _dot_exact_rhs(p_sum, ov_ref[...])
    blk = lax.broadcasted_iota(jnp.int32, (tq, N_SEL), 1)
    cur = (t0 + lax.broadcasted_iota(jnp.int32, (tq, N_SEL), 0)) // SEL_BLOCK
    forced = (blk == 0) | (blk == cur) | (blk == cur - 1)
    score = jnp.where(blk > cur, NEG_INF, jnp.where(forced, -NEG_INF, imp))
    rank = jnp.zeros((tq, N_SEL), F32)
    for jp in range(N_SEL):
        col = score[:, jp:jp + 1]
        beats = (col > score) | ((col == score) & (blk > jp))
        rank = rank + jnp.where(beats, 1.0, 0.0)
    sel = jnp.where((rank < SEL_TOPN) & (blk <= cur), 1.0, 0.0).astype(BF16)
    expand = jnp.where(lax.broadcasted_iota(jnp.int32, (N_SEL, SEQ), 1) // SEL_BLOCK
                       == lax.broadcasted_iota(jnp.int32, (N_SEL, SEQ), 0), 1.0, 0.0).astype(BF16)
    sel_keys = jnp.dot(sel, expand, preferred_element_type=F32)
    sel_keys = jnp.concatenate([sel_keys] * hg, axis=0)

    s_s = jnp.dot(qs, kst_ref[0], preferred_element_type=F32)
    dist_s = tpos - lax.broadcasted_iota(jnp.int32, (1, SEQ), 1).astype(F32)
    o_slc, _, _ = attend(s_s, dist_s, (sel_keys > 0.5) & (dist_s >= 0), vs_ref[0])

    span = WINDOW + tq
    start = pl.multiple_of(jnp.maximum(t0 - WINDOW, 0), LANES)
    s_w = jnp.dot(qs, kwt_ref[0, :, pl.ds(start, span)], preferred_element_type=F32)
    dist_w = tpos - (start + lax.broadcasted_iota(jnp.int32, (1, span), 1)).astype(F32)
    o_win, _, _ = attend(s_w, dist_w, (dist_w >= 0) & (dist_w < WINDOW), vw_ref[0, pl.ds(start, span), :])

    gates = _sigmoid(gate_ref[...])
    src = lax.broadcasted_iota(jnp.int32, (LANES, hg * HEAD_DIM), 0)
    dst_head = lax.broadcasted_iota(jnp.int32, (LANES, hg * HEAD_DIM), 1) // HEAD_DIM
    wide = lambda o: jnp.concatenate([o[h * tq:(h + 1) * tq] for h in range(hg)], axis=1)
    out = jnp.zeros((tq, hg * HEAD_DIM), F32)
    for c, o in enumerate((o_cmp, o_slc, o_win)):
        pick = jnp.where(src == c * NSA_HEADS + g * hg + dst_head, 1.0, 0.0).astype(BF16)
        out = out + _dot_exact_rhs(gates, pick) * wide(o)
    o_ref[...] = out


def _nsa(proj, kcvc, kst, vs, kwt, vw, overlap):
    tq = 128
    nq = SEQ // tq
    G = NSA_KV_HEADS
    kv_t = pl.BlockSpec((1, HEAD_DIM, SEQ), lambda b, g, i: (b * G + g, 0, 0))
    kv_n = pl.BlockSpec((1, SEQ, HEAD_DIM), lambda b, g, i: (b * G + g, 0, 0))
    cmp_spec = lambda c: pl.BlockSpec((1, 1, N_CMP_PAD, HEAD_DIM), lambda b, g, i, c=c: (c, b * G + g, 0, 0))
    return pl.pallas_call(
        functools.partial(_nsa_kernel, tq=tq),
        grid=(BATCH, G, nq),
        in_specs=[pl.BlockSpec((tq, NSA_GROUP * HEAD_DIM), lambda b, g, i: (b * nq + i, COL_Q // 256 + g)),
                  cmp_spec(0), cmp_spec(1), kv_t, kv_n, kv_t, kv_n,
                  pl.BlockSpec((tq, LANES), lambda b, g, i: (b * nq + i, COL_NG // LANES)),
                  pl.BlockSpec((N_CMP_PAD, N_SEL), lambda b, g, i: (0, 0))],
        out_specs=pl.BlockSpec((tq, NSA_GROUP * HEAD_DIM), lambda b, g, i: (b * nq + i, g)),
        out_shape=jax.ShapeDtypeStruct((ROWS, NSA_Q_WIDTH), F32),
        compiler_params=_params(("parallel", "parallel", "parallel")),
        name="nsa",
    )(proj, kcvc, kcvc, kst, vs, kwt, vw, proj, overlap)


def _mix_kernel(x_ref, ya_ref, yb_ref, ga_ref, gb_ref, wa_ref, wb_ref, wo_ref, h_ref):
    ma = _dot(ya_ref[...], wa_ref[...])
    mb = _dot(yb_ref[...], wb_ref[...])
    mixed = _sigmoid(ga_ref[...]) * ma + _sigmoid(gb_ref[...]) * mb
    h_ref[...] = x_ref[...] + _dot(mixed, wo_ref[...])


def _mix(x2, ya, yb, proj, wa, wb, wo):
    tm = 256
    row = lambda w, c=0: pl.BlockSpec((tm, w), lambda i, c=c: (i, c))
    const = lambda s: pl.BlockSpec(s, lambda i: (0, 0), pipeline_mode=pl.Buffered(1))
    return pl.pallas_call(
        _mix_kernel,
        grid=(ROWS // tm,),
        in_specs=[row(D_MODEL), row(1024), row(1024), row(D_MODEL, COL_GA // D_MODEL), row(D_MODEL, COL_GB // D_MODEL),
                  const((1024, D_MODEL)), const((1024, D_MODEL)), const((D_MODEL, D_MODEL))],
        out_specs=row(D_MODEL),
        out_shape=jax.ShapeDtypeStruct((ROWS, D_MODEL), F32),
        compiler_params=_params(("parallel",)),
        name="mix",
    )(x2, ya, yb, proj, proj, wa, wb, wo)


def _mlp_kernel(h_ref, gm_ref, wu_ref, wd_ref, gf_ref, o_ref, hn_ref, acc_ref):
    f = pl.program_id(1)

    @pl.when(f == 0)
    def _():
        h = h_ref[...]
        ms = jnp.mean(h * h, axis=-1, keepdims=True)
        hn_ref[...] = (h * lax.rsqrt(ms + NORM_EPS) * gm_ref[...]).astype(BF16)
        acc_ref[...] = jnp.zeros_like(acc_ref)

    u = jnp.maximum(jnp.dot(hn_ref[...], wu_ref[...], preferred_element_type=F32), 0.0)
    acc_ref[...] += jnp.dot((u * u).astype(BF16), wd_ref[...], preferred_element_type=F32)

    @pl.when(f == pl.num_programs(1) - 1)
    def _():
        h2 = h_ref[...] + acc_ref[...]
        ms = jnp.mean(h2 * h2, axis=-1, keepdims=True)
        o_ref[...] = h2 * lax.rsqrt(ms + NORM_EPS) * gf_ref[...]


def _mlp(h, g_mlp, w_up, w_down, g_final):
    tm, tf = 512, 512
    return pl.pallas_call(
        _mlp_kernel,
        grid=(ROWS // tm, D_FF // tf),
        in_specs=[pl.BlockSpec((tm, D_MODEL), lambda i, f: (i, 0)),
                  pl.BlockSpec((1, D_MODEL), lambda i, f: (0, 0)),
                  pl.BlockSpec((D_MODEL, tf), lambda i, f: (0, f)),
                  pl.BlockSpec((tf, D_MODEL), lambda i, f: (f, 0)),
                  pl.BlockSpec((1, D_MODEL), lambda i, f: (0, 0))],
        out_specs=pl.BlockSpec((tm, D_MODEL), lambda i, f: (i, 0)),
        out_shape=jax.ShapeDtypeStruct((ROWS, D_MODEL), F32),
        scratch_shapes=[pltpu.VMEM((tm, D_MODEL), BF16), pltpu.VMEM((tm, D_MODEL), F32)],
        compiler_params=_params(("parallel", "arbitrary")),
        name="mlp",
    )(h, g_mlp, w_up, w_down, g_final)


def _pack_cols(w, pad):
    return jnp.concatenate([
        w[..., 6128:10224],
        w[..., 0:3168], pad(32),
        w[..., 3168:3264], pad(32),
        w[..., 3264:3520],
        w[..., 3520:6080],
        w[..., 6080:6128], pad(80),
    ], axis=-1)


def _overlap_matrix():
    cs = np.arange(N_CMP)[:, None] * CMP_STRIDE
    ss = np.arange(N_SEL)[None, :] * SEL_BLOCK
    ov = np.clip(np.minimum(cs + CMP_BLOCK, ss + SEL_BLOCK) - np.maximum(cs, ss), 0, None) / CMP_BLOCK
    out = np.zeros((N_CMP_PAD, N_SEL), np.float32)
    out[:N_CMP] = ov
    return out


def kernel(x, norm_mix, w_in, rwkv_mu, rwkv_w0, rwkv_w_up, rwkv_a0, rwkv_a_up, rwkv_g_up, rwkv_k_k, rwkv_k_a,
           rwkv_r_k, rwkv_lnx_w, rwkv_lnx_b, cmp_pe_k, cmp_w1_k, cmp_w2_k, cmp_pe_v, cmp_w1_v, cmp_w2_v,
           w_out_rwkv, w_out_nsa, w_o, norm_mlp, mlp_w_up, mlp_w_down, norm_final):
    assert x.shape == (BATCH, SEQ, D_MODEL) and w_in.shape[0] == 1
    l = 0
    G = NSA_KV_HEADS
    x2 = x.reshape(ROWS, D_MODEL)
    row = lambda a: a.reshape(1, -1)

    w_p = _pack_cols(w_in[l], lambda n: jnp.zeros((D_MODEL, n), F32)).astype(BF16)
    proj = _inproj(x2, row(norm_mix[l]), w_p)

    mu = rwkv_mu[l]
    mu_p = jnp.concatenate([mu[0:3168], jnp.zeros((32,), F32), mu[3168:3264], jnp.zeros((32,), F32), mu[3264:3520]])
    zrows = jnp.zeros((32, RWKV_WIDTH), F32)
    ones = jnp.asarray(np.kron(np.eye(RWKV_HEADS), np.ones((HEAD_DIM, HEAD_DIM))), BF16)
    r, w, k, v, kk, b, g, bonus = _rwkvprep(
        proj, row(mu_p), row(rwkv_w0[l]), jnp.concatenate([rwkv_w_up[l], zrows]),
        row(rwkv_a0[l]), jnp.concatenate([rwkv_a_up[l], zrows]), rwkv_g_up[l],
        row(rwkv_k_k[l]), row(rwkv_k_a[l]), row(rwkv_r_k[l]), ones)
    y = _scan(_to_jop(kk), _to_jop(w), _to_jop(b), _to_jop(k), _to_jop(r), _to_iop(v))
    ya = _rwkvpost(_from_iop(y), bonus, g, row(rwkv_lnx_w[l]), row(rwkv_lnx_b[l]), ones)

    kv = lambda c0: proj[:, c0:c0 + NSA_KV_WIDTH].reshape(BATCH, SEQ, G, HEAD_DIM)
    chunks = jnp.stack([kv(COL_KC), kv(COL_VC)]).transpose(0, 1, 3, 2, 4)
    chunks = chunks.reshape(2, BATCH * G, N_CMP_PAD, CMP_STRIDE * HEAD_DIM)
    kcvc = _compress(chunks, jnp.stack([cmp_w1_k[l], cmp_w1_v[l]]),
                     jnp.stack([cmp_pe_k[l].reshape(1, -1), cmp_pe_v[l].reshape(1, -1)]),
                     jnp.stack([cmp_w2_k[l], cmp_w2_v[l]]))
    keys_t = lambda c0: kv(c0).transpose(0, 2, 3, 1).reshape(BATCH * G, HEAD_DIM, SEQ).astype(BF16)
    vals = lambda c0: kv(c0).transpose(0, 2, 1, 3).reshape(BATCH * G, SEQ, HEAD_DIM).astype(BF16)
    yb = _nsa(proj, kcvc, keys_t(COL_KS), vals(COL_VS), keys_t(COL_KW), vals(COL_VW),
              jnp.asarray(_overlap_matrix(), BF16))

    h = _mix(x2, ya, yb, proj, w_out_rwkv[l].astype(BF16), w_out_nsa[l].astype(BF16), w_o[l].astype(BF16))
    out = _mlp(h, row(norm_mlp[l]), mlp_w_up[l].astype(BF16), mlp_w_down[l].astype(BF16), row(norm_final))
    return out.reshape(BATCH, SEQ, D_MODEL)
```

```python
import functools

import numpy as np
import jax
import jax.numpy as jnp
from jax import lax
from jax.experimental import pallas as pl
from jax.experimental.pallas import tpu as pltpu

F32 = jnp.float32
BF16 = jnp.bfloat16

D_MODEL = 2048
BATCH = 4
SEQ = 2048
ROWS = BATCH * SEQ
RWKV_HEADS = 16
HEAD_DIM = 64
RWKV_WIDTH = RWKV_HEADS * HEAD_DIM
LORA_DECAY = 96
LORA_ICLR = 96
LORA_GATE = 256
GN_EPS = 64e-5
NSA_HEADS = 16
NSA_KV_HEADS = 4
NSA_GROUP = NSA_HEADS // NSA_KV_HEADS
NSA_Q_WIDTH = NSA_HEADS * HEAD_DIM
NSA_KV_WIDTH = NSA_KV_HEADS * HEAD_DIM
CMP_BLOCK = 32
CMP_STRIDE = 16
CMP_HIDDEN = 256
N_CMP = (SEQ - CMP_BLOCK) // CMP_STRIDE + 1
N_CMP_PAD = 128
SEL_BLOCK = 64
SEL_TOPN = 16
N_SEL = SEQ // SEL_BLOCK
WINDOW = 512
D_FF = 4 * D_MODEL
NORM_EPS = 1e-5
NEG_INF = -1e30
TINY = 1e-30

LANES = 128
VMEM_LIMIT = 56 * 1024 * 1024

COL_GA = 0
COL_GB = 2048
COL_R = 4096
COL_K = 5120
COL_V = 6144
COL_LORA = 7168
COL_Q = 7680
COL_KC = 8704
COL_VC = 8960
COL_KS = 9216
COL_VS = 9472
COL_KW = 9728
COL_VW = 9984
COL_NG = 10240
PROJ_COLS = 10368
LORA_PACK = 512


def _dot(a, b):
    return jnp.dot(a.astype(BF16), b.astype(BF16), preferred_element_type=F32)


def _hdot(a, b):
    return jnp.dot(a, b, preferred_element_type=F32, precision=lax.Precision.HIGHEST)


def _sigmoid(x):
    return 1.0 / (1.0 + jnp.exp(-x))


def _params(sem, limit=VMEM_LIMIT):
    return pltpu.CompilerParams(dimension_semantics=sem, vmem_limit_bytes=limit)


def _inproj_kernel(x_ref, g_ref, w_ref, o_ref, xn_ref):
    @pl.when(pl.program_id(1) == 0)
    def _():
        x = x_ref[...]
        ms = jnp.mean(x * x, axis=-1, keepdims=True)
        xn_ref[...] = (x * lax.rsqrt(ms + NORM_EPS) * g_ref[...]).astype(BF16)

    o_ref[...] = jnp.dot(xn_ref[...], w_ref[...], preferred_element_type=F32)


def _inproj(x2, g, w_p):
    tm, tn = 512, 1152
    return pl.pallas_call(
        _inproj_kernel,
        grid=(ROWS // tm, PROJ_COLS // tn),
        in_specs=[pl.BlockSpec((tm, D_MODEL), lambda i, j: (i, 0)),
                  pl.BlockSpec((1, D_MODEL), lambda i, j: (0, 0)),
                  pl.BlockSpec((D_MODEL, tn), lambda i, j: (0, j))],
        out_specs=pl.BlockSpec((tm, tn), lambda i, j: (i, j)),
        out_shape=jax.ShapeDtypeStruct((ROWS, PROJ_COLS), F32),
        scratch_shapes=[pltpu.VMEM((tm, D_MODEL), BF16)],
        compiler_params=_params(("parallel", "arbitrary")),
        name="inproj",
    )(x2, g, w_p)


def _head_sum(x, ones_ref):
    hi = x.astype(BF16)
    lo = (x - hi.astype(F32)).astype(BF16)
    ones = ones_ref[...]
    return (jnp.dot(hi, ones, preferred_element_type=F32)
            + jnp.dot(lo, ones, preferred_element_type=F32))


def _rwkvprep_kernel(r_ref, k_ref, v_ref, lo_ref, rp_ref, kp_ref, vp_ref, lop_ref,
                     mu_ref, w0_ref, wup_ref, a0_ref, aup_ref, gup_ref, kk_ref, ka_ref, rk_ref, ones_ref,
                     ro_ref, wo_ref, ko_ref, vo_ref, kko_ref, bo_ref, go_ref, bon_ref, *, tm):
    i = pl.program_id(0)
    has_prev = jnp.where((i * tm) % SEQ == 0, 0.0, 1.0).astype(F32)

    def shift(cur_ref, prev_ref, mu):
        z = cur_ref[...]
        zp = pltpu.roll(z, 1, axis=0)
        prev_row = prev_ref[7:8, :] * has_prev
        row = lax.broadcasted_iota(jnp.int32, z.shape, 0)
        zp = jnp.where(row == 0, prev_row, zp)
        return z + (zp - z) * mu

    r = shift(r_ref, rp_ref, mu_ref[:, 0:1024])
    k = shift(k_ref, kp_ref, mu_ref[:, 1024:2048])
    v = shift(v_ref, vp_ref, mu_ref[:, 2048:3072])
    lo = shift(lo_ref, lop_ref, mu_ref[:, 3072:3584])
    w_lo, a_lo, g_lo = lo[:, 0:128], lo[:, 128:256], lo[:, 256:512]

    u = w0_ref[...] + _hdot(jnp.tanh(w_lo), wup_ref[...])
    decay = jnp.exp(-(_sigmoid(u) * float(np.exp(-0.5))))
    a = _sigmoid(a0_ref[...] + _hdot(a_lo, aup_ref[...]))
    g = _hdot(_sigmoid(g_lo), gup_ref[...])

    kk = k * kk_ref[...]
    ss = _head_sum(kk * kk, ones_ref)
    kk = kk * lax.rsqrt(jnp.maximum(ss, 1e-24))
    k = k * (1.0 + (a - 1.0) * ka_ref[...])
    coef = _head_sum(r * k * rk_ref[...], ones_ref)

    ro_ref[...] = r
    wo_ref[...] = decay
    ko_ref[...] = k
    vo_ref[...] = v
    kko_ref[...] = kk
    bo_ref[...] = kk * a
    go_ref[...] = g
    bon_ref[...] = coef * v


def _rwkvprep(proj, mu_p, w0, wup_p, a0, aup_p, gup, k_k, k_a, r_k, ones):
    tm = 256
    pb = tm // 8
    cur = lambda c: pl.BlockSpec((tm, 1024), lambda i, c=c: (i, c))
    prev = lambda c: pl.BlockSpec((8, 1024), lambda i, c=c: (jnp.maximum(i * pb - 1, 0), c))
    full = lambda s: pl.BlockSpec(s, lambda i: (0,) * len(s))
    out = pl.BlockSpec((tm, 1024), lambda i: (i, 0))
    osh = jax.ShapeDtypeStruct((ROWS, RWKV_WIDTH), F32)
    return pl.pallas_call(
        functools.partial(_rwkvprep_kernel, tm=tm),
        grid=(ROWS // tm,),
        in_specs=[cur(COL_R // 1024), cur(COL_K // 1024), cur(COL_V // 1024),
                  pl.BlockSpec((tm, LORA_PACK), lambda i: (i, COL_LORA // LORA_PACK)),
                  prev(COL_R // 1024), prev(COL_K // 1024), prev(COL_V // 1024),
                  pl.BlockSpec((8, LORA_PACK), lambda i: (jnp.maximum(i * pb - 1, 0), COL_LORA // LORA_PACK)),
                  full((1, 3584)), full((1, 1024)), full((128, 1024)), full((1, 1024)), full((128, 1024)),
                  full((256, 1024)), full((1, 1024)), full((1, 1024)), full((1, 1024)), full((1024, 1024))],
        out_specs=[out] * 8,
        out_shape=[osh] * 8,
        compiler_params=_params(("parallel",)),
        name="rwkvprep",
    )(proj, proj, proj, proj, proj, proj, proj, proj,
      mu_p, w0, wup_p, a0, aup_p, gup, k_k, k_a, r_k, ones)


N_JPAIR = HEAD_DIM // 2
N_JOPS = 5


def _scan_kernel(kkn_ref, w_ref, b_ref, k_ref, r_ref, v_ref, y_ref, s_ref, sa_ref, d_ref, *, tc):
    @pl.when(pl.program_id(0) == 0)
    def _():
        s_ref[...] = jnp.zeros_like(s_ref)
        sa_ref[...] = jnp.zeros_like(sa_ref)

    lo_half = lax.broadcasted_iota(jnp.int32, (tc, N_JPAIR, LANES), 2) < 64
    for n, ref in enumerate((kkn_ref, w_ref, b_ref, k_ref, r_ref)):
        x = ref[...]
        xr = pltpu.roll(x, 64, axis=2)
        d_ref[n, :, 0:N_JPAIR, :] = jnp.where(lo_half, x, xr)
        d_ref[n, :, N_JPAIR:HEAD_DIM, :] = jnp.where(lo_half, xr, x)

    def tree(parts):
        while len(parts) > 1:
            parts = [parts[n] + parts[n + 1] for n in range(0, len(parts), 2)]
        return parts[0]

    def step(t, sa):
        vt = v_ref[t]
        acc_y = [None] * 4
        acc_s = [None] * 4
        for j in range(HEAD_DIM):
            row = lambda n: d_ref[n, t, pl.ds(j, 1), :]
            sn = s_ref[j] * row(1) - sa * row(2) + vt * row(3)
            s_ref[j] = sn
            ty = sn * row(4)
            ts = sn * row(0)
            acc_y[j % 4] = ty if acc_y[j % 4] is None else acc_y[j % 4] + ty
            acc_s[j % 4] = ts if acc_s[j % 4] is None else acc_s[j % 4] + ts
        y_ref[t] = tree(acc_y)
        return tree(acc_s)

    sa_ref[...] = lax.fori_loop(0, tc, step, sa_ref[...])


def _scan(kkn, w, b, k, r, v):
    tc = 64
    spec = pl.BlockSpec((tc, N_JPAIR, LANES), lambda i: (i, 0, 0))
    return pl.pallas_call(
        functools.partial(_scan_kernel, tc=tc),
        grid=(SEQ // tc,),
        in_specs=[spec] * 6,
        out_specs=spec,
        out_shape=jax.ShapeDtypeStruct((SEQ, N_JPAIR, LANES), F32),
        scratch_shapes=[pltpu.VMEM((HEAD_DIM, N_JPAIR, LANES), F32),
                        pltpu.VMEM((N_JPAIR, LANES), F32),
                        pltpu.VMEM((N_JOPS, tc, HEAD_DIM, LANES), F32)],
        compiler_params=_params(("arbitrary",)),
        name="scan",
    )(kkn, w, b, k, r, v)


def _to_scan(a):
    a = a.reshape(BATCH, SEQ, RWKV_HEADS, 2, N_JPAIR).transpose(1, 4, 3, 0, 2)
    return a.reshape(SEQ, N_JPAIR, LANES)


def _from_scan(y):
    y = y.reshape(SEQ, N_JPAIR, 2, BATCH, RWKV_HEADS).transpose(3, 0, 4, 2, 1)
    return y.reshape(ROWS, RWKV_WIDTH)


def _rwkvpost_kernel(y_ref, bon_ref, g_ref, lw_ref, lb_ref, ones_ref, o_ref):
    y = y_ref[...]
    mean = _head_sum(y, ones_ref) * (1.0 / HEAD_DIM)
    d = y - mean
    var = _head_sum(d * d, ones_ref) * (1.0 / HEAD_DIM)
    yn = d * lax.rsqrt(var + GN_EPS) * lw_ref[...] + lb_ref[...]
    o_ref[...] = (yn + bon_ref[...]) * g_ref[...]


def _rwkvpost(y, bonus, g, lnx_w, lnx_b, ones):
    tm = 512
    blk = pl.BlockSpec((tm, 1024), lambda i: (i, 0))
    vec = pl.BlockSpec((1, 1024), lambda i: (0, 0))
    return pl.pallas_call(
        _rwkvpost_kernel,
        grid=(ROWS // tm,),
        in_specs=[blk, blk, blk, vec, vec, pl.BlockSpec((1024, 1024), lambda i: (0, 0))],
        out_specs=blk,
        out_shape=jax.ShapeDtypeStruct((ROWS, RWKV_WIDTH), F32),
        compiler_params=_params(("parallel",)),
        name="rwkvpost",
    )(y, bonus, g, lnx_w, lnx_b, ones)


def _compress_kernel(c_ref, w1_ref, pe_ref, w2_ref, o_ref):
    ch = c_ref[0, 0]
    w1 = w1_ref[0]
    half = CMP_STRIDE * HEAD_DIM
    top = _hdot(ch, w1[:half])
    bot = _hdot(ch, w1[half:])
    bias = _hdot(jnp.broadcast_to(pe_ref[0], (8, CMP_BLOCK * HEAD_DIM)), w1)[0:1]
    hid = top + pltpu.roll(bot, N_CMP_PAD - 1, axis=0) + bias
    c0 = float(np.sqrt(2.0 / np.pi))
    act = 0.5 * hid * (1.0 + jnp.tanh(c0 * (hid + 0.044715 * (hid * hid * hid))))
    o_ref[0, 0] = _hdot(act, w2_ref[0])


def _compress(chunks, w1, pe, w2):
    nbg = BATCH * NSA_KV_HEADS
    return pl.pallas_call(
        _compress_kernel,
        grid=(2, nbg),
        in_specs=[pl.BlockSpec((1, 1, N_CMP_PAD, CMP_STRIDE * HEAD_DIM), lambda c, n: (c, n, 0, 0)),
                  pl.BlockSpec((1, CMP_BLOCK * HEAD_DIM, CMP_HIDDEN), lambda c, n: (c, 0, 0)),
                  pl.BlockSpec((1, 1, CMP_BLOCK * HEAD_DIM), lambda c, n: (c, 0, 0)),
                  pl.BlockSpec((1, CMP_HIDDEN, HEAD_DIM), lambda c, n: (c, 0, 0))],
        out_specs=pl.BlockSpec((1, 1, N_CMP_PAD, HEAD_DIM), lambda c, n: (c, n, 0, 0)),
        out_shape=jax.ShapeDtypeStruct((2, nbg, N_CMP_PAD, HEAD_DIM), F32),
        compiler_params=_params(("parallel", "parallel")),
        name="compress",
    )(chunks, w1, pe, w2)


MASK_BIG = float(2.0 ** 100)
KEY_CHUNK = 512
N_FEAT = 64


def _key_features():
    s = np.arange(SEQ)
    f = np.zeros((SEQ, N_FEAT), np.float32)
    f[s, s // SEL_BLOCK] = 1.0
    f[:, 32:35] = (SEL_BLOCK * (s // SEL_BLOCK))[:, None]
    f[:, 35:38] = (s % SEL_BLOCK)[:, None]
    return f


def _split3(x):
    a1 = x.astype(BF16)
    r1 = x - a1.astype(F32)
    a2 = r1.astype(BF16)
    a3 = (r1 - a2.astype(F32)).astype(BF16)
    return a1, a2, a3


def _nsa_kernel(q_ref, kc_ref, vc_ref, ks_ref, vst_ref, kw_ref, vwt_ref, gate_ref, ovt_ref, feat_ref, o_ref,
                gt_ref, *, tq):
    g = pl.program_id(1)
    t0 = pl.program_id(2) * tq
    hg = NSA_GROUP
    nq = hg * tq
    dot = functools.partial(jnp.dot, preferred_element_type=F32)

    qt = (q_ref[...] * HEAD_DIM ** -0.5).T
    qt = jnp.concatenate([qt[h * HEAD_DIM:(h + 1) * HEAD_DIM] for h in range(hg)], axis=1).astype(BF16)
    lane = lax.broadcasted_iota(jnp.int32, (1, nq), 1)
    slope = jnp.exp2(-0.5 * (g * hg + lane // tq + 1).astype(F32))
    qpos = t0 + lane % tq

    n_idx = lax.broadcasted_iota(jnp.int32, (N_CMP_PAD, 1), 0)
    dist_c = qpos.astype(F32) - (n_idx * CMP_STRIDE + (CMP_BLOCK - 1)).astype(F32)
    mask_c = (dist_c >= 0) & (n_idx < N_CMP)
    x_c = jnp.where(mask_c, dot(kc_ref[0, 0].astype(BF16), qt) - slope * dist_c, NEG_INF)
    p_c = jnp.where(mask_c, jnp.exp(x_c - jnp.max(x_c, axis=0, keepdims=True)), 0.0)
    den_c = jnp.maximum(jnp.sum(p_c, axis=0, keepdims=True), TINY)
    o_cmp = dot(vc_ref[0, 0].T.astype(BF16), p_c.astype(BF16)) / den_c

    p_n = p_c / den_c
    p_sum = (p_n[:, 0:tq] + p_n[:, tq:2 * tq]) + (p_n[:, 2 * tq:3 * tq] + p_n[:, 3 * tq:4 * tq])
    ovt = ovt_ref[...]
    imp = sum(dot(ovt, part) for part in _split3(p_sum))
    blk = lax.broadcasted_iota(jnp.int32, (N_SEL, tq), 0)
    cur = (t0 + lax.broadcasted_iota(jnp.int32, (N_SEL, tq), 1)) // SEL_BLOCK
    forced = (blk == 0) | (blk == cur) | (blk == cur - 1)
    score = jnp.where(blk > cur, NEG_INF, jnp.where(forced, -NEG_INF, imp))
    rank = jnp.zeros((N_SEL, tq), F32)
    for jp in range(N_SEL):
        row = score[jp:jp + 1, :]
        beats = (row > score) | ((row == score) & (blk > jp))
        rank = rank + jnp.where(beats, 1.0, 0.0)
    sel_bias = jnp.where((rank < SEL_TOPN) & (blk <= cur), 0.0, -MASK_BIG)
    sel_bias = jnp.concatenate([sel_bias] * hg, axis=1)

    s1, s2, s3 = (piece.astype(F32) for piece in _split3(slope))
    r8 = lax.broadcasted_iota(jnp.int32, (8, nq), 0)
    srow = jnp.where((r8 == 0) | (r8 == 3), s1, jnp.where((r8 == 1) | (r8 == 4), s2, s3))
    srow = jnp.where(r8 < 6, srow, 0.0)
    pad = jnp.zeros((N_FEAT - N_SEL - 8, nq), F32)
    q_slc = jnp.concatenate([qt, jnp.concatenate([sel_bias, srow, pad], axis=0).astype(BF16)], axis=0)
    q_win = jnp.concatenate([qt, jnp.concatenate([jnp.zeros_like(sel_bias), srow, pad], axis=0).astype(BF16)],
                            axis=0)

    def scores(k_ref, start, size, q_aug):
        keys = jnp.concatenate([k_ref[0, pl.ds(start, size), :], feat_ref[pl.ds(start, size), :]], axis=1)
        return dot(keys, q_aug)

    def online(carry, s, v_t):
        m, l, acc = carry
        m_new = jnp.maximum(m, jnp.max(s, axis=0, keepdims=True))
        alpha = jnp.exp(m - m_new)
        p = jnp.exp(s - m_new)
        return (m_new, alpha * l + jnp.sum(p, axis=0, keepdims=True),
                alpha * acc + dot(v_t, p.astype(BF16)))

    init = (jnp.full((1, nq), -3e38, F32), jnp.zeros((1, nq), F32), jnp.zeros((HEAD_DIM, nq), F32))

    def slc_chunk(c, carry):
        c0 = pl.multiple_of(c * KEY_CHUNK, KEY_CHUNK)
        return online(carry, scores(ks_ref, c0, KEY_CHUNK, q_slc), vst_ref[0, :, pl.ds(c0, KEY_CHUNK)])

    n_full = t0 // KEY_CHUNK
    carry = lax.fori_loop(0, n_full, slc_chunk, init)
    c0 = pl.multiple_of(n_full * KEY_CHUNK, KEY_CHUNK)
    kpos = c0 + lax.broadcasted_iota(jnp.int32, (KEY_CHUNK, 1), 0)
    s_d = jnp.where(kpos <= qpos, scores(ks_ref, c0, KEY_CHUNK, q_slc), -MASK_BIG)
    _, l_s, acc_s = online(carry, s_d, vst_ref[0, :, pl.ds(c0, KEY_CHUNK)])
    o_slc = acc_s / jnp.maximum(l_s, TINY)

    span = WINDOW + tq
    start = pl.multiple_of(jnp.maximum(t0 - WINDOW, 0), LANES)
    dist_w = qpos - (start + lax.broadcasted_iota(jnp.int32, (span, 1), 0))
    s_w = jnp.where((dist_w >= 0) & (dist_w < WINDOW), scores(kw_ref, start, span, q_win), -MASK_BIG)
    _, l_w, acc_w = online(init, s_w, vwt_ref[0, :, pl.ds(start, span)])
    o_win = acc_w / jnp.maximum(l_w, TINY)

    gt_ref[...] = _sigmoid(gate_ref[...]).T
    out = jnp.zeros((HEAD_DIM, nq), F32)
    for c, o in enumerate((o_cmp, o_slc, o_win)):
        g4 = gt_ref[pl.ds(c * NSA_HEADS + g * hg, hg), :]
        out = out + jnp.concatenate([g4[h:h + 1, :] for h in range(hg)], axis=1) * o
    o_ref[...] = jnp.concatenate([out[:, h * tq:(h + 1) * tq].T for h in range(hg)], axis=1)


def _nsa(proj, kcvc, ks, vst, kw, vwt, overlap_t, feat):
    tq = 128
    nq = SEQ // tq
    G = NSA_KV_HEADS
    kv_n = pl.BlockSpec((1, SEQ, HEAD_DIM), lambda b, g, i: (b * G + g, 0, 0))
    kv_t = pl.BlockSpec((1, HEAD_DIM, SEQ), lambda b, g, i: (b * G + g, 0, 0))
    cmp_spec = lambda c: pl.BlockSpec((1, 1, N_CMP_PAD, HEAD_DIM), lambda b, g, i, c=c: (c, b * G + g, 0, 0))
    return pl.pallas_call(
        functools.partial(_nsa_kernel, tq=tq),
        grid=(BATCH, G, nq),
        in_specs=[pl.BlockSpec((tq, NSA_GROUP * HEAD_DIM), lambda b, g, i: (b * nq + i, COL_Q // 256 + g)),
                  cmp_spec(0), cmp_spec(1), kv_n, kv_t, kv_n, kv_t,
                  pl.BlockSpec((tq, LANES), lambda b, g, i: (b * nq + i, COL_NG // LANES)),
                  pl.BlockSpec((N_SEL, N_CMP_PAD), lambda b, g, i: (0, 0)),
                  pl.BlockSpec((SEQ, N_FEAT), lambda b, g, i: (0, 0))],
        out_specs=pl.BlockSpec((tq, NSA_GROUP * HEAD_DIM), lambda b, g, i: (b * nq + i, g)),
        out_shape=jax.ShapeDtypeStruct((ROWS, NSA_Q_WIDTH), F32),
        scratch_shapes=[pltpu.VMEM((LANES, tq), F32)],
        compiler_params=_params(("parallel", "parallel", "parallel")),
        name="nsa",
    )(proj, kcvc, kcvc, ks, vst, kw, vwt, proj, overlap_t, feat)


def _mix_kernel(x_ref, ya_ref, yb_ref, ga_ref, gb_ref, wa_ref, wb_ref, wo_ref, h_ref):
    ma = _dot(ya_ref[...], wa_ref[...])
    mb = _dot(yb_ref[...], wb_ref[...])
    mixed = _sigmoid(ga_ref[...]) * ma + _sigmoid(gb_ref[...]) * mb
    h_ref[...] = x_ref[...] + _dot(mixed, wo_ref[...])


def _mix(x2, ya, yb, proj, wa, wb, wo):
    tm = 256
    row = lambda w, c=0: pl.BlockSpec((tm, w), lambda i, c=c: (i, c))
    const = lambda s: pl.BlockSpec(s, lambda i: (0, 0), pipeline_mode=pl.Buffered(1))
    return pl.pallas_call(
        _mix_kernel,
        grid=(ROWS // tm,),
        in_specs=[row(D_MODEL), row(1024), row(1024), row(D_MODEL, COL_GA // D_MODEL), row(D_MODEL, COL_GB // D_MODEL),
                  const((1024, D_MODEL)), const((1024, D_MODEL)), const((D_MODEL, D_MODEL))],
        out_specs=row(D_MODEL),
        out_shape=jax.ShapeDtypeStruct((ROWS, D_MODEL), F32),
        compiler_params=_params(("parallel",)),
        name="mix",
    )(x2, ya, yb, proj, proj, wa, wb, wo)


def _mlp_kernel(h_ref, gm_ref, wu_ref, wd_ref, gf_ref, o_ref, hn_ref, acc_ref):
    f = pl.program_id(1)

    @pl.when(f == 0)
    def _():
        h = h_ref[...]
        ms = jnp.mean(h * h, axis=-1, keepdims=True)
        hn_ref[...] = (h * lax.rsqrt(ms + NORM_EPS) * gm_ref[...]).astype(BF16)
        acc_ref[...] = jnp.zeros_like(acc_ref)

    u = jnp.maximum(jnp.dot(hn_ref[...], wu_ref[...], preferred_element_type=F32), 0.0)
    acc_ref[...] += jnp.dot((u * u).astype(BF16), wd_ref[...], preferred_element_type=F32)

    @pl.when(f == pl.num_programs(1) - 1)
    def _():
        h2 = h_ref[...] + acc_ref[...]
        ms = jnp.mean(h2 * h2, axis=-1, keepdims=True)
        o_ref[...] = h2 * lax.rsqrt(ms + NORM_EPS) * gf_ref[...]


def _mlp(h, g_mlp, w_up, w_down, g_final):
    tm, tf = 512, 512
    return pl.pallas_call(
        _mlp_kernel,
        grid=(ROWS // tm, D_FF // tf),
        in_specs=[pl.BlockSpec((tm, D_MODEL), lambda i, f: (i, 0)),
                  pl.BlockSpec((1, D_MODEL), lambda i, f: (0, 0)),
                  pl.BlockSpec((D_MODEL, tf), lambda i, f: (0, f)),
                  pl.BlockSpec((tf, D_MODEL), lambda i, f: (f, 0)),
                  pl.BlockSpec((1, D_MODEL), lambda i, f: (0, 0))],
        out_specs=pl.BlockSpec((tm, D_MODEL), lambda i, f: (i, 0)),
        out_shape=jax.ShapeDtypeStruct((ROWS, D_MODEL), F32),
        scratch_shapes=[pltpu.VMEM((tm, D_MODEL), BF16), pltpu.VMEM((tm, D_MODEL), F32)],
        compiler_params=_params(("parallel", "arbitrary")),
        name="mlp",
    )(h, g_mlp, w_up, w_down, g_final)


def _pack_cols(w, pad):
    return jnp.concatenate([
        w[..., 6128:10224],
        w[..., 0:3168], pad(32),
        w[..., 3168:3264], pad(32),
        w[..., 3264:3520],
        w[..., 3520:6080],
        w[..., 6080:6128], pad(80),
    ], axis=-1)


def _overlap_matrix():
    cs = np.arange(N_CMP)[:, None] * CMP_STRIDE
    ss = np.arange(N_SEL)[None, :] * SEL_BLOCK
    ov = np.clip(np.minimum(cs + CMP_BLOCK, ss + SEL_BLOCK) - np.maximum(cs, ss), 0, None) / CMP_BLOCK
    out = np.zeros((N_CMP_PAD, N_SEL), np.float32)
    out[:N_CMP] = ov
    return out


def kernel(x, norm_mix, w_in, rwkv_mu, rwkv_w0, rwkv_w_up, rwkv_a0, rwkv_a_up, rwkv_g_up, rwkv_k_k, rwkv_k_a,
           rwkv_r_k, rwkv_lnx_w, rwkv_lnx_b, cmp_pe_k, cmp_w1_k, cmp_w2_k, cmp_pe_v, cmp_w1_v, cmp_w2_v,
           w_out_rwkv, w_out_nsa, w_o, norm_mlp, mlp_w_up, mlp_w_down, norm_final):
    assert x.shape == (BATCH, SEQ, D_MODEL) and w_in.shape[0] == 1
    l = 0
    G = NSA_KV_HEADS
    x2 = x.reshape(ROWS, D_MODEL)
    row = lambda a: a.reshape(1, -1)

    w_p = _pack_cols(w_in[l], lambda n: jnp.zeros((D_MODEL, n), F32)).astype(BF16)
    proj = _inproj(x2, row(norm_mix[l]), w_p)

    mu = rwkv_mu[l]
    mu_p = jnp.concatenate([mu[0:3168], jnp.zeros((32,), F32), mu[3168:3264], jnp.zeros((32,), F32), mu[3264:3520]])
    zrows = jnp.zeros((32, RWKV_WIDTH), F32)
    ones = jnp.asarray(np.kron(np.eye(RWKV_HEADS), np.ones((HEAD_DIM, HEAD_DIM))), BF16)
    r, w, k, v, kk, b, g, bonus = _rwkvprep(
        proj, row(mu_p), row(rwkv_w0[l]), jnp.concatenate([rwkv_w_up[l], zrows]),
        row(rwkv_a0[l]), jnp.concatenate([rwkv_a_up[l], zrows]), rwkv_g_up[l],
        row(rwkv_k_k[l]), row(rwkv_k_a[l]), row(rwkv_r_k[l]), ones)
    kkn = jnp.concatenate([_to_scan(kk)[1:], jnp.zeros((1, N_JPAIR, LANES), F32)])
    y = _scan(kkn, _to_scan(w), _to_scan(b), _to_scan(k), _to_scan(r), _to_scan(v))
    ya = _rwkvpost(_from_scan(y), bonus, g, row(rwkv_lnx_w[l]), row(rwkv_lnx_b[l]), ones)

    kv = lambda c0: proj[:, c0:c0 + NSA_KV_WIDTH].reshape(BATCH, SEQ, G, HEAD_DIM)
    chunks = jnp.stack([kv(COL_KC), kv(COL_VC)]).transpose(0, 1, 3, 2, 4)
    chunks = chunks.reshape(2, BATCH * G, N_CMP_PAD, CMP_STRIDE * HEAD_DIM)
    kcvc = _compress(chunks, jnp.stack([cmp_w1_k[l], cmp_w1_v[l]]),
                     jnp.stack([cmp_pe_k[l].reshape(1, -1), cmp_pe_v[l].reshape(1, -1)]),
                     jnp.stack([cmp_w2_k[l], cmp_w2_v[l]]))
    kv_b = lambda c0: lax.optimization_barrier(proj[:, c0:c0 + NSA_KV_WIDTH]).reshape(BATCH, SEQ, G, HEAD_DIM)
    keys = lambda c0: kv_b(c0).transpose(0, 2, 1, 3).reshape(BATCH * G, SEQ, HEAD_DIM).astype(BF16)
    vals_t = lambda c0: kv_b(c0).transpose(0, 2, 3, 1).reshape(BATCH * G, HEAD_DIM, SEQ).astype(BF16)
    yb = _nsa(proj, kcvc, keys(COL_KS), vals_t(COL_VS), keys(COL_KW), vals_t(COL_VW),
              jnp.asarray(_overlap_matrix().T, BF16), jnp.asarray(_key_features(), BF16))

    h = _mix(x2, ya, yb, proj, w_out_rwkv[l].astype(BF16), w_out_nsa[l].astype(BF16), w_o[l].astype(BF16))
    out = _mlp(h, row(norm_mlp[l]), mlp_w_up[l].astype(BF16), mlp_w_down[l].astype(BF16), row(norm_final))
    return out.reshape(BATCH, SEQ, D_MODEL)
```

```python
import functools

import numpy as np
import jax
import jax.numpy as jnp
from jax import lax
from jax.experimental import pallas as pl
from jax.experimental.pallas import tpu as pltpu

F32 = jnp.float32
BF16 = jnp.bfloat16

D_MODEL = 2048
BATCH = 4
SEQ = 2048
ROWS = BATCH * SEQ
RWKV_HEADS = 16
HEAD_DIM = 64
RWKV_WIDTH = RWKV_HEADS * HEAD_DIM
LORA_DECAY = 96
LORA_ICLR = 96
LORA_GATE = 256
GN_EPS = 64e-5
NSA_HEADS = 16
NSA_KV_HEADS = 4
NSA_GROUP = NSA_HEADS // NSA_KV_HEADS
NSA_Q_WIDTH = NSA_HEADS * HEAD_DIM
NSA_KV_WIDTH = NSA_KV_HEADS * HEAD_DIM
CMP_BLOCK = 32
CMP_STRIDE = 16
CMP_HIDDEN = 256
N_CMP = (SEQ - CMP_BLOCK) // CMP_STRIDE + 1
N_CMP_PAD = 128
SEL_BLOCK = 64
SEL_TOPN = 16
N_SEL = SEQ // SEL_BLOCK
WINDOW = 512
D_FF = 4 * D_MODEL
NORM_EPS = 1e-5
NEG_INF = -1e30
TINY = 1e-30

LANES = 128
VMEM_LIMIT = 56 * 1024 * 1024

COL_GA = 0
COL_GB = 2048
COL_R = 4096
COL_K = 5120
COL_V = 6144
COL_LORA = 7168
COL_Q = 7680
COL_KC = 8704
COL_VC = 8960
COL_KS = 9216
COL_VS = 9472
COL_KW = 9728
COL_VW = 9984
COL_NG = 10240
PROJ_COLS = 10368
LORA_PACK = 512


def _dot(a, b):
    return jnp.dot(a.astype(BF16), b.astype(BF16), preferred_element_type=F32)


def _sigmoid(x):
    return 1.0 / (1.0 + jnp.exp(-x))


def _params(sem, limit=VMEM_LIMIT):
    return pltpu.CompilerParams(dimension_semantics=sem, vmem_limit_bytes=limit)


def _inproj_kernel(x_ref, g_ref, w_ref, o_ref, xn_ref):
    @pl.when(pl.program_id(1) == 0)
    def _():
        x = x_ref[...]
        ms = jnp.mean(x * x, axis=-1, keepdims=True)
        xn_ref[...] = (x * lax.rsqrt(ms + NORM_EPS) * g_ref[...]).astype(BF16)

    o_ref[...] = jnp.dot(xn_ref[...], w_ref[...], preferred_element_type=F32)


def _inproj(x2, g, w_p):
    tm, tn = 1024, 1152
    return pl.pallas_call(
        _inproj_kernel,
        grid=(ROWS // tm, PROJ_COLS // tn),
        in_specs=[pl.BlockSpec((tm, D_MODEL), lambda i, j: (i, 0)),
                  pl.BlockSpec((1, D_MODEL), lambda i, j: (0, 0)),
                  pl.BlockSpec((D_MODEL, tn), lambda i, j: (0, j))],
        out_specs=pl.BlockSpec((tm, tn), lambda i, j: (i, j)),
        out_shape=jax.ShapeDtypeStruct((ROWS, PROJ_COLS), F32),
        scratch_shapes=[pltpu.VMEM((tm, D_MODEL), BF16)],
        compiler_params=_params(("parallel", "arbitrary")),
        name="inproj",
    )(x2, g, w_p)


def _head_sum(x, ones_ref):
    hi = x.astype(BF16)
    lo = (x - hi.astype(F32)).astype(BF16)
    ones = ones_ref[...]
    return (jnp.dot(hi, ones, preferred_element_type=F32)
            + jnp.dot(lo, ones, preferred_element_type=F32))


def _rwkvprep_kernel(r_ref, k_ref, v_ref, lo_ref, rp_ref, kp_ref, vp_ref, lop_ref,
                     mu_ref, w0_ref, wup_ref, a0_ref, aup_ref, gup_ref, kk_ref, ka_ref, rk_ref, ones_ref,
                     ro_ref, wo_ref, ko_ref, vo_ref, kko_ref, bo_ref, go_ref, bon_ref, *, tm):
    i = pl.program_id(0)
    has_prev = jnp.where((i * tm) % SEQ == 0, 0.0, 1.0).astype(F32)

    def shift(cur_ref, prev_ref, mu):
        z = cur_ref[...]
        zp = pltpu.roll(z, 1, axis=0)
        prev_row = prev_ref[7:8, :] * has_prev
        row = lax.broadcasted_iota(jnp.int32, z.shape, 0)
        zp = jnp.where(row == 0, prev_row, zp)
        return z + (zp - z) * mu

    r = shift(r_ref, rp_ref, mu_ref[:, 0:1024])
    k = shift(k_ref, kp_ref, mu_ref[:, 1024:2048])
    v = shift(v_ref, vp_ref, mu_ref[:, 2048:3072])
    lo = shift(lo_ref, lop_ref, mu_ref[:, 3072:3584])
    w_lo, a_lo, g_lo = lo[:, 0:128], lo[:, 128:256], lo[:, 256:512]

    u = w0_ref[...] + _dot(jnp.tanh(w_lo), wup_ref[...])
    decay = jnp.exp(-(_sigmoid(u) * float(np.exp(-0.5))))
    a = _sigmoid(a0_ref[...] + _dot(a_lo, aup_ref[...]))
    g = _dot(_sigmoid(g_lo), gup_ref[...])

    kk = k * kk_ref[...]
    ss = _head_sum(kk * kk, ones_ref)
    kk = kk * lax.rsqrt(jnp.maximum(ss, 1e-24))
    k = k * (1.0 + (a - 1.0) * ka_ref[...])
    coef = _head_sum(r * k * rk_ref[...], ones_ref)

    ro_ref[...] = r
    wo_ref[...] = decay
    ko_ref[...] = k
    vo_ref[...] = v
    kko_ref[...] = kk
    bo_ref[...] = kk * a
    go_ref[...] = g
    bon_ref[...] = coef * v


def _rwkvprep(proj, mu_p, w0, wup_p, a0, aup_p, gup, k_k, k_a, r_k, ones):
    tm = 256
    pb = tm // 8
    cur = lambda c: pl.BlockSpec((tm, 1024), lambda i, c=c: (i, c))
    prev = lambda c: pl.BlockSpec((8, 1024), lambda i, c=c: (jnp.maximum(i * pb - 1, 0), c))
    full = lambda s: pl.BlockSpec(s, lambda i: (0,) * len(s))
    out = pl.BlockSpec((tm, 1024), lambda i: (i, 0))
    osh = jax.ShapeDtypeStruct((ROWS, RWKV_WIDTH), F32)
    return pl.pallas_call(
        functools.partial(_rwkvprep_kernel, tm=tm),
        grid=(ROWS // tm,),
        in_specs=[cur(COL_R // 1024), cur(COL_K // 1024), cur(COL_V // 1024),
                  pl.BlockSpec((tm, LORA_PACK), lambda i: (i, COL_LORA // LORA_PACK)),
                  prev(COL_R // 1024), prev(COL_K // 1024), prev(COL_V // 1024),
                  pl.BlockSpec((8, LORA_PACK), lambda i: (jnp.maximum(i * pb - 1, 0), COL_LORA // LORA_PACK)),
                  full((1, 3584)), full((1, 1024)), full((128, 1024)), full((1, 1024)), full((128, 1024)),
                  full((256, 1024)), full((1, 1024)), full((1, 1024)), full((1, 1024)), full((1024, 1024))],
        out_specs=[out] * 8,
        out_shape=[osh] * 8,
        compiler_params=_params(("parallel",)),
        name="rwkvprep",
    )(proj, proj, proj, proj, proj, proj, proj, proj,
      mu_p, w0, wup_p, a0, aup_p, gup, k_k, k_a, r_k, ones)


N_JPAIR = HEAD_DIM // 2
N_JOPS = 5


def _scan_kernel(kkn_ref, w_ref, b_ref, k_ref, r_ref, v_ref, y_ref, s_ref, sa_ref, d_ref, *, tc):
    @pl.when(pl.program_id(0) == 0)
    def _():
        s_ref[...] = jnp.zeros_like(s_ref)
        sa_ref[...] = jnp.zeros_like(sa_ref)

    lo_half = lax.broadcasted_iota(jnp.int32, (N_JPAIR, tc, LANES), 2) < 64
    for n, ref in enumerate((kkn_ref, w_ref, b_ref, k_ref, r_ref)):
        x = ref[...]
        xr = pltpu.roll(x, 64, axis=2)
        d_ref[n, 0:N_JPAIR] = jnp.where(lo_half, x, xr)
        d_ref[n, N_JPAIR:HEAD_DIM] = jnp.where(lo_half, xr, x)

    def tree(parts):
        while len(parts) > 1:
            parts = [parts[n] + parts[n + 1] for n in range(0, len(parts), 2)]
        return parts[0]

    def step(t, sa):
        vt = v_ref[t]
        acc_y = [None] * 4
        acc_s = [None] * 4
        for j in range(HEAD_DIM):
            row = lambda n: d_ref[n, j, pl.ds(t, 1), :]
            sn = s_ref[j] * row(1) - sa * row(2) + vt * row(3)
            s_ref[j] = sn
            ty = sn * row(4)
            ts = sn * row(0)
            acc_y[j % 4] = ty if acc_y[j % 4] is None else acc_y[j % 4] + ty
            acc_s[j % 4] = ts if acc_s[j % 4] is None else acc_s[j % 4] + ts
        y_ref[t] = tree(acc_y)
        return tree(acc_s)

    sa_ref[...] = lax.fori_loop(0, tc, step, sa_ref[...])


def _scan(kkn, w, b, k, r, v):
    tc = 64
    spec = pl.BlockSpec((tc, N_JPAIR, LANES), lambda i: (i, 0, 0))
    jspec = pl.BlockSpec((N_JPAIR, tc, LANES), lambda i: (0, i, 0))
    return pl.pallas_call(
        functools.partial(_scan_kernel, tc=tc),
        grid=(SEQ // tc,),
        in_specs=[jspec] * N_JOPS + [spec],
        out_specs=spec,
        out_shape=jax.ShapeDtypeStruct((SEQ, N_JPAIR, LANES), F32),
        scratch_shapes=[pltpu.VMEM((HEAD_DIM, N_JPAIR, LANES), F32),
                        pltpu.VMEM((N_JPAIR, LANES), F32),
                        pltpu.VMEM((N_JOPS, HEAD_DIM, tc, LANES), F32)],
        compiler_params=_params(("arbitrary",)),
        name="scan",
    )(kkn, w, b, k, r, v)


RELAYOUT_T = 128


def _relayout_kernel(x_ref, o_ref, z_ref):
    for b in range(BATCH):
        z_ref[b] = x_ref[b].T
    for p in range(N_JPAIR):
        tile = jnp.concatenate([z_ref[b, pl.ds(half * N_JPAIR + p, RWKV_HEADS, stride=HEAD_DIM), :]
                                for half in range(2) for b in range(BATCH)], axis=0)
        o_ref[p] = tile.T


def _relayout(a):
    return pl.pallas_call(
        _relayout_kernel,
        grid=(SEQ // RELAYOUT_T,),
        in_specs=[pl.BlockSpec((BATCH, RELAYOUT_T, RWKV_WIDTH), lambda i: (0, i, 0))],
        out_specs=pl.BlockSpec((N_JPAIR, RELAYOUT_T, LANES), lambda i: (0, i, 0)),
        out_shape=jax.ShapeDtypeStruct((N_JPAIR, SEQ, LANES), F32),
        scratch_shapes=[pltpu.VMEM((BATCH, RWKV_WIDTH, RELAYOUT_T), F32)],
        compiler_params=_params(("parallel",)),
        name="relayout",
    )(a.reshape(BATCH, SEQ, RWKV_WIDTH))


def _to_scan(a):
    a = a.reshape(BATCH, SEQ, RWKV_HEADS, 2, N_JPAIR).transpose(1, 4, 3, 0, 2)
    return a.reshape(SEQ, N_JPAIR, LANES)


def _from_scan(y):
    y = y.reshape(SEQ, N_JPAIR, 2, BATCH, RWKV_HEADS).transpose(3, 0, 4, 2, 1)
    return y.reshape(ROWS, RWKV_WIDTH)


def _rwkvpost_kernel(y_ref, bon_ref, g_ref, lw_ref, lb_ref, ones_ref, o_ref):
    y = y_ref[...]
    mean = _head_sum(y, ones_ref) * (1.0 / HEAD_DIM)
    d = y - mean
    var = _head_sum(d * d, ones_ref) * (1.0 / HEAD_DIM)
    yn = d * lax.rsqrt(var + GN_EPS) * lw_ref[...] + lb_ref[...]
    o_ref[...] = (yn + bon_ref[...]) * g_ref[...]


def _rwkvpost(y, bonus, g, lnx_w, lnx_b, ones):
    tm = 512
    blk = pl.BlockSpec((tm, 1024), lambda i: (i, 0))
    vec = pl.BlockSpec((1, 1024), lambda i: (0, 0))
    return pl.pallas_call(
        _rwkvpost_kernel,
        grid=(ROWS // tm,),
        in_specs=[blk, blk, blk, vec, vec, pl.BlockSpec((1024, 1024), lambda i: (0, 0))],
        out_specs=blk,
        out_shape=jax.ShapeDtypeStruct((ROWS, RWKV_WIDTH), F32),
        compiler_params=_params(("parallel",)),
        name="rwkvpost",
    )(y, bonus, g, lnx_w, lnx_b, ones)


def _compress_kernel(c_ref, w1_ref, pe_ref, w2_ref, o_ref):
    ch = c_ref[0, 0]
    w1 = w1_ref[0]
    half = CMP_STRIDE * HEAD_DIM
    top = _dot(ch, w1[:half])
    bot = _dot(ch, w1[half:])
    bias = _dot(jnp.broadcast_to(pe_ref[0], (8, CMP_BLOCK * HEAD_DIM)), w1)[0:1]
    hid = top + pltpu.roll(bot, N_CMP_PAD - 1, axis=0) + bias
    c0 = float(np.sqrt(2.0 / np.pi))
    act = 0.5 * hid * (1.0 + jnp.tanh(c0 * (hid + 0.044715 * (hid * hid * hid))))
    o_ref[0, 0] = _dot(act, w2_ref[0])


def _compress(chunks, w1, pe, w2):
    nbg = BATCH * NSA_KV_HEADS
    return pl.pallas_call(
        _compress_kernel,
        grid=(2, nbg),
        in_specs=[pl.BlockSpec((1, 1, N_CMP_PAD, CMP_STRIDE * HEAD_DIM), lambda c, n: (c, n, 0, 0)),
                  pl.BlockSpec((1, CMP_BLOCK * HEAD_DIM, CMP_HIDDEN), lambda c, n: (c, 0, 0)),
                  pl.BlockSpec((1, 1, CMP_BLOCK * HEAD_DIM), lambda c, n: (c, 0, 0)),
                  pl.BlockSpec((1, CMP_HIDDEN, HEAD_DIM), lambda c, n: (c, 0, 0))],
        out_specs=pl.BlockSpec((1, 1, N_CMP_PAD, HEAD_DIM), lambda c, n: (c, n, 0, 0)),
        out_shape=jax.ShapeDtypeStruct((2, nbg, N_CMP_PAD, HEAD_DIM), F32),
        compiler_params=_params(("parallel", "parallel")),
        name="compress",
    )(chunks, w1, pe, w2)


MASK_BIG = float(2.0 ** 100)
KEY_CHUNK = 512
N_FEAT = 64


def _key_features():
    s = np.arange(SEQ)
    f = np.zeros((SEQ, N_FEAT), np.float32)
    f[s, s // SEL_BLOCK] = 1.0
    f[:, 32:35] = (SEL_BLOCK * (s // SEL_BLOCK))[:, None]
    f[:, 35:38] = (s % SEL_BLOCK)[:, None]
    return f


def _split3(x):
    a1 = x.astype(BF16)
    r1 = x - a1.astype(F32)
    a2 = r1.astype(BF16)
    a3 = (r1 - a2.astype(F32)).astype(BF16)
    return a1, a2, a3


def _nsa_kernel(q_ref, kc_ref, vc_ref, ks_ref, vst_ref, kw_ref, vwt_ref, gate_ref, ovt_ref, feat_ref, o_ref,
                gt_ref, *, tq):
    g = pl.program_id(1)
    t0 = pl.program_id(2) * tq
    hg = NSA_GROUP
    nq = hg * tq
    dot = functools.partial(jnp.dot, preferred_element_type=F32)

    qt = (q_ref[...] * HEAD_DIM ** -0.5).T
    qt = jnp.concatenate([qt[h * HEAD_DIM:(h + 1) * HEAD_DIM] for h in range(hg)], axis=1).astype(BF16)
    lane = lax.broadcasted_iota(jnp.int32, (1, nq), 1)
    slope = jnp.exp2(-0.5 * (g * hg + lane // tq + 1).astype(F32))
    qpos = t0 + lane % tq

    n_idx = lax.broadcasted_iota(jnp.int32, (N_CMP_PAD, 1), 0)
    dist_c = qpos.astype(F32) - (n_idx * CMP_STRIDE + (CMP_BLOCK - 1)).astype(F32)
    mask_c = (dist_c >= 0) & (n_idx < N_CMP)
    x_c = jnp.where(mask_c, dot(kc_ref[0, 0].astype(BF16), qt) - slope * dist_c, NEG_INF)
    p_c = jnp.where(mask_c, jnp.exp(x_c - jnp.max(x_c, axis=0, keepdims=True)), 0.0)
    den_c = jnp.maximum(jnp.sum(p_c, axis=0, keepdims=True), TINY)
    o_cmp = dot(vc_ref[0, 0].T.astype(BF16), p_c.astype(BF16)) / den_c

    p_n = p_c / den_c
    p_sum = (p_n[:, 0:tq] + p_n[:, tq:2 * tq]) + (p_n[:, 2 * tq:3 * tq] + p_n[:, 3 * tq:4 * tq])
    ovt = ovt_ref[...]
    imp = sum(dot(ovt, part) for part in _split3(p_sum))
    blk = lax.broadcasted_iota(jnp.int32, (N_SEL, tq), 0)
    cur = (t0 + lax.broadcasted_iota(jnp.int32, (N_SEL, tq), 1)) // SEL_BLOCK
    forced = (blk == 0) | (blk == cur) | (blk == cur - 1)
    score = jnp.where(blk > cur, NEG_INF, jnp.where(forced, -NEG_INF, imp))
    rank = jnp.zeros((N_SEL, tq), F32)
    for jp in range(N_SEL):
        row = score[jp:jp + 1, :]
        beats = (row > score) | ((row == score) & (blk > jp))
        rank = rank + jnp.where(beats, 1.0, 0.0)
    sel_bias = jnp.where((rank < SEL_TOPN) & (blk <= cur), 0.0, -MASK_BIG)
    sel_bias = jnp.concatenate([sel_bias] * hg, axis=1)

    s1, s2, s3 = (piece.astype(F32) for piece in _split3(slope))
    r8 = lax.broadcasted_iota(jnp.int32, (8, nq), 0)
    srow = jnp.where((r8 == 0) | (r8 == 3), s1, jnp.where((r8 == 1) | (r8 == 4), s2, s3))
    srow = jnp.where(r8 < 6, srow, 0.0)
    pad = jnp.zeros((N_FEAT - N_SEL - 8, nq), F32)
    q_slc = jnp.concatenate([qt, jnp.concatenate([sel_bias, srow, pad], axis=0).astype(BF16)], axis=0)
    q_win = jnp.concatenate([qt, jnp.concatenate([jnp.zeros_like(sel_bias), srow, pad], axis=0).astype(BF16)],
                            axis=0)

    def scores(k_ref, start, size, q_aug):
        keys = jnp.concatenate([k_ref[0, pl.ds(start, size), :], feat_ref[pl.ds(start, size), :]], axis=1)
        return dot(keys, q_aug)

    def online(carry, s, v_t):
        m, l, acc = carry
        m_new = jnp.maximum(m, jnp.max(s, axis=0, keepdims=True))
        alpha = jnp.exp(m - m_new)
        p = jnp.exp(s - m_new)
        return (m_new, alpha * l + jnp.sum(p, axis=0, keepdims=True),
                alpha * acc + dot(v_t, p.astype(BF16)))

    init = (jnp.full((1, nq), -3e38, F32), jnp.zeros((1, nq), F32), jnp.zeros((HEAD_DIM, nq), F32))

    def slc_chunk(c, carry):
        c0 = pl.multiple_of(c * KEY_CHUNK, KEY_CHUNK)
        return online(carry, scores(ks_ref, c0, KEY_CHUNK, q_slc), vst_ref[0, :, pl.ds(c0, KEY_CHUNK)])

    n_full = t0 // KEY_CHUNK
    carry = lax.fori_loop(0, n_full, slc_chunk, init)
    c0 = pl.multiple_of(n_full * KEY_CHUNK, KEY_CHUNK)
    kpos = c0 + lax.broadcasted_iota(jnp.int32, (KEY_CHUNK, 1), 0)
    s_d = jnp.where(kpos <= qpos, scores(ks_ref, c0, KEY_CHUNK, q_slc), -MASK_BIG)
    _, l_s, acc_s = online(carry, s_d, vst_ref[0, :, pl.ds(c0, KEY_CHUNK)])
    o_slc = acc_s / jnp.maximum(l_s, TINY)

    span = WINDOW + tq
    start = pl.multiple_of(jnp.maximum(t0 - WINDOW, 0), LANES)
    dist_w = qpos - (start + lax.broadcasted_iota(jnp.int32, (span, 1), 0))
    s_w = jnp.where((dist_w >= 0) & (dist_w < WINDOW), scores(kw_ref, start, span, q_win), -MASK_BIG)
    _, l_w, acc_w = online(init, s_w, vwt_ref[0, :, pl.ds(start, span)])
    o_win = acc_w / jnp.maximum(l_w, TINY)

    gt_ref[...] = _sigmoid(gate_ref[...]).T
    out = jnp.zeros((HEAD_DIM, nq), F32)
    for c, o in enumerate((o_cmp, o_slc, o_win)):
        g4 = gt_ref[pl.ds(c * NSA_HEADS + g * hg, hg), :]
        out = out + jnp.concatenate([g4[h:h + 1, :] for h in range(hg)], axis=1) * o
    o_ref[...] = jnp.concatenate([out[:, h * tq:(h + 1) * tq].T for h in range(hg)], axis=1)


def _nsa(proj, kcvc, ks, vst, kw, vwt, overlap_t, feat):
    tq = 128
    nq = SEQ // tq
    G = NSA_KV_HEADS
    kv_n = pl.BlockSpec((1, SEQ, HEAD_DIM), lambda b, g, i: (b * G + g, 0, 0))
    kv_t = pl.BlockSpec((1, HEAD_DIM, SEQ), lambda b, g, i: (b * G + g, 0, 0))
    cmp_spec = lambda c: pl.BlockSpec((1, 1, N_CMP_PAD, HEAD_DIM), lambda b, g, i, c=c: (c, b * G + g, 0, 0))
    return pl.pallas_call(
        functools.partial(_nsa_kernel, tq=tq),
        grid=(BATCH, G, nq),
        in_specs=[pl.BlockSpec((tq, NSA_GROUP * HEAD_DIM), lambda b, g, i: (b * nq + i, COL_Q // 256 + g)),
                  cmp_spec(0), cmp_spec(1), kv_n, kv_t, kv_n, kv_t,
                  pl.BlockSpec((tq, LANES), lambda b, g, i: (b * nq + i, COL_NG // LANES)),
                  pl.BlockSpec((N_SEL, N_CMP_PAD), lambda b, g, i: (0, 0)),
                  pl.BlockSpec((SEQ, N_FEAT), lambda b, g, i: (0, 0))],
        out_specs=pl.BlockSpec((tq, NSA_GROUP * HEAD_DIM), lambda b, g, i: (b * nq + i, g)),
        out_shape=jax.ShapeDtypeStruct((ROWS, NSA_Q_WIDTH), F32),
        scratch_shapes=[pltpu.VMEM((LANES, tq), F32)],
        compiler_params=_params(("parallel", "parallel", "parallel")),
        name="nsa",
    )(proj, kcvc, kcvc, ks, vst, kw, vwt, proj, overlap_t, feat)


def _mix_kernel(x_ref, ya_ref, yb_ref, ga_ref, gb_ref, wa_ref, wb_ref, wo_ref, h_ref):
    ma = _dot(ya_ref[...], wa_ref[...])
    mb = _dot(yb_ref[...], wb_ref[...])
    mixed = _sigmoid(ga_ref[...]) * ma + _sigmoid(gb_ref[...]) * mb
    h_ref[...] = x_ref[...] + _dot(mixed, wo_ref[...])


def _mix(x2, ya, yb, proj, wa, wb, wo):
    tm = 256
    row = lambda w, c=0: pl.BlockSpec((tm, w), lambda i, c=c: (i, c))
    const = lambda s: pl.BlockSpec(s, lambda i: (0, 0), pipeline_mode=pl.Buffered(1))
    return pl.pallas_call(
        _mix_kernel,
        grid=(ROWS // tm,),
        in_specs=[row(D_MODEL), row(1024), row(1024), row(D_MODEL, COL_GA // D_MODEL), row(D_MODEL, COL_GB // D_MODEL),
                  const((1024, D_MODEL)), const((1024, D_MODEL)), const((D_MODEL, D_MODEL))],
        out_specs=row(D_MODEL),
        out_shape=jax.ShapeDtypeStruct((ROWS, D_MODEL), F32),
        compiler_params=_params(("parallel",)),
        name="mix",
    )(x2, ya, yb, proj, proj, wa, wb, wo)


def _mlp_kernel(h_ref, gm_ref, wu_ref, wd_ref, gf_ref, o_ref, hn_ref, acc_ref):
    f = pl.program_id(1)

    @pl.when(f == 0)
    def _():
        h = h_ref[...]
        ms = jnp.mean(h * h, axis=-1, keepdims=True)
        hn_ref[...] = (h * lax.rsqrt(ms + NORM_EPS) * gm_ref[...]).astype(BF16)
        acc_ref[...] = jnp.zeros_like(acc_ref)

    u = jnp.maximum(jnp.dot(hn_ref[...], wu_ref[...], preferred_element_type=F32), 0.0)
    acc_ref[...] += jnp.dot((u * u).astype(BF16), wd_ref[...], preferred_element_type=F32)

    @pl.when(f == pl.num_programs(1) - 1)
    def _():
        h2 = h_ref[...] + acc_ref[...]
        ms = jnp.mean(h2 * h2, axis=-1, keepdims=True)
        o_ref[...] = h2 * lax.rsqrt(ms + NORM_EPS) * gf_ref[...]


def _mlp(h, g_mlp, w_up, w_down, g_final):
    tm, tf = 512, 512
    return pl.pallas_call(
        _mlp_kernel,
        grid=(ROWS // tm, D_FF // tf),
        in_specs=[pl.BlockSpec((tm, D_MODEL), lambda i, f: (i, 0)),
                  pl.BlockSpec((1, D_MODEL), lambda i, f: (0, 0)),
                  pl.BlockSpec((D_MODEL, tf), lambda i, f: (0, f)),
                  pl.BlockSpec((tf, D_MODEL), lambda i, f: (f, 0)),
                  pl.BlockSpec((1, D_MODEL), lambda i, f: (0, 0))],
        out_specs=pl.BlockSpec((tm, D_MODEL), lambda i, f: (i, 0)),
        out_shape=jax.ShapeDtypeStruct((ROWS, D_MODEL), F32),
        scratch_shapes=[pltpu.VMEM((tm, D_MODEL), BF16), pltpu.VMEM((tm, D_MODEL), F32)],
        compiler_params=_params(("parallel", "arbitrary")),
        name="mlp",
    )(h, g_mlp, w_up, w_down, g_final)


def _pack_cols(w, pad):
    return jnp.concatenate([
        w[..., 6128:10224],
        w[..., 0:3168], pad(32),
        w[..., 3168:3264], pad(32),
        w[..., 3264:3520],
        w[..., 3520:6080],
        w[..., 6080:6128], pad(80),
    ], axis=-1)


def _overlap_matrix():
    cs = np.arange(N_CMP)[:, None] * CMP_STRIDE
    ss = np.arange(N_SEL)[None, :] * SEL_BLOCK
    ov = np.clip(np.minimum(cs + CMP_BLOCK, ss + SEL_BLOCK) - np.maximum(cs, ss), 0, None) / CMP_BLOCK
    out = np.zeros((N_CMP_PAD, N_SEL), np.float32)
    out[:N_CMP] = ov
    return out


def kernel(x, norm_mix, w_in, rwkv_mu, rwkv_w0, rwkv_w_up, rwkv_a0, rwkv_a_up, rwkv_g_up, rwkv_k_k, rwkv_k_a,
           rwkv_r_k, rwkv_lnx_w, rwkv_lnx_b, cmp_pe_k, cmp_w1_k, cmp_w2_k, cmp_pe_v, cmp_w1_v, cmp_w2_v,
           w_out_rwkv, w_out_nsa, w_o, norm_mlp, mlp_w_up, mlp_w_down, norm_final):
    assert x.shape == (BATCH, SEQ, D_MODEL) and w_in.shape[0] == 1
    l = 0
    G = NSA_KV_HEADS
    x2 = x.reshape(ROWS, D_MODEL)
    row = lambda a: a.reshape(1, -1)

    w_p = _pack_cols(w_in[l], lambda n: jnp.zeros((D_MODEL, n), F32)).astype(BF16)
    proj = _inproj(x2, row(norm_mix[l]), w_p)

    mu = rwkv_mu[l]
    mu_p = jnp.concatenate([mu[0:3168], jnp.zeros((32,), F32), mu[3168:3264], jnp.zeros((32,), F32), mu[3264:3520]])
    zrows = jnp.zeros((32, RWKV_WIDTH), F32)
    ones = jnp.asarray(np.kron(np.eye(RWKV_HEADS), np.ones((HEAD_DIM, HEAD_DIM))), BF16)
    r, w, k, v, kk, b, g, bonus = _rwkvprep(
        proj, row(mu_p), row(rwkv_w0[l]), jnp.concatenate([rwkv_w_up[l], zrows]),
        row(rwkv_a0[l]), jnp.concatenate([rwkv_a_up[l], zrows]), rwkv_g_up[l],
        row(rwkv_k_k[l]), row(rwkv_k_a[l]), row(rwkv_r_k[l]), ones)
    kkn = jnp.concatenate([kk.reshape(BATCH, SEQ, RWKV_WIDTH)[:, 1:], jnp.zeros((BATCH, 1, RWKV_WIDTH), F32)], axis=1)
    y = _scan(_relayout(kkn), _relayout(w), _relayout(b), _relayout(k), _relayout(r), _to_scan(v))
    ya = _rwkvpost(_from_scan(y), bonus, g, row(rwkv_lnx_w[l]), row(rwkv_lnx_b[l]), ones)

    kv = lambda c0: proj[:, c0:c0 + NSA_KV_WIDTH].reshape(BATCH, SEQ, G, HEAD_DIM)
    chunks = jnp.stack([kv(COL_KC), kv(COL_VC)]).transpose(0, 1, 3, 2, 4)
    chunks = chunks.reshape(2, BATCH * G, N_CMP_PAD, CMP_STRIDE * HEAD_DIM)
    kcvc = _compress(chunks, jnp.stack([cmp_w1_k[l], cmp_w1_v[l]]),
                     jnp.stack([cmp_pe_k[l].reshape(1, -1), cmp_pe_v[l].reshape(1, -1)]),
                     jnp.stack([cmp_w2_k[l], cmp_w2_v[l]]))
    kv_b = lambda c0: lax.optimization_barrier(proj[:, c0:c0 + NSA_KV_WIDTH]).reshape(BATCH, SEQ, G, HEAD_DIM)
    keys = lambda c0: kv_b(c0).transpose(0, 2, 1, 3).reshape(BATCH * G, SEQ, HEAD_DIM).astype(BF16)
    vals_t = lambda c0: kv_b(c0).transpose(0, 2, 3, 1).reshape(BATCH * G, HEAD_DIM, SEQ).astype(BF16)
    yb = _nsa(proj, kcvc, keys(COL_KS), vals_t(COL_VS), keys(COL_KW), vals_t(COL_VW),
              jnp.asarray(_overlap_matrix().T, BF16), jnp.asarray(_key_features(), BF16))

    h = _mix(x2, ya, yb, proj, w_out_rwkv[l].astype(BF16), w_out_nsa[l].astype(BF16), w_o[l].astype(BF16))
    out = _mlp(h, row(norm_mlp[l]), mlp_w_up[l].astype(BF16), mlp_w_down[l].astype(BF16), row(norm_final))
    return out.reshape(BATCH, SEQ, D_MODEL)
```

```python
import functools

import numpy as np
import jax
import jax.numpy as jnp
from jax import lax
from jax.experimental import pallas as pl
from jax.experimental.pallas import tpu as pltpu

F32 = jnp.float32
BF16 = jnp.bfloat16

D_MODEL = 2048
BATCH = 4
SEQ = 2048
ROWS = BATCH * SEQ
RWKV_HEADS = 16
HEAD_DIM = 64
RWKV_WIDTH = RWKV_HEADS * HEAD_DIM
LORA_DECAY = 96
LORA_ICLR = 96
LORA_GATE = 256
GN_EPS = 64e-5
NSA_HEADS = 16
NSA_KV_HEADS = 4
NSA_GROUP = NSA_HEADS // NSA_KV_HEADS
NSA_Q_WIDTH = NSA_HEADS * HEAD_DIM
NSA_KV_WIDTH = NSA_KV_HEADS * HEAD_DIM
CMP_BLOCK = 32
CMP_STRIDE = 16
CMP_HIDDEN = 256
N_CMP = (SEQ - CMP_BLOCK) // CMP_STRIDE + 1
N_CMP_PAD = 128
SEL_BLOCK = 64
SEL_TOPN = 16
N_SEL = SEQ // SEL_BLOCK
WINDOW = 512
D_FF = 4 * D_MODEL
NORM_EPS = 1e-5
NEG_INF = -1e30
TINY = 1e-30

LANES = 128
VMEM_LIMIT = 56 * 1024 * 1024

COL_GA = 0
COL_GB = 2048
COL_R = 4096
COL_K = 5120
COL_V = 6144
COL_LORA = 7168
COL_Q = 7680
COL_KC = 8704
COL_VC = 8960
COL_KS = 9216
COL_VS = 9472
COL_KW = 9728
COL_VW = 9984
COL_NG = 10240
PROJ_COLS = 10752
LORA_PACK = 512


def _dot(a, b):
    return jnp.dot(a.astype(BF16), b.astype(BF16), preferred_element_type=F32)


def _sigmoid(x):
    return 1.0 / (1.0 + jnp.exp(-x))


def _params(sem, limit=VMEM_LIMIT):
    return pltpu.CompilerParams(dimension_semantics=sem, vmem_limit_bytes=limit)


def _inproj_kernel(x_ref, g_ref, w_ref, o_ref, xn_ref):
    @pl.when(pl.program_id(1) == 0)
    def _():
        x = x_ref[...]
        ms = jnp.mean(x * x, axis=-1, keepdims=True)
        xn_ref[...] = (x * lax.rsqrt(ms + NORM_EPS) * g_ref[...]).astype(BF16)

    o_ref[...] = jnp.dot(xn_ref[...], w_ref[...], preferred_element_type=F32)


def _inproj(x2, g, w_p):
    tm, tn = 1024, 1536
    return pl.pallas_call(
        _inproj_kernel,
        grid=(ROWS // tm, PROJ_COLS // tn),
        in_specs=[pl.BlockSpec((tm, D_MODEL), lambda i, j: (i, 0)),
                  pl.BlockSpec((1, D_MODEL), lambda i, j: (0, 0)),
                  pl.BlockSpec((D_MODEL, tn), lambda i, j: (0, j))],
        out_specs=pl.BlockSpec((tm, tn), lambda i, j: (i, j)),
        out_shape=jax.ShapeDtypeStruct((ROWS, PROJ_COLS), F32),
        scratch_shapes=[pltpu.VMEM((tm, D_MODEL), BF16)],
        compiler_params=_params(("parallel", "arbitrary")),
        name="inproj",
    )(x2, g, w_p)


def _head_sum(x, ones_ref):
    hi = x.astype(BF16)
    lo = (x - hi.astype(F32)).astype(BF16)
    ones = ones_ref[...]
    return (jnp.dot(hi, ones, preferred_element_type=F32)
            + jnp.dot(lo, ones, preferred_element_type=F32))


def _rwkvprep_kernel(r_ref, k_ref, v_ref, lo_ref, rp_ref, kp_ref, vp_ref, lop_ref,
                     mu_ref, w0_ref, wup_ref, a0_ref, aup_ref, gup_ref, kk_ref, ka_ref, rk_ref, ones_ref,
                     ro_ref, wo_ref, ko_ref, vo_ref, kko_ref, bo_ref, go_ref, bon_ref, *, tm):
    i = pl.program_id(0)
    has_prev = jnp.where((i * tm) % SEQ == 0, 0.0, 1.0).astype(F32)

    def shift(cur_ref, prev_ref, mu):
        z = cur_ref[...]
        zp = pltpu.roll(z, 1, axis=0)
        prev_row = prev_ref[7:8, :] * has_prev
        row = lax.broadcasted_iota(jnp.int32, z.shape, 0)
        zp = jnp.where(row == 0, prev_row, zp)
        return z + (zp - z) * mu

    r = shift(r_ref, rp_ref, mu_ref[:, 0:1024])
    k = shift(k_ref, kp_ref, mu_ref[:, 1024:2048])
    v = shift(v_ref, vp_ref, mu_ref[:, 2048:3072])
    lo = shift(lo_ref, lop_ref, mu_ref[:, 3072:3584])
    w_lo, a_lo, g_lo = lo[:, 0:128], lo[:, 128:256], lo[:, 256:512]

    u = w0_ref[...] + _dot(jnp.tanh(w_lo), wup_ref[...])
    decay = jnp.exp(-(_sigmoid(u) * float(np.exp(-0.5))))
    a = _sigmoid(a0_ref[...] + _dot(a_lo, aup_ref[...]))
    g = _dot(_sigmoid(g_lo), gup_ref[...])

    kk = k * kk_ref[...]
    ss = _head_sum(kk * kk, ones_ref)
    kk = kk * lax.rsqrt(jnp.maximum(ss, 1e-24))
    k = k * (1.0 + (a - 1.0) * ka_ref[...])
    coef = _head_sum(r * k * rk_ref[...], ones_ref)

    ro_ref[...] = r
    wo_ref[...] = decay
    ko_ref[...] = k
    vo_ref[...] = v
    kko_ref[...] = kk
    bo_ref[...] = kk * a
    go_ref[...] = g
    bon_ref[...] = coef * v


def _rwkvprep(proj, mu_p, w0, wup_p, a0, aup_p, gup, k_k, k_a, r_k, ones):
    tm = 256
    pb = tm // 8
    cur = lambda c: pl.BlockSpec((tm, 1024), lambda i, c=c: (i, c))
    prev = lambda c: pl.BlockSpec((8, 1024), lambda i, c=c: (jnp.maximum(i * pb - 1, 0), c))
    full = lambda s: pl.BlockSpec(s, lambda i: (0,) * len(s))
    out = pl.BlockSpec((tm, 1024), lambda i: (i, 0))
    osh = jax.ShapeDtypeStruct((ROWS, RWKV_WIDTH), F32)
    return pl.pallas_call(
        functools.partial(_rwkvprep_kernel, tm=tm),
        grid=(ROWS // tm,),
        in_specs=[cur(COL_R // 1024), cur(COL_K // 1024), cur(COL_V // 1024),
                  pl.BlockSpec((tm, LORA_PACK), lambda i: (i, COL_LORA // LORA_PACK)),
                  prev(COL_R // 1024), prev(COL_K // 1024), prev(COL_V // 1024),
                  pl.BlockSpec((8, LORA_PACK), lambda i: (jnp.maximum(i * pb - 1, 0), COL_LORA // LORA_PACK)),
                  full((1, 3584)), full((1, 1024)), full((128, 1024)), full((1, 1024)), full((128, 1024)),
                  full((256, 1024)), full((1, 1024)), full((1, 1024)), full((1, 1024)), full((1024, 1024))],
        out_specs=[out] * 8,
        out_shape=[osh] * 8,
        compiler_params=_params(("parallel",)),
        name="rwkvprep",
    )(proj, proj, proj, proj, proj, proj, proj, proj,
      mu_p, w0, wup_p, a0, aup_p, gup, k_k, k_a, r_k, ones)


N_JPAIR = HEAD_DIM // 2
N_JOPS = 5


def _scan_kernel(kkn_ref, w_ref, b_ref, k_ref, r_ref, v_ref, y_ref, s_ref, sa_ref, d_ref, *, tc):
    @pl.when(pl.program_id(0) == 0)
    def _():
        s_ref[...] = jnp.zeros_like(s_ref)
        sa_ref[...] = jnp.zeros_like(sa_ref)

    lo_half = lax.broadcasted_iota(jnp.int32, (N_JPAIR, tc, LANES), 2) < 64
    for n, ref in enumerate((kkn_ref, w_ref, b_ref, k_ref, r_ref)):
        x = ref[...]
        xr = pltpu.roll(x, 64, axis=2)
        d_ref[n, 0:N_JPAIR] = jnp.where(lo_half, x, xr)
        d_ref[n, N_JPAIR:HEAD_DIM] = jnp.where(lo_half, xr, x)

    def tree(parts):
        while len(parts) > 1:
            parts = [parts[n] + parts[n + 1] for n in range(0, len(parts), 2)]
        return parts[0]

    def step(t, sa):
        vt = v_ref[t]
        acc_y = [None] * 4
        acc_s = [None] * 4
        for j in range(HEAD_DIM):
            row = lambda n: d_ref[n, j, pl.ds(t, 1), :]
            sn = s_ref[j] * row(1) - sa * row(2) + vt * row(3)
            s_ref[j] = sn
            ty = sn * row(4)
            ts = sn * row(0)
            acc_y[j % 4] = ty if acc_y[j % 4] is None else acc_y[j % 4] + ty
            acc_s[j % 4] = ts if acc_s[j % 4] is None else acc_s[j % 4] + ts
        y_ref[t] = tree(acc_y)
        return tree(acc_s)

    sa_ref[...] = lax.fori_loop(0, tc, step, sa_ref[...])


def _scan(kkn, w, b, k, r, v):
    tc = 64
    spec = pl.BlockSpec((tc, N_JPAIR, LANES), lambda i: (i, 0, 0))
    jspec = pl.BlockSpec((N_JPAIR, tc, LANES), lambda i: (0, i, 0))
    return pl.pallas_call(
        functools.partial(_scan_kernel, tc=tc),
        grid=(SEQ // tc,),
        in_specs=[jspec] * N_JOPS + [spec],
        out_specs=spec,
        out_shape=jax.ShapeDtypeStruct((SEQ, N_JPAIR, LANES), F32),
        scratch_shapes=[pltpu.VMEM((HEAD_DIM, N_JPAIR, LANES), F32),
                        pltpu.VMEM((N_JPAIR, LANES), F32),
                        pltpu.VMEM((N_JOPS, HEAD_DIM, tc, LANES), F32)],
        compiler_params=_params(("arbitrary",)),
        name="scan",
    )(kkn, w, b, k, r, v)


RELAYOUT_T = 128


def _relayout_kernel(x_ref, o_ref, z_ref):
    for b in range(BATCH):
        z_ref[b] = x_ref[b].T
    for p in range(N_JPAIR):
        tile = jnp.concatenate([z_ref[b, pl.ds(half * N_JPAIR + p, RWKV_HEADS, stride=HEAD_DIM), :]
                                for half in range(2) for b in range(BATCH)], axis=0)
        o_ref[p] = tile.T


def _relayout(a):
    return pl.pallas_call(
        _relayout_kernel,
        grid=(SEQ // RELAYOUT_T,),
        in_specs=[pl.BlockSpec((BATCH, RELAYOUT_T, RWKV_WIDTH), lambda i: (0, i, 0))],
        out_specs=pl.BlockSpec((N_JPAIR, RELAYOUT_T, LANES), lambda i: (0, i, 0)),
        out_shape=jax.ShapeDtypeStruct((N_JPAIR, SEQ, LANES), F32),
        scratch_shapes=[pltpu.VMEM((BATCH, RWKV_WIDTH, RELAYOUT_T), F32)],
        compiler_params=_params(("parallel",)),
        name="relayout",
    )(a.reshape(BATCH, SEQ, RWKV_WIDTH))


def _to_scan(a):
    a = a.reshape(BATCH, SEQ, RWKV_HEADS, 2, N_JPAIR).transpose(1, 4, 3, 0, 2)
    return a.reshape(SEQ, N_JPAIR, LANES)


def _from_scan(y):
    y = y.reshape(SEQ, N_JPAIR, 2, BATCH, RWKV_HEADS).transpose(3, 0, 4, 2, 1)
    return y.reshape(ROWS, RWKV_WIDTH)


def _rwkvpost_kernel(y_ref, bon_ref, g_ref, lw_ref, lb_ref, ones_ref, o_ref):
    y = y_ref[...]
    mean = _head_sum(y, ones_ref) * (1.0 / HEAD_DIM)
    d = y - mean
    var = _head_sum(d * d, ones_ref) * (1.0 / HEAD_DIM)
    yn = d * lax.rsqrt(var + GN_EPS) * lw_ref[...] + lb_ref[...]
    o_ref[...] = (yn + bon_ref[...]) * g_ref[...]


def _rwkvpost(y, bonus, g, lnx_w, lnx_b, ones):
    tm = 512
    blk = pl.BlockSpec((tm, 1024), lambda i: (i, 0))
    vec = pl.BlockSpec((1, 1024), lambda i: (0, 0))
    return pl.pallas_call(
        _rwkvpost_kernel,
        grid=(ROWS // tm,),
        in_specs=[blk, blk, blk, vec, vec, pl.BlockSpec((1024, 1024), lambda i: (0, 0))],
        out_specs=blk,
        out_shape=jax.ShapeDtypeStruct((ROWS, RWKV_WIDTH), F32),
        compiler_params=_params(("parallel",)),
        name="rwkvpost",
    )(y, bonus, g, lnx_w, lnx_b, ones)


def _compress_kernel(c_ref, w1_ref, pe_ref, w2_ref, o_ref):
    ch = c_ref[0, 0]
    w1 = w1_ref[0]
    half = CMP_STRIDE * HEAD_DIM
    top = _dot(ch, w1[:half])
    bot = _dot(ch, w1[half:])
    bias = _dot(jnp.broadcast_to(pe_ref[0], (8, CMP_BLOCK * HEAD_DIM)), w1)[0:1]
    hid = top + pltpu.roll(bot, N_CMP_PAD - 1, axis=0) + bias
    c0 = float(np.sqrt(2.0 / np.pi))
    act = 0.5 * hid * (1.0 + jnp.tanh(c0 * (hid + 0.044715 * (hid * hid * hid))))
    o_ref[0, 0] = _dot(act, w2_ref[0])


def _compress(chunks, w1, pe, w2):
    nbg = BATCH * NSA_KV_HEADS
    return pl.pallas_call(
        _compress_kernel,
        grid=(2, nbg),
        in_specs=[pl.BlockSpec((1, 1, N_CMP_PAD, CMP_STRIDE * HEAD_DIM), lambda c, n: (c, n, 0, 0)),
                  pl.BlockSpec((1, CMP_BLOCK * HEAD_DIM, CMP_HIDDEN), lambda c, n: (c, 0, 0)),
                  pl.BlockSpec((1, 1, CMP_BLOCK * HEAD_DIM), lambda c, n: (c, 0, 0)),
                  pl.BlockSpec((1, CMP_HIDDEN, HEAD_DIM), lambda c, n: (c, 0, 0))],
        out_specs=pl.BlockSpec((1, 1, N_CMP_PAD, HEAD_DIM), lambda c, n: (c, n, 0, 0)),
        out_shape=jax.ShapeDtypeStruct((2, nbg, N_CMP_PAD, HEAD_DIM), F32),
        compiler_params=_params(("parallel", "parallel")),
        name="compress",
    )(chunks, w1, pe, w2)


MASK_BIG = float(2.0 ** 100)
KEY_CHUNK = 512
N_FEAT = 64


def _key_features():
    s = np.arange(SEQ)
    f = np.zeros((SEQ, N_FEAT), np.float32)
    f[s, s // SEL_BLOCK] = 1.0
    f[:, 32:35] = (SEL_BLOCK * (s // SEL_BLOCK))[:, None]
    f[:, 35:38] = (s % SEL_BLOCK)[:, None]
    return f


def _split3(x):
    a1 = x.astype(BF16)
    r1 = x - a1.astype(F32)
    a2 = r1.astype(BF16)
    a3 = (r1 - a2.astype(F32)).astype(BF16)
    return a1, a2, a3


def _nsa_kernel(q_ref, kc_ref, vc_ref, ks_ref, vst_ref, kw_ref, vwt_ref, gate_ref, ovt_ref, feat_ref, o_ref,
                gt_ref, slc_ref, *, tq):
    g = pl.program_id(1)
    t0 = pl.program_id(2) * tq
    hg = NSA_GROUP
    nq = hg * tq
    dot = functools.partial(jnp.dot, preferred_element_type=F32)

    qt = (q_ref[...] * HEAD_DIM ** -0.5).T
    qt = jnp.concatenate([qt[h * HEAD_DIM:(h + 1) * HEAD_DIM] for h in range(hg)], axis=1).astype(BF16)
    lane = lax.broadcasted_iota(jnp.int32, (1, nq), 1)
    slope = jnp.exp2(-0.5 * (g * hg + lane // tq + 1).astype(F32))
    qpos = t0 + lane % tq

    n_idx = lax.broadcasted_iota(jnp.int32, (N_CMP_PAD, 1), 0)
    dist_c = qpos.astype(F32) - (n_idx * CMP_STRIDE + (CMP_BLOCK - 1)).astype(F32)
    mask_c = (dist_c >= 0) & (n_idx < N_CMP)
    x_c = jnp.where(mask_c, dot(kc_ref[0, 0].astype(BF16), qt) - slope * dist_c, NEG_INF)
    p_c = jnp.where(mask_c, jnp.exp(x_c - jnp.max(x_c, axis=0, keepdims=True)), 0.0)
    den_c = jnp.maximum(jnp.sum(p_c, axis=0, keepdims=True), TINY)
    o_cmp = dot(vc_ref[0, 0].T.astype(BF16), p_c.astype(BF16)) / den_c

    p_n = p_c / den_c
    p_sum = (p_n[:, 0:tq] + p_n[:, tq:2 * tq]) + (p_n[:, 2 * tq:3 * tq] + p_n[:, 3 * tq:4 * tq])
    ovt = ovt_ref[...]
    imp = sum(dot(ovt, part) for part in _split3(p_sum))
    blk = lax.broadcasted_iota(jnp.int32, (N_SEL, tq), 0)
    cur = (t0 + lax.broadcasted_iota(jnp.int32, (N_SEL, tq), 1)) // SEL_BLOCK
    forced = (blk == 0) | (blk == cur) | (blk == cur - 1)
    score = jnp.where(blk > cur, NEG_INF, jnp.where(forced, -NEG_INF, imp))
    rank = jnp.zeros((N_SEL, tq), F32)
    for jp in range(N_SEL):
        row = score[jp:jp + 1, :]
        beats = (row > score) | ((row == score) & (blk > jp))
        rank = rank + jnp.where(beats, 1.0, 0.0)
    sel_bias = jnp.where((rank < SEL_TOPN) & (blk <= cur), 0.0, -MASK_BIG)
    sel_bias = jnp.concatenate([sel_bias] * hg, axis=1)

    s1, s2, s3 = (piece.astype(F32) for piece in _split3(slope))
    r8 = lax.broadcasted_iota(jnp.int32, (8, nq), 0)
    srow = jnp.where((r8 == 0) | (r8 == 3), s1, jnp.where((r8 == 1) | (r8 == 4), s2, s3))
    srow = jnp.where(r8 < 6, srow, 0.0)
    pad = jnp.zeros((N_FEAT - N_SEL - 8, nq), F32)
    q_slc = jnp.concatenate([qt, jnp.concatenate([sel_bias, srow, pad], axis=0).astype(BF16)], axis=0)
    q_win = jnp.concatenate([qt, jnp.concatenate([jnp.zeros_like(sel_bias), srow, pad], axis=0).astype(BF16)],
                            axis=0)

    def scores(k_ref, start, size, q_aug):
        keys = jnp.concatenate([k_ref[0, pl.ds(start, size), :], feat_ref[pl.ds(start, size), :]], axis=1)
        return dot(keys, q_aug)

    def attend(s, v_t):
        p = jnp.exp(s - jnp.max(s, axis=0, keepdims=True))
        return dot(v_t, p.astype(BF16)) / jnp.maximum(jnp.sum(p, axis=0, keepdims=True), TINY)

    def slc_variant(n_chunks):
        size = n_chunks * KEY_CHUNK
        s = scores(ks_ref, 0, size, q_slc)
        kpos = (size - KEY_CHUNK) + lax.broadcasted_iota(jnp.int32, (KEY_CHUNK, 1), 0)
        last = jnp.where(kpos <= qpos, s[size - KEY_CHUNK:], -MASK_BIG)
        s = last if n_chunks == 1 else jnp.concatenate([s[:size - KEY_CHUNK], last], axis=0)
        slc_ref[...] = attend(s, vst_ref[0, :, 0:size])

    for n_chunks in range(1, SEQ // KEY_CHUNK + 1):
        pl.when(t0 // KEY_CHUNK == n_chunks - 1)(functools.partial(slc_variant, n_chunks))
    o_slc = slc_ref[...]

    span = WINDOW + tq
    start = pl.multiple_of(jnp.maximum(t0 - WINDOW, 0), LANES)
    dist_w = qpos - (start + lax.broadcasted_iota(jnp.int32, (span, 1), 0))
    s_w = jnp.where((dist_w >= 0) & (dist_w < WINDOW), scores(kw_ref, start, span, q_win), -MASK_BIG)
    o_win = attend(s_w, vwt_ref[0, :, pl.ds(start, span)])

    gt_ref[...] = _sigmoid(gate_ref[...]).T
    out = jnp.zeros((HEAD_DIM, nq), F32)
    for c, o in enumerate((o_cmp, o_slc, o_win)):
        g4 = gt_ref[pl.ds(c * NSA_HEADS + g * hg, hg), :]
        out = out + jnp.concatenate([g4[h:h + 1, :] for h in range(hg)], axis=1) * o
    o_ref[...] = jnp.concatenate([out[:, h * tq:(h + 1) * tq].T for h in range(hg)], axis=1)


def _nsa(proj, kcvc, ks, vst, kw, vwt, overlap_t, feat):
    tq = 128
    nq = SEQ // tq
    G = NSA_KV_HEADS
    kv_n = pl.BlockSpec((1, SEQ, HEAD_DIM), lambda b, g, i: (b * G + g, 0, 0))
    kv_t = pl.BlockSpec((1, HEAD_DIM, SEQ), lambda b, g, i: (b * G + g, 0, 0))
    cmp_spec = lambda c: pl.BlockSpec((1, 1, N_CMP_PAD, HEAD_DIM), lambda b, g, i, c=c: (c, b * G + g, 0, 0))
    return pl.pallas_call(
        functools.partial(_nsa_kernel, tq=tq),
        grid=(BATCH, G, nq),
        in_specs=[pl.BlockSpec((tq, NSA_GROUP * HEAD_DIM), lambda b, g, i: (b * nq + i, COL_Q // 256 + g)),
                  cmp_spec(0), cmp_spec(1), kv_n, kv_t, kv_n, kv_t,
                  pl.BlockSpec((tq, LANES), lambda b, g, i: (b * nq + i, COL_NG // LANES)),
                  pl.BlockSpec((N_SEL, N_CMP_PAD), lambda b, g, i: (0, 0)),
                  pl.BlockSpec((SEQ, N_FEAT), lambda b, g, i: (0, 0))],
        out_specs=pl.BlockSpec((tq, NSA_GROUP * HEAD_DIM), lambda b, g, i: (b * nq + i, g)),
        out_shape=jax.ShapeDtypeStruct((ROWS, NSA_Q_WIDTH), F32),
        scratch_shapes=[pltpu.VMEM((LANES, tq), F32), pltpu.VMEM((HEAD_DIM, NSA_GROUP * tq), F32)],
        compiler_params=_params(("parallel", "parallel", "parallel")),
        name="nsa",
    )(proj, kcvc, kcvc, ks, vst, kw, vwt, proj, overlap_t, feat)


def _mix_kernel(x_ref, ya_ref, yb_ref, ga_ref, gb_ref, wa_ref, wb_ref, wo_ref, h_ref):
    ma = _dot(ya_ref[...], wa_ref[...])
    mb = _dot(yb_ref[...], wb_ref[...])
    mixed = _sigmoid(ga_ref[...]) * ma + _sigmoid(gb_ref[...]) * mb
    h_ref[...] = x_ref[...] + _dot(mixed, wo_ref[...])


def _mix(x2, ya, yb, proj, wa, wb, wo):
    tm = 256
    row = lambda w, c=0: pl.BlockSpec((tm, w), lambda i, c=c: (i, c))
    const = lambda s: pl.BlockSpec(s, lambda i: (0, 0), pipeline_mode=pl.Buffered(1))
    return pl.pallas_call(
        _mix_kernel,
        grid=(ROWS // tm,),
        in_specs=[row(D_MODEL), row(1024), row(1024), row(D_MODEL, COL_GA // D_MODEL), row(D_MODEL, COL_GB // D_MODEL),
                  const((1024, D_MODEL)), const((1024, D_MODEL)), const((D_MODEL, D_MODEL))],
        out_specs=row(D_MODEL),
        out_shape=jax.ShapeDtypeStruct((ROWS, D_MODEL), F32),
        compiler_params=_params(("parallel",)),
        name="mix",
    )(x2, ya, yb, proj, proj, wa, wb, wo)


def _mlp_kernel(h_ref, gm_ref, wu_ref, wd_ref, gf_ref, o_ref, hn_ref, acc_ref):
    f = pl.program_id(1)

    @pl.when(f == 0)
    def _():
        h = h_ref[...]
        ms = jnp.mean(h * h, axis=-1, keepdims=True)
        hn_ref[...] = (h * lax.rsqrt(ms + NORM_EPS) * gm_ref[...]).astype(BF16)
        acc_ref[...] = jnp.zeros_like(acc_ref)

    u = jnp.maximum(jnp.dot(hn_ref[...], wu_ref[...], preferred_element_type=F32), 0.0)
    acc_ref[...] += jnp.dot((u * u).astype(BF16), wd_ref[...], preferred_element_type=F32)

    @pl.when(f == pl.num_programs(1) - 1)
    def _():
        h2 = h_ref[...] + acc_ref[...]
        ms = jnp.mean(h2 * h2, axis=-1, keepdims=True)
        o_ref[...] = h2 * lax.rsqrt(ms + NORM_EPS) * gf_ref[...]


def _mlp(h, g_mlp, w_up, w_down, g_final):
    tm, tf = 512, 512
    return pl.pallas_call(
        _mlp_kernel,
        grid=(ROWS // tm, D_FF // tf),
        in_specs=[pl.BlockSpec((tm, D_MODEL), lambda i, f: (i, 0)),
                  pl.BlockSpec((1, D_MODEL), lambda i, f: (0, 0)),
                  pl.BlockSpec((D_MODEL, tf), lambda i, f: (0, f)),
                  pl.BlockSpec((tf, D_MODEL), lambda i, f: (f, 0)),
                  pl.BlockSpec((1, D_MODEL), lambda i, f: (0, 0))],
        out_specs=pl.BlockSpec((tm, D_MODEL), lambda i, f: (i, 0)),
        out_shape=jax.ShapeDtypeStruct((ROWS, D_MODEL), F32),
        scratch_shapes=[pltpu.VMEM((tm, D_MODEL), BF16), pltpu.VMEM((tm, D_MODEL), F32)],
        compiler_params=_params(("parallel", "arbitrary")),
        name="mlp",
    )(h, g_mlp, w_up, w_down, g_final)


def _pack_cols(w, pad):
    return jnp.concatenate([
        w[..., 6128:10224],
        w[..., 0:3168], pad(32),
        w[..., 3168:3264], pad(32),
        w[..., 3264:3520],
        w[..., 3520:6080],
        w[..., 6080:6128], pad(464),
    ], axis=-1)


def _overlap_matrix():
    cs = np.arange(N_CMP)[:, None] * CMP_STRIDE
    ss = np.arange(N_SEL)[None, :] * SEL_BLOCK
    ov = np.clip(np.minimum(cs + CMP_BLOCK, ss + SEL_BLOCK) - np.maximum(cs, ss), 0, None) / CMP_BLOCK
    out = np.zeros((N_CMP_PAD, N_SEL), np.float32)
    out[:N_CMP] = ov
    return out


def kernel(x, norm_mix, w_in, rwkv_mu, rwkv_w0, rwkv_w_up, rwkv_a0, rwkv_a_up, rwkv_g_up, rwkv_k_k, rwkv_k_a,
           rwkv_r_k, rwkv_lnx_w, rwkv_lnx_b, cmp_pe_k, cmp_w1_k, cmp_w2_k, cmp_pe_v, cmp_w1_v, cmp_w2_v,
           w_out_rwkv, w_out_nsa, w_o, norm_mlp, mlp_w_up, mlp_w_down, norm_final):
    assert x.shape == (BATCH, SEQ, D_MODEL) and w_in.shape[0] == 1
    l = 0
    G = NSA_KV_HEADS
    x2 = x.reshape(ROWS, D_MODEL)
    row = lambda a: a.reshape(1, -1)

    w_p = _pack_cols(w_in[l], lambda n: jnp.zeros((D_MODEL, n), F32)).astype(BF16)
    proj = _inproj(x2, row(norm_mix[l]), w_p)

    mu = rwkv_mu[l]
    mu_p = jnp.concatenate([mu[0:3168], jnp.zeros((32,), F32), mu[3168:3264], jnp.zeros((32,), F32), mu[3264:3520]])
    zrows = jnp.zeros((32, RWKV_WIDTH), F32)
    ones = jnp.asarray(np.kron(np.eye(RWKV_HEADS), np.ones((HEAD_DIM, HEAD_DIM))), BF16)
    r, w, k, v, kk, b, g, bonus = _rwkvprep(
        proj, row(mu_p), row(rwkv_w0[l]), jnp.concatenate([rwkv_w_up[l], zrows]),
        row(rwkv_a0[l]), jnp.concatenate([rwkv_a_up[l], zrows]), rwkv_g_up[l],
        row(rwkv_k_k[l]), row(rwkv_k_a[l]), row(rwkv_r_k[l]), ones)
    kkn = jnp.concatenate([kk.reshape(BATCH, SEQ, RWKV_WIDTH)[:, 1:], jnp.zeros((BATCH, 1, RWKV_WIDTH), F32)], axis=1)
    y = _scan(_relayout(kkn), _relayout(w), _relayout(b), _relayout(k), _relayout(r), _to_scan(v))
    ya = _rwkvpost(_from_scan(y), bonus, g, row(rwkv_lnx_w[l]), row(rwkv_lnx_b[l]), ones)

    kv = lambda c0: proj[:, c0:c0 + NSA_KV_WIDTH].reshape(BATCH, SEQ, G, HEAD_DIM)
    chunks = jnp.stack([kv(COL_KC), kv(COL_VC)]).transpose(0, 1, 3, 2, 4)
    chunks = chunks.reshape(2, BATCH * G, N_CMP_PAD, CMP_STRIDE * HEAD_DIM)
    kcvc = _compress(chunks, jnp.stack([cmp_w1_k[l], cmp_w1_v[l]]),
                     jnp.stack([cmp_pe_k[l].reshape(1, -1), cmp_pe_v[l].reshape(1, -1)]),
                     jnp.stack([cmp_w2_k[l], cmp_w2_v[l]]))
    kv_b = lambda c0: lax.optimization_barrier(proj[:, c0:c0 + NSA_KV_WIDTH]).reshape(BATCH, SEQ, G, HEAD_DIM)
    keys = lambda c0: kv_b(c0).transpose(0, 2, 1, 3).reshape(BATCH * G, SEQ, HEAD_DIM).astype(BF16)
    vals_t = lambda c0: kv_b(c0).transpose(0, 2, 3, 1).reshape(BATCH * G, HEAD_DIM, SEQ).astype(BF16)
    yb = _nsa(proj, kcvc, keys(COL_KS), vals_t(COL_VS), keys(COL_KW), vals_t(COL_VW),
              jnp.asarray(_overlap_matrix().T, BF16), jnp.asarray(_key_features(), BF16))

    h = _mix(x2, ya, yb, proj, w_out_rwkv[l].astype(BF16), w_out_nsa[l].astype(BF16), w_o[l].astype(BF16))
    out = _mlp(h, row(norm_mlp[l]), mlp_w_up[l].astype(BF16), mlp_w_down[l].astype(BF16), row(norm_final))
    return out.reshape(BATCH, SEQ, D_MODEL)
```

```python
import functools

import numpy as np
import jax
import jax.numpy as jnp
from jax import lax
from jax.experimental import pallas as pl
from jax.experimental.pallas import tpu as pltpu

F32 = jnp.float32
BF16 = jnp.bfloat16

D_MODEL = 2048
BATCH = 4
SEQ = 2048
ROWS = BATCH * SEQ
RWKV_HEADS = 16
HEAD_DIM = 64
RWKV_WIDTH = RWKV_HEADS * HEAD_DIM
LORA_DECAY = 96
LORA_ICLR = 96
LORA_GATE = 256
GN_EPS = 64e-5
NSA_HEADS = 16
NSA_KV_HEADS = 4
NSA_GROUP = NSA_HEADS // NSA_KV_HEADS
NSA_Q_WIDTH = NSA_HEADS * HEAD_DIM
NSA_KV_WIDTH = NSA_KV_HEADS * HEAD_DIM
CMP_BLOCK = 32
CMP_STRIDE = 16
CMP_HIDDEN = 256
N_CMP = (SEQ - CMP_BLOCK) // CMP_STRIDE + 1
N_CMP_PAD = 128
SEL_BLOCK = 64
SEL_TOPN = 16
N_SEL = SEQ // SEL_BLOCK
WINDOW = 512
D_FF = 4 * D_MODEL
NORM_EPS = 1e-5
NEG_INF = -1e30
TINY = 1e-30

LANES = 128
VMEM_LIMIT = 56 * 1024 * 1024

COL_GA = 0
COL_GB = 2048
COL_R = 4096
COL_K = 5120
COL_V = 6144
COL_LORA = 7168
COL_Q = 7680
COL_KC = 8704
COL_VC = 8960
COL_KS = 9216
COL_VS = 9472
COL_KW = 9728
COL_VW = 9984
COL_NG = 10240
PROJ_COLS = 10752
LORA_PACK = 512


def _dot(a, b):
    return jnp.dot(a.astype(BF16), b.astype(BF16), preferred_element_type=F32)


def _sigmoid(x):
    return 1.0 / (1.0 + jnp.exp(-x))


def _params(sem, limit=VMEM_LIMIT):
    return pltpu.CompilerParams(dimension_semantics=sem, vmem_limit_bytes=limit)


def _inproj_kernel(x_ref, g_ref, w_ref, o_ref, xn_ref):
    @pl.when(pl.program_id(1) == 0)
    def _():
        x = x_ref[...]
        ms = jnp.mean(x * x, axis=-1, keepdims=True)
        xn_ref[...] = (x * lax.rsqrt(ms + NORM_EPS) * g_ref[...]).astype(BF16)

    o_ref[...] = jnp.dot(xn_ref[...], w_ref[...], preferred_element_type=F32)


def _pack_kernel(w_ref, o_ref):
    w = w_ref[...]
    o_ref[...] = _pack_cols(w, lambda n: jnp.zeros((w.shape[0], n), F32)).astype(BF16)


def _pack_weight(w):
    tk = 128
    return pl.pallas_call(
        _pack_kernel,
        grid=(D_MODEL // tk,),
        in_specs=[pl.BlockSpec((tk, w.shape[1]), lambda i: (i, 0))],
        out_specs=pl.BlockSpec((tk, PROJ_COLS), lambda i: (i, 0)),
        out_shape=jax.ShapeDtypeStruct((D_MODEL, PROJ_COLS), BF16),
        compiler_params=_params(("parallel",)),
        name="packw",
    )(w)


def _inproj(x2, g, w_p):
    tm, tn = 1024, 1536
    return pl.pallas_call(
        _inproj_kernel,
        grid=(ROWS // tm, PROJ_COLS // tn),
        in_specs=[pl.BlockSpec((tm, D_MODEL), lambda i, j: (i, 0)),
                  pl.BlockSpec((1, D_MODEL), lambda i, j: (0, 0)),
                  pl.BlockSpec((D_MODEL, tn), lambda i, j: (0, j))],
        out_specs=pl.BlockSpec((tm, tn), lambda i, j: (i, j)),
        out_shape=jax.ShapeDtypeStruct((ROWS, PROJ_COLS), F32),
        scratch_shapes=[pltpu.VMEM((tm, D_MODEL), BF16)],
        compiler_params=_params(("parallel", "arbitrary")),
        name="inproj",
    )(x2, g, w_p)


def _head_sum(x, ones_ref):
    hi = x.astype(BF16)
    lo = (x - hi.astype(F32)).astype(BF16)
    ones = ones_ref[...]
    return (jnp.dot(hi, ones, preferred_element_type=F32)
            + jnp.dot(lo, ones, preferred_element_type=F32))


def _rwkvprep_kernel(r_ref, k_ref, v_ref, lo_ref, rp_ref, kp_ref, vp_ref, lop_ref,
                     mu_ref, w0_ref, wup_ref, a0_ref, aup_ref, gup_ref, kk_ref, ka_ref, rk_ref, ones_ref,
                     ro_ref, wo_ref, ko_ref, vo_ref, kko_ref, bo_ref, go_ref, bon_ref, *, tm):
    i = pl.program_id(0)
    has_prev = jnp.where((i * tm) % SEQ == 0, 0.0, 1.0).astype(F32)

    def shift(cur_ref, prev_ref, mu):
        z = cur_ref[...]
        zp = pltpu.roll(z, 1, axis=0)
        prev_row = prev_ref[7:8, :] * has_prev
        row = lax.broadcasted_iota(jnp.int32, z.shape, 0)
        zp = jnp.where(row == 0, prev_row, zp)
        return z + (zp - z) * mu

    r = shift(r_ref, rp_ref, mu_ref[:, 0:1024])
    k = shift(k_ref, kp_ref, mu_ref[:, 1024:2048])
    v = shift(v_ref, vp_ref, mu_ref[:, 2048:3072])
    lo = shift(lo_ref, lop_ref, mu_ref[:, 3072:3584])
    w_lo, a_lo, g_lo = lo[:, 0:128], lo[:, 128:256], lo[:, 256:512]

    u = w0_ref[...] + _dot(jnp.tanh(w_lo), wup_ref[...])
    decay = jnp.exp(-(_sigmoid(u) * float(np.exp(-0.5))))
    a = _sigmoid(a0_ref[...] + _dot(a_lo, aup_ref[...]))
    g = _dot(_sigmoid(g_lo), gup_ref[...])

    kk = k * kk_ref[...]
    ss = _head_sum(kk * kk, ones_ref)
    kk = kk * lax.rsqrt(jnp.maximum(ss, 1e-24))
    k = k * (1.0 + (a - 1.0) * ka_ref[...])
    coef = _head_sum(r * k * rk_ref[...], ones_ref)

    ro_ref[...] = r
    wo_ref[...] = decay
    ko_ref[...] = k
    vo_ref[...] = v
    kko_ref[...] = kk
    bo_ref[...] = kk * a
    go_ref[...] = g
    bon_ref[...] = coef * v


def _rwkvprep(proj, mu_p, w0, wup_p, a0, aup_p, gup, k_k, k_a, r_k, ones):
    tm = 256
    pb = tm // 8
    cur = lambda c: pl.BlockSpec((tm, 1024), lambda i, c=c: (i, c))
    prev = lambda c: pl.BlockSpec((8, 1024), lambda i, c=c: (jnp.maximum(i * pb - 1, 0), c))
    full = lambda s: pl.BlockSpec(s, lambda i: (0,) * len(s))
    out = pl.BlockSpec((tm, 1024), lambda i: (i, 0))
    osh = jax.ShapeDtypeStruct((ROWS, RWKV_WIDTH), F32)
    return pl.pallas_call(
        functools.partial(_rwkvprep_kernel, tm=tm),
        grid=(ROWS // tm,),
        in_specs=[cur(COL_R // 1024), cur(COL_K // 1024), cur(COL_V // 1024),
                  pl.BlockSpec((tm, LORA_PACK), lambda i: (i, COL_LORA // LORA_PACK)),
                  prev(COL_R // 1024), prev(COL_K // 1024), prev(COL_V // 1024),
                  pl.BlockSpec((8, LORA_PACK), lambda i: (jnp.maximum(i * pb - 1, 0), COL_LORA // LORA_PACK)),
                  full((1, 3584)), full((1, 1024)), full((128, 1024)), full((1, 1024)), full((128, 1024)),
                  full((256, 1024)), full((1, 1024)), full((1, 1024)), full((1, 1024)), full((1024, 1024))],
        out_specs=[out] * 8,
        out_shape=[osh] * 8,
        compiler_params=_params(("parallel",)),
        name="rwkvprep",
    )(proj, proj, proj, proj, proj, proj, proj, proj,
      mu_p, w0, wup_p, a0, aup_p, gup, k_k, k_a, r_k, ones)


N_JPAIR = HEAD_DIM // 2
N_JOPS = 5


def _scan_kernel(kkn_ref, w_ref, b_ref, k_ref, r_ref, v_ref, y_ref, s_ref, sa_ref, d_ref, *, tc):
    @pl.when(pl.program_id(0) == 0)
    def _():
        s_ref[...] = jnp.zeros_like(s_ref)
        sa_ref[...] = jnp.zeros_like(sa_ref)

    lo_half = lax.broadcasted_iota(jnp.int32, (N_JPAIR, tc, LANES), 2) < 64
    for n, ref in enumerate((kkn_ref, w_ref, b_ref, k_ref, r_ref)):
        x = ref[...]
        xr = pltpu.roll(x, 64, axis=2)
        d_ref[n, 0:N_JPAIR] = jnp.where(lo_half, x, xr)
        d_ref[n, N_JPAIR:HEAD_DIM] = jnp.where(lo_half, xr, x)

    def tree(parts):
        while len(parts) > 1:
            parts = [parts[n] + parts[n + 1] for n in range(0, len(parts), 2)]
        return parts[0]

    def step(t, sa):
        vt = v_ref[t]
        acc_y = [None] * 4
        acc_s = [None] * 4
        for j in range(HEAD_DIM):
            row = lambda n: d_ref[n, j, pl.ds(t, 1), :]
            sn = s_ref[j] * row(1) - sa * row(2) + vt * row(3)
            s_ref[j] = sn
            ty = sn * row(4)
            ts = sn * row(0)
            acc_y[j % 4] = ty if acc_y[j % 4] is None else acc_y[j % 4] + ty
            acc_s[j % 4] = ts if acc_s[j % 4] is None else acc_s[j % 4] + ts
        y_ref[t] = tree(acc_y)
        return tree(acc_s)

    sa_ref[...] = lax.fori_loop(0, tc, step, sa_ref[...])


def _scan(kkn, w, b, k, r, v):
    tc = 64
    spec = pl.BlockSpec((tc, N_JPAIR, LANES), lambda i: (i, 0, 0))
    jspec = pl.BlockSpec((N_JPAIR, tc, LANES), lambda i: (0, i, 0))
    return pl.pallas_call(
        functools.partial(_scan_kernel, tc=tc),
        grid=(SEQ // tc,),
        in_specs=[jspec] * N_JOPS + [spec],
        out_specs=spec,
        out_shape=jax.ShapeDtypeStruct((SEQ, N_JPAIR, LANES), F32),
        scratch_shapes=[pltpu.VMEM((HEAD_DIM, N_JPAIR, LANES), F32),
                        pltpu.VMEM((N_JPAIR, LANES), F32),
                        pltpu.VMEM((N_JOPS, HEAD_DIM, tc, LANES), F32)],
        compiler_params=_params(("arbitrary",)),
        name="scan",
    )(kkn, w, b, k, r, v)


RELAYOUT_T = 128
_HALF_BATCH = [(half, b) for half in range(2) for b in range(BATCH)]


def _gather_lanes(z_ref, p):
    return jnp.concatenate([z_ref[b, pl.ds(half * N_JPAIR + p, RWKV_HEADS, stride=HEAD_DIM), :]
                            for half, b in _HALF_BATCH], axis=0)


def _relayout_kernel(x_ref, o_ref, z_ref, *, time_major):
    for b in range(BATCH):
        z_ref[b] = x_ref[b].T
    for p in range(N_JPAIR):
        tile = _gather_lanes(z_ref, p).T
        if time_major:
            o_ref[pl.ds(p, RELAYOUT_T, stride=N_JPAIR), :] = tile
        else:
            o_ref[p] = tile


def _relayout_back_kernel(y_ref, o_ref, z_ref):
    for p in range(N_JPAIR):
        tile = y_ref[pl.ds(p, RELAYOUT_T, stride=N_JPAIR), :].T
        for n, (half, b) in enumerate(_HALF_BATCH):
            z_ref[b, pl.ds(half * N_JPAIR + p, RWKV_HEADS, stride=HEAD_DIM), :] = (
                tile[n * RWKV_HEADS:(n + 1) * RWKV_HEADS])
    for b in range(BATCH):
        o_ref[b] = z_ref[b].T


_NATURAL_SPEC = pl.BlockSpec((BATCH, RELAYOUT_T, RWKV_WIDTH), lambda i: (0, i, 0))
_TIME_MAJOR_SPEC = pl.BlockSpec((RELAYOUT_T * N_JPAIR, LANES), lambda i: (i, 0))
_RELAYOUT_SCRATCH = [pltpu.VMEM((BATCH, RWKV_WIDTH, RELAYOUT_T), F32)]


def _relayout(a, time_major=False):
    if time_major:
        out_spec, out_shape = _TIME_MAJOR_SPEC, (SEQ * N_JPAIR, LANES)
    else:
        out_spec, out_shape = pl.BlockSpec((N_JPAIR, RELAYOUT_T, LANES), lambda i: (0, i, 0)), (N_JPAIR, SEQ, LANES)
    out = pl.pallas_call(
        functools.partial(_relayout_kernel, time_major=time_major),
        grid=(SEQ // RELAYOUT_T,),
        in_specs=[_NATURAL_SPEC],
        out_specs=out_spec,
        out_shape=jax.ShapeDtypeStruct(out_shape, F32),
        scratch_shapes=_RELAYOUT_SCRATCH,
        compiler_params=_params(("parallel",)),
        name="relayout",
    )(a.reshape(BATCH, SEQ, RWKV_WIDTH))
    return out.reshape(SEQ, N_JPAIR, LANES) if time_major else out


def _relayout_back(y):
    return pl.pallas_call(
        _relayout_back_kernel,
        grid=(SEQ // RELAYOUT_T,),
        in_specs=[_TIME_MAJOR_SPEC],
        out_specs=_NATURAL_SPEC,
        out_shape=jax.ShapeDtypeStruct((BATCH, SEQ, RWKV_WIDTH), F32),
        scratch_shapes=_RELAYOUT_SCRATCH,
        compiler_params=_params(("parallel",)),
        name="relayout_back",
    )(y.reshape(SEQ * N_JPAIR, LANES)).reshape(ROWS, RWKV_WIDTH)


def _rwkvpost_kernel(y_ref, bon_ref, g_ref, lw_ref, lb_ref, ones_ref, o_ref):
    y = y_ref[...]
    mean = _head_sum(y, ones_ref) * (1.0 / HEAD_DIM)
    d = y - mean
    var = _head_sum(d * d, ones_ref) * (1.0 / HEAD_DIM)
    yn = d * lax.rsqrt(var + GN_EPS) * lw_ref[...] + lb_ref[...]
    o_ref[...] = (yn + bon_ref[...]) * g_ref[...]


def _rwkvpost(y, bonus, g, lnx_w, lnx_b, ones):
    tm = 512
    blk = pl.BlockSpec((tm, 1024), lambda i: (i, 0))
    vec = pl.BlockSpec((1, 1024), lambda i: (0, 0))
    return pl.pallas_call(
        _rwkvpost_kernel,
        grid=(ROWS // tm,),
        in_specs=[blk, blk, blk, vec, vec, pl.BlockSpec((1024, 1024), lambda i: (0, 0))],
        out_specs=blk,
        out_shape=jax.ShapeDtypeStruct((ROWS, RWKV_WIDTH), F32),
        compiler_params=_params(("parallel",)),
        name="rwkvpost",
    )(y, bonus, g, lnx_w, lnx_b, ones)


def _compress_kernel(c_ref, w1_ref, pe_ref, w2_ref, o_ref):
    ch = c_ref[0, 0]
    w1 = w1_ref[0]
    half = CMP_STRIDE * HEAD_DIM
    top = _dot(ch, w1[:half])
    bot = _dot(ch, w1[half:])
    bias = _dot(jnp.broadcast_to(pe_ref[0], (8, CMP_BLOCK * HEAD_DIM)), w1)[0:1]
    hid = top + pltpu.roll(bot, N_CMP_PAD - 1, axis=0) + bias
    c0 = float(np.sqrt(2.0 / np.pi))
    act = 0.5 * hid * (1.0 + jnp.tanh(c0 * (hid + 0.044715 * (hid * hid * hid))))
    o_ref[0, 0] = _dot(act, w2_ref[0])


def _compress(chunks, w1, pe, w2):
    nbg = BATCH * NSA_KV_HEADS
    return pl.pallas_call(
        _compress_kernel,
        grid=(2, nbg),
        in_specs=[pl.BlockSpec((1, 1, N_CMP_PAD, CMP_STRIDE * HEAD_DIM), lambda c, n: (c, n, 0, 0)),
                  pl.BlockSpec((1, CMP_BLOCK * HEAD_DIM, CMP_HIDDEN), lambda c, n: (c, 0, 0)),
                  pl.BlockSpec((1, 1, CMP_BLOCK * HEAD_DIM), lambda c, n: (c, 0, 0)),
                  pl.BlockSpec((1, CMP_HIDDEN, HEAD_DIM), lambda c, n: (c, 0, 0))],
        out_specs=pl.BlockSpec((1, 1, N_CMP_PAD, HEAD_DIM), lambda c, n: (c, n, 0, 0)),
        out_shape=jax.ShapeDtypeStruct((2, nbg, N_CMP_PAD, HEAD_DIM), F32),
        compiler_params=_params(("parallel", "parallel")),
        name="compress",
    )(chunks, w1, pe, w2)


MASK_BIG = float(2.0 ** 100)
KEY_CHUNK = 512
N_FEAT = 64


def _key_features():
    s = np.arange(SEQ)
    f = np.zeros((SEQ, N_FEAT), np.float32)
    f[s, s // SEL_BLOCK] = 1.0
    f[:, 32:35] = (SEL_BLOCK * (s // SEL_BLOCK))[:, None]
    f[:, 35:38] = (s % SEL_BLOCK)[:, None]
    return f


def _split3(x):
    a1 = x.astype(BF16)
    r1 = x - a1.astype(F32)
    a2 = r1.astype(BF16)
    a3 = (r1 - a2.astype(F32)).astype(BF16)
    return a1, a2, a3


def _nsa_kernel(q_ref, kc_ref, vc_ref, ks_ref, vst_ref, kw_ref, vwt_ref, gate_ref, ovt_ref, feat_ref, o_ref,
                gt_ref, slc_ref, *, tq):
    g = pl.program_id(1)
    t0 = pl.program_id(2) * tq
    hg = NSA_GROUP
    nq = hg * tq
    dot = functools.partial(jnp.dot, preferred_element_type=F32)

    qt = (q_ref[...] * HEAD_DIM ** -0.5).T
    qt = jnp.concatenate([qt[h * HEAD_DIM:(h + 1) * HEAD_DIM] for h in range(hg)], axis=1).astype(BF16)
    lane = lax.broadcasted_iota(jnp.int32, (1, nq), 1)
    slope = jnp.exp2(-0.5 * (g * hg + lane // tq + 1).astype(F32))
    qpos = t0 + lane % tq

    n_idx = lax.broadcasted_iota(jnp.int32, (N_CMP_PAD, 1), 0)
    dist_c = qpos.astype(F32) - (n_idx * CMP_STRIDE + (CMP_BLOCK - 1)).astype(F32)
    mask_c = (dist_c >= 0) & (n_idx < N_CMP)
    x_c = jnp.where(mask_c, dot(kc_ref[0, 0].astype(BF16), qt) - slope * dist_c, NEG_INF)
    p_c = jnp.where(mask_c, jnp.exp(x_c - jnp.max(x_c, axis=0, keepdims=True)), 0.0)
    den_c = jnp.maximum(jnp.sum(p_c, axis=0, keepdims=True), TINY)
    o_cmp = dot(vc_ref[0, 0].T.astype(BF16), p_c.astype(BF16)) / den_c

    p_n = p_c / den_c
    p_sum = (p_n[:, 0:tq] + p_n[:, tq:2 * tq]) + (p_n[:, 2 * tq:3 * tq] + p_n[:, 3 * tq:4 * tq])
    ovt = ovt_ref[...]
    imp = sum(dot(ovt, part) for part in _split3(p_sum))
    blk = lax.broadcasted_iota(jnp.int32, (N_SEL, tq), 0)
    cur = (t0 + lax.broadcasted_iota(jnp.int32, (N_SEL, tq), 1)) // SEL_BLOCK
    forced = (blk == 0) | (blk == cur) | (blk == cur - 1)
    score = jnp.where(blk > cur, NEG_INF, jnp.where(forced, -NEG_INF, imp))
    rank = jnp.zeros((N_SEL, tq), F32)
    for jp in range(N_SEL):
        row = score[jp:jp + 1, :]
        beats = (row > score) | ((row == score) & (blk > jp))
        rank = rank + jnp.where(beats, 1.0, 0.0)
    sel_bias = jnp.where((rank < SEL_TOPN) & (blk <= cur), 0.0, -MASK_BIG)
    sel_bias = jnp.concatenate([sel_bias] * hg, axis=1)

    s1, s2, s3 = (piece.astype(F32) for piece in _split3(slope))
    r8 = lax.broadcasted_iota(jnp.int32, (8, nq), 0)
    srow = jnp.where((r8 == 0) | (r8 == 3), s1, jnp.where((r8 == 1) | (r8 == 4), s2, s3))
    srow = jnp.where(r8 < 6, srow, 0.0)
    pad = jnp.zeros((N_FEAT - N_SEL - 8, nq), F32)
    q_slc = jnp.concatenate([qt, jnp.concatenate([sel_bias, srow, pad], axis=0).astype(BF16)], axis=0)
    q_win = jnp.concatenate([qt, jnp.concatenate([jnp.zeros_like(sel_bias), srow, pad], axis=0).astype(BF16)],
                            axis=0)

    def scores(k_ref, start, size, q_aug):
        keys = jnp.concatenate([k_ref[0, pl.ds(start, size), :], feat_ref[pl.ds(start, size), :]], axis=1)
        return dot(keys, q_aug)

    def attend(s, v_t):
        p = jnp.exp(s - jnp.max(s, axis=0, keepdims=True))
        return dot(v_t, p.astype(BF16)) / jnp.maximum(jnp.sum(p, axis=0, keepdims=True), TINY)

    def slc_variant(n_chunks):
        size = n_chunks * KEY_CHUNK
        s = scores(ks_ref, 0, size, q_slc)
        kpos = (size - KEY_CHUNK) + lax.broadcasted_iota(jnp.int32, (KEY_CHUNK, 1), 0)
        last = jnp.where(kpos <= qpos, s[size - KEY_CHUNK:], -MASK_BIG)
        s = last if n_chunks == 1 else jnp.concatenate([s[:size - KEY_CHUNK], last], axis=0)
        slc_ref[...] = attend(s, vst_ref[0, :, 0:size])

    for n_chunks in range(1, SEQ // KEY_CHUNK + 1):
        pl.when(t0 // KEY_CHUNK == n_chunks - 1)(functools.partial(slc_variant, n_chunks))
    o_slc = slc_ref[...]

    span = WINDOW + tq
    start = pl.multiple_of(jnp.maximum(t0 - WINDOW, 0), LANES)
    dist_w = qpos - (start + lax.broadcasted_iota(jnp.int32, (span, 1), 0))
    s_w = jnp.where((dist_w >= 0) & (dist_w < WINDOW), scores(kw_ref, start, span, q_win), -MASK_BIG)
    o_win = attend(s_w, vwt_ref[0, :, pl.ds(start, span)])

    gt_ref[...] = _sigmoid(gate_ref[...]).T
    out = jnp.zeros((HEAD_DIM, nq), F32)
    for c, o in enumerate((o_cmp, o_slc, o_win)):
        g4 = gt_ref[pl.ds(c * NSA_HEADS + g * hg, hg), :]
        out = out + jnp.concatenate([g4[h:h + 1, :] for h in range(hg)], axis=1) * o
    o_ref[...] = jnp.concatenate([out[:, h * tq:(h + 1) * tq].T for h in range(hg)], axis=1)


def _nsa(proj, kcvc, ks, vst, kw, vwt, overlap_t, feat):
    tq = 128
    nq = SEQ // tq
    G = NSA_KV_HEADS
    kv_n = pl.BlockSpec((1, SEQ, HEAD_DIM), lambda b, g, i: (b * G + g, 0, 0))
    kv_t = pl.BlockSpec((1, HEAD_DIM, SEQ), lambda b, g, i: (b * G + g, 0, 0))
    cmp_spec = lambda c: pl.BlockSpec((1, 1, N_CMP_PAD, HEAD_DIM), lambda b, g, i, c=c: (c, b * G + g, 0, 0))
    return pl.pallas_call(
        functools.partial(_nsa_kernel, tq=tq),
        grid=(BATCH, G, nq),
        in_specs=[pl.BlockSpec((tq, NSA_GROUP * HEAD_DIM), lambda b, g, i: (b * nq + i, COL_Q // 256 + g)),
                  cmp_spec(0), cmp_spec(1), kv_n, kv_t, kv_n, kv_t,
                  pl.BlockSpec((tq, LANES), lambda b, g, i: (b * nq + i, COL_NG // LANES)),
                  pl.BlockSpec((N_SEL, N_CMP_PAD), lambda b, g, i: (0, 0)),
                  pl.BlockSpec((SEQ, N_FEAT), lambda b, g, i: (0, 0))],
        out_specs=pl.BlockSpec((tq, NSA_GROUP * HEAD_DIM), lambda b, g, i: (b * nq + i, g)),
        out_shape=jax.ShapeDtypeStruct((ROWS, NSA_Q_WIDTH), F32),
        scratch_shapes=[pltpu.VMEM((LANES, tq), F32), pltpu.VMEM((HEAD_DIM, NSA_GROUP * tq), F32)],
        compiler_params=_params(("parallel", "parallel", "parallel")),
        name="nsa",
    )(proj, kcvc, kcvc, ks, vst, kw, vwt, proj, overlap_t, feat)


def _mix_kernel(x_ref, ya_ref, yb_ref, ga_ref, gb_ref, wa_ref, wb_ref, wo_ref, h_ref):
    ma = _dot(ya_ref[...], wa_ref[...])
    mb = _dot(yb_ref[...], wb_ref[...])
    mixed = _sigmoid(ga_ref[...]) * ma + _sigmoid(gb_ref[...]) * mb
    h_ref[...] = x_ref[...] + _dot(mixed, wo_ref[...])


def _mix(x2, ya, yb, proj, wa, wb, wo):
    tm = 256
    row = lambda w, c=0: pl.BlockSpec((tm, w), lambda i, c=c: (i, c))
    const = lambda s: pl.BlockSpec(s, lambda i: (0, 0), pipeline_mode=pl.Buffered(1))
    return pl.pallas_call(
        _mix_kernel,
        grid=(ROWS // tm,),
        in_specs=[row(D_MODEL), row(1024), row(1024), row(D_MODEL, COL_GA // D_MODEL), row(D_MODEL, COL_GB // D_MODEL),
                  const((1024, D_MODEL)), const((1024, D_MODEL)), const((D_MODEL, D_MODEL))],
        out_specs=row(D_MODEL),
        out_shape=jax.ShapeDtypeStruct((ROWS, D_MODEL), F32),
        compiler_params=_params(("parallel",)),
        name="mix",
    )(x2, ya, yb, proj, proj, wa, wb, wo)


def _mlp_kernel(h_ref, gm_ref, wu_ref, wd_ref, gf_ref, o_ref, hn_ref, acc_ref):
    f = pl.program_id(1)

    @pl.when(f == 0)
    def _():
        h = h_ref[...]
        ms = jnp.mean(h * h, axis=-1, keepdims=True)
        hn_ref[...] = (h * lax.rsqrt(ms + NORM_EPS) * gm_ref[...]).astype(BF16)
        acc_ref[...] = jnp.zeros_like(acc_ref)

    u = jnp.maximum(jnp.dot(hn_ref[...], wu_ref[...], preferred_element_type=F32), 0.0)
    acc_ref[...] += jnp.dot((u * u).astype(BF16), wd_ref[...], preferred_element_type=F32)

    @pl.when(f == pl.num_programs(1) - 1)
    def _():
        h2 = h_ref[...] + acc_ref[...]
        ms = jnp.mean(h2 * h2, axis=-1, keepdims=True)
        o_ref[...] = h2 * lax.rsqrt(ms + NORM_EPS) * gf_ref[...]


def _mlp(h, g_mlp, w_up, w_down, g_final):
    tm, tf = 512, 1024
    return pl.pallas_call(
        _mlp_kernel,
        grid=(ROWS // tm, D_FF // tf),
        in_specs=[pl.BlockSpec((tm, D_MODEL), lambda i, f: (i, 0)),
                  pl.BlockSpec((1, D_MODEL), lambda i, f: (0, 0)),
                  pl.BlockSpec((D_MODEL, tf), lambda i, f: (0, f)),
                  pl.BlockSpec((tf, D_MODEL), lambda i, f: (f, 0)),
                  pl.BlockSpec((1, D_MODEL), lambda i, f: (0, 0))],
        out_specs=pl.BlockSpec((tm, D_MODEL), lambda i, f: (i, 0)),
        out_shape=jax.ShapeDtypeStruct((ROWS, D_MODEL), F32),
        scratch_shapes=[pltpu.VMEM((tm, D_MODEL), BF16), pltpu.VMEM((tm, D_MODEL), F32)],
        compiler_params=_params(("parallel", "arbitrary")),
        name="mlp",
    )(h, g_mlp, w_up, w_down, g_final)


def _pack_cols(w, pad):
    return jnp.concatenate([
        w[..., 6128:10224],
        w[..., 0:3168], pad(32),
        w[..., 3168:3264], pad(32),
        w[..., 3264:3520],
        w[..., 3520:6080],
        w[..., 6080:6128], pad(464),
    ], axis=-1)


def _overlap_matrix():
    cs = np.arange(N_CMP)[:, None] * CMP_STRIDE
    ss = np.arange(N_SEL)[None, :] * SEL_BLOCK
    ov = np.clip(np.minimum(cs + CMP_BLOCK, ss + SEL_BLOCK) - np.maximum(cs, ss), 0, None) / CMP_BLOCK
    out = np.zeros((N_CMP_PAD, N_SEL), np.float32)
    out[:N_CMP] = ov
    return out


def kernel(x, norm_mix, w_in, rwkv_mu, rwkv_w0, rwkv_w_up, rwkv_a0, rwkv_a_up, rwkv_g_up, rwkv_k_k, rwkv_k_a,
           rwkv_r_k, rwkv_lnx_w, rwkv_lnx_b, cmp_pe_k, cmp_w1_k, cmp_w2_k, cmp_pe_v, cmp_w1_v, cmp_w2_v,
           w_out_rwkv, w_out_nsa, w_o, norm_mlp, mlp_w_up, mlp_w_down, norm_final):
    assert x.shape == (BATCH, SEQ, D_MODEL) and w_in.shape[0] == 1
    l = 0
    G = NSA_KV_HEADS
    x2 = x.reshape(ROWS, D_MODEL)
    row = lambda a: a.reshape(1, -1)

    w_p = _pack_weight(w_in[l])
    proj = _inproj(x2, row(norm_mix[l]), w_p)

    mu = rwkv_mu[l]
    mu_p = jnp.concatenate([mu[0:3168], jnp.zeros((32,), F32), mu[3168:3264], jnp.zeros((32,), F32), mu[3264:3520]])
    zrows = jnp.zeros((32, RWKV_WIDTH), F32)
    ones = jnp.asarray(np.kron(np.eye(RWKV_HEADS), np.ones((HEAD_DIM, HEAD_DIM))), BF16)
    r, w, k, v, kk, b, g, bonus = _rwkvprep(
        proj, row(mu_p), row(rwkv_w0[l]), jnp.concatenate([rwkv_w_up[l], zrows]),
        row(rwkv_a0[l]), jnp.concatenate([rwkv_a_up[l], zrows]), rwkv_g_up[l],
        row(rwkv_k_k[l]), row(rwkv_k_a[l]), row(rwkv_r_k[l]), ones)
    kkn = jnp.concatenate([kk.reshape(BATCH, SEQ, RWKV_WIDTH)[:, 1:], jnp.zeros((BATCH, 1, RWKV_WIDTH), F32)], axis=1)
    y = _scan(_relayout(kkn), _relayout(w), _relayout(b), _relayout(k), _relayout(r), _relayout(v, time_major=True))
    ya = _rwkvpost(_relayout_back(y), bonus, g, row(rwkv_lnx_w[l]), row(rwkv_lnx_b[l]), ones)

    kv = lambda c0: proj[:, c0:c0 + NSA_KV_WIDTH].reshape(BATCH, SEQ, G, HEAD_DIM)
    chunks = jnp.stack([kv(COL_KC), kv(COL_VC)]).transpose(0, 1, 3, 2, 4)
    chunks = chunks.reshape(2, BATCH * G, N_CMP_PAD, CMP_STRIDE * HEAD_DIM)
    kcvc = _compress(chunks, jnp.stack([cmp_w1_k[l], cmp_w1_v[l]]),
                     jnp.stack([cmp_pe_k[l].reshape(1, -1), cmp_pe_v[l].reshape(1, -1)]),
                     jnp.stack([cmp_w2_k[l], cmp_w2_v[l]]))
    kv_b = lambda c0: lax.optimization_barrier(proj[:, c0:c0 + NSA_KV_WIDTH]).reshape(BATCH, SEQ, G, HEAD_DIM)
    keys = lambda c0: kv_b(c0).transpose(0, 2, 1, 3).reshape(BATCH * G, SEQ, HEAD_DIM).astype(BF16)
    vals_t = lambda c0: kv_b(c0).transpose(0, 2, 3, 1).reshape(BATCH * G, HEAD_DIM, SEQ).astype(BF16)
    yb = _nsa(proj, kcvc, keys(COL_KS), vals_t(COL_VS), keys(COL_KW), vals_t(COL_VW),
              jnp.asarray(_overlap_matrix().T, BF16), jnp.asarray(_key_features(), BF16))

    h = _mix(x2, ya, yb, proj, w_out_rwkv[l].astype(BF16), w_out_nsa[l].astype(BF16), w_o[l].astype(BF16))
    out = _mlp(h, row(norm_mlp[l]), mlp_w_up[l].astype(BF16), mlp_w_down[l].astype(BF16), row(norm_final))
    return out.reshape(BATCH, SEQ, D_MODEL)
```

```python
import functools

import numpy as np
import jax
import jax.numpy as jnp
from jax import lax
from jax.experimental import pallas as pl
from jax.experimental.pallas import tpu as pltpu

F32 = jnp.float32
BF16 = jnp.bfloat16

D_MODEL = 2048
BATCH = 4
SEQ = 2048
ROWS = BATCH * SEQ
RWKV_HEADS = 16
HEAD_DIM = 64
RWKV_WIDTH = RWKV_HEADS * HEAD_DIM
LORA_DECAY = 96
LORA_ICLR = 96
LORA_GATE = 256
GN_EPS = 64e-5
NSA_HEADS = 16
NSA_KV_HEADS = 4
NSA_GROUP = NSA_HEADS // NSA_KV_HEADS
NSA_Q_WIDTH = NSA_HEADS * HEAD_DIM
NSA_KV_WIDTH = NSA_KV_HEADS * HEAD_DIM
CMP_BLOCK = 32
CMP_STRIDE = 16
CMP_HIDDEN = 256
N_CMP = (SEQ - CMP_BLOCK) // CMP_STRIDE + 1
N_CMP_PAD = 128
SEL_BLOCK = 64
SEL_TOPN = 16
N_SEL = SEQ // SEL_BLOCK
WINDOW = 512
D_FF = 4 * D_MODEL
NORM_EPS = 1e-5
NEG_INF = -1e30
TINY = 1e-30

LANES = 128
VMEM_LIMIT = 56 * 1024 * 1024

COL_GA = 0
COL_GB = 2048
COL_R = 4096
COL_K = 5120
COL_V = 6144
COL_LORA = 7168
COL_Q = 7680
COL_KC = 8704
COL_VC = 8960
COL_KS = 9216
COL_VS = 9472
COL_KW = 9728
COL_VW = 9984
COL_NG = 10240
PROJ_COLS = 10752
LORA_PACK = 512


def _dot(a, b):
    return jnp.dot(a.astype(BF16), b.astype(BF16), preferred_element_type=F32)


def _sigmoid(x):
    return 1.0 / (1.0 + jnp.exp(-x))


def _params(sem, limit=VMEM_LIMIT):
    return pltpu.CompilerParams(dimension_semantics=sem, vmem_limit_bytes=limit)


def _inproj_kernel(x_ref, g_ref, w_ref, o_ref, xn_ref):
    @pl.when(pl.program_id(1) == 0)
    def _():
        x = x_ref[...]
        ms = jnp.mean(x * x, axis=-1, keepdims=True)
        xn_ref[...] = (x * lax.rsqrt(ms + NORM_EPS) * g_ref[...]).astype(BF16)

    o_ref[...] = jnp.dot(xn_ref[...], w_ref[...], preferred_element_type=F32)


def _pack_kernel(w_ref, o_ref):
    w = w_ref[...]
    o_ref[...] = _pack_cols(w, lambda n: jnp.zeros((w.shape[0], n), F32)).astype(BF16)


def _pack_weight(w):
    tk = 128
    return pl.pallas_call(
        _pack_kernel,
        grid=(D_MODEL // tk,),
        in_specs=[pl.BlockSpec((tk, w.shape[1]), lambda i: (i, 0))],
        out_specs=pl.BlockSpec((tk, PROJ_COLS), lambda i: (i, 0)),
        out_shape=jax.ShapeDtypeStruct((D_MODEL, PROJ_COLS), BF16),
        compiler_params=_params(("parallel",)),
        name="packw",
    )(w)


def _inproj(x2, g, w_p):
    tm, tn = 1024, 1536
    return pl.pallas_call(
        _inproj_kernel,
        grid=(ROWS // tm, PROJ_COLS // tn),
        in_specs=[pl.BlockSpec((tm, D_MODEL), lambda i, j: (i, 0)),
                  pl.BlockSpec((1, D_MODEL), lambda i, j: (0, 0)),
                  pl.BlockSpec((D_MODEL, tn), lambda i, j: (0, j))],
        out_specs=pl.BlockSpec((tm, tn), lambda i, j: (i, j)),
        out_shape=jax.ShapeDtypeStruct((ROWS, PROJ_COLS), F32),
        scratch_shapes=[pltpu.VMEM((tm, D_MODEL), BF16)],
        compiler_params=_params(("parallel", "arbitrary")),
        name="inproj",
    )(x2, g, w_p)


def _dot_hi_lo(x, m):
    hi = x.astype(BF16)
    lo = (x - hi.astype(F32)).astype(BF16)
    return jnp.dot(hi, m, preferred_element_type=F32) + jnp.dot(lo, m, preferred_element_type=F32)


def _head_sum(x, gather_ref, spread_ref):
    return _dot_hi_lo(_dot_hi_lo(x, gather_ref[...]), spread_ref[...])


def _rwkvprep_kernel(r_ref, k_ref, v_ref, lo_ref, rp_ref, kp_ref, vp_ref, lop_ref,
                     mu_ref, w0_ref, wup_ref, a0_ref, aup_ref, gup_ref, kk_ref, ka_ref, rk_ref, hg_ref, hs_ref,
                     ro_ref, wo_ref, ko_ref, vo_ref, kko_ref, bo_ref, go_ref, bon_ref, *, tm):
    i = pl.program_id(0)
    has_prev = jnp.where((i * tm) % SEQ == 0, 0.0, 1.0).astype(F32)

    def shift(cur_ref, prev_ref, mu):
        z = cur_ref[...]
        zp = pltpu.roll(z, 1, axis=0)
        prev_row = prev_ref[7:8, :] * has_prev
        row = lax.broadcasted_iota(jnp.int32, z.shape, 0)
        zp = jnp.where(row == 0, prev_row, zp)
        return z + (zp - z) * mu

    r = shift(r_ref, rp_ref, mu_ref[:, 0:1024])
    k = shift(k_ref, kp_ref, mu_ref[:, 1024:2048])
    v = shift(v_ref, vp_ref, mu_ref[:, 2048:3072])
    lo = shift(lo_ref, lop_ref, mu_ref[:, 3072:3584])
    w_lo, a_lo, g_lo = lo[:, 0:128], lo[:, 128:256], lo[:, 256:512]

    u = w0_ref[...] + _dot(jnp.tanh(w_lo), wup_ref[...])
    decay = jnp.exp(-(_sigmoid(u) * float(np.exp(-0.5))))
    a = _sigmoid(a0_ref[...] + _dot(a_lo, aup_ref[...]))
    g = _dot(_sigmoid(g_lo), gup_ref[...])

    kk = k * kk_ref[...]
    ss = _head_sum(kk * kk, hg_ref, hs_ref)
    kk = kk * lax.rsqrt(jnp.maximum(ss, 1e-24))
    k = k * (1.0 + (a - 1.0) * ka_ref[...])
    coef = _head_sum(r * k * rk_ref[...], hg_ref, hs_ref)

    ro_ref[...] = r
    wo_ref[...] = decay
    ko_ref[...] = k
    vo_ref[...] = v
    kko_ref[...] = kk
    bo_ref[...] = kk * a
    go_ref[...] = g
    bon_ref[...] = coef * v


def _rwkvprep(proj, mu_p, w0, wup_p, a0, aup_p, gup, k_k, k_a, r_k, head_gather, head_spread):
    tm = 256
    pb = tm // 8
    cur = lambda c: pl.BlockSpec((tm, 1024), lambda i, c=c: (i, c))
    prev = lambda c: pl.BlockSpec((8, 1024), lambda i, c=c: (jnp.maximum(i * pb - 1, 0), c))
    full = lambda s: pl.BlockSpec(s, lambda i: (0,) * len(s))
    out = pl.BlockSpec((tm, 1024), lambda i: (i, 0))
    osh = jax.ShapeDtypeStruct((ROWS, RWKV_WIDTH), F32)
    return pl.pallas_call(
        functools.partial(_rwkvprep_kernel, tm=tm),
        grid=(ROWS // tm,),
        in_specs=[cur(COL_R // 1024), cur(COL_K // 1024), cur(COL_V // 1024),
                  pl.BlockSpec((tm, LORA_PACK), lambda i: (i, COL_LORA // LORA_PACK)),
                  prev(COL_R // 1024), prev(COL_K // 1024), prev(COL_V // 1024),
                  pl.BlockSpec((8, LORA_PACK), lambda i: (jnp.maximum(i * pb - 1, 0), COL_LORA // LORA_PACK)),
                  full((1, 3584)), full((1, 1024)), full((128, 1024)), full((1, 1024)), full((128, 1024)),
                  full((256, 1024)), full((1, 1024)), full((1, 1024)), full((1, 1024)),
                  full((RWKV_WIDTH, LANES)), full((LANES, RWKV_WIDTH))],
        out_specs=[out] * 8,
        out_shape=[osh] * 8,
        compiler_params=_params(("parallel",)),
        name="rwkvprep",
    )(proj, proj, proj, proj, proj, proj, proj, proj,
      mu_p, w0, wup_p, a0, aup_p, gup, k_k, k_a, r_k, head_gather, head_spread)


N_JPAIR = HEAD_DIM // 2
N_JOPS = 5


def _scan_kernel(kkn_ref, w_ref, b_ref, k_ref, r_ref, v_ref, y_ref, s_ref, sa_ref, d_ref, *, tc):
    @pl.when(pl.program_id(0) == 0)
    def _():
        s_ref[...] = jnp.zeros_like(s_ref)
        sa_ref[...] = jnp.zeros_like(sa_ref)

    lo_half = lax.broadcasted_iota(jnp.int32, (N_JPAIR, tc, LANES), 2) < 64
    for n, ref in enumerate((kkn_ref, w_ref, b_ref, k_ref, r_ref)):
        x = ref[...]
        xr = pltpu.roll(x, 64, axis=2)
        d_ref[n, 0:N_JPAIR] = jnp.where(lo_half, x, xr)
        d_ref[n, N_JPAIR:HEAD_DIM] = jnp.where(lo_half, xr, x)

    def tree(parts):
        while len(parts) > 1:
            parts = [parts[n] + parts[n + 1] for n in range(0, len(parts), 2)]
        return parts[0]

    def step(t, sa):
        vt = v_ref[t]
        acc_y = [None] * 4
        acc_s = [None] * 4
        for j in range(HEAD_DIM):
            row = lambda n: d_ref[n, j, pl.ds(t, 1), :]
            sn = s_ref[j] * row(1) - sa * row(2) + vt * row(3)
            s_ref[j] = sn
            ty = sn * row(4)
            ts = sn * row(0)
            acc_y[j % 4] = ty if acc_y[j % 4] is None else acc_y[j % 4] + ty
            acc_s[j % 4] = ts if acc_s[j % 4] is None else acc_s[j % 4] + ts
        y_ref[t] = tree(acc_y)
        return tree(acc_s)

    sa_ref[...] = lax.fori_loop(0, tc, step, sa_ref[...])


def _scan(kkn, w, b, k, r, v):
    tc = 64
    spec = pl.BlockSpec((tc, N_JPAIR, LANES), lambda i: (i, 0, 0))
    jspec = pl.BlockSpec((N_JPAIR, tc, LANES), lambda i: (0, i, 0))
    return pl.pallas_call(
        functools.partial(_scan_kernel, tc=tc),
        grid=(SEQ // tc,),
        in_specs=[jspec] * N_JOPS + [spec],
        out_specs=spec,
        out_shape=jax.ShapeDtypeStruct((SEQ, N_JPAIR, LANES), F32),
        scratch_shapes=[pltpu.VMEM((HEAD_DIM, N_JPAIR, LANES), F32),
                        pltpu.VMEM((N_JPAIR, LANES), F32),
                        pltpu.VMEM((N_JOPS, HEAD_DIM, tc, LANES), F32)],
        compiler_params=_params(("arbitrary",)),
        name="scan",
    )(kkn, w, b, k, r, v)


RELAYOUT_T = 128
_HALF_BATCH = [(half, b) for half in range(2) for b in range(BATCH)]


def _gather_lanes(z_ref, p):
    return jnp.concatenate([z_ref[b, pl.ds(half * N_JPAIR + p, RWKV_HEADS, stride=HEAD_DIM), :]
                            for half, b in _HALF_BATCH], axis=0)


def _relayout_kernel(x_ref, o_ref, z_ref, *, time_major):
    for b in range(BATCH):
        z_ref[b] = x_ref[b].T
    for p in range(N_JPAIR):
        tile = _gather_lanes(z_ref, p).T
        if time_major:
            o_ref[pl.ds(p, RELAYOUT_T, stride=N_JPAIR), :] = tile
        else:
            o_ref[p] = tile


def _relayout_back_kernel(y_ref, o_ref, z_ref):
    for p in range(N_JPAIR):
        tile = y_ref[pl.ds(p, RELAYOUT_T, stride=N_JPAIR), :].T
        for n, (half, b) in enumerate(_HALF_BATCH):
            z_ref[b, pl.ds(half * N_JPAIR + p, RWKV_HEADS, stride=HEAD_DIM), :] = (
                tile[n * RWKV_HEADS:(n + 1) * RWKV_HEADS])
    for b in range(BATCH):
        o_ref[b] = z_ref[b].T


_NATURAL_SPEC = pl.BlockSpec((BATCH, RELAYOUT_T, RWKV_WIDTH), lambda i: (0, i, 0))
_TIME_MAJOR_SPEC = pl.BlockSpec((RELAYOUT_T * N_JPAIR, LANES), lambda i: (i, 0))
_RELAYOUT_SCRATCH = [pltpu.VMEM((BATCH, RWKV_WIDTH, RELAYOUT_T), F32)]


def _relayout(a, time_major=False):
    if time_major:
        out_spec, out_shape = _TIME_MAJOR_SPEC, (SEQ * N_JPAIR, LANES)
    else:
        out_spec, out_shape = pl.BlockSpec((N_JPAIR, RELAYOUT_T, LANES), lambda i: (0, i, 0)), (N_JPAIR, SEQ, LANES)
    out = pl.pallas_call(
        functools.partial(_relayout_kernel, time_major=time_major),
        grid=(SEQ // RELAYOUT_T,),
        in_specs=[_NATURAL_SPEC],
        out_specs=out_spec,
        out_shape=jax.ShapeDtypeStruct(out_shape, F32),
        scratch_shapes=_RELAYOUT_SCRATCH,
        compiler_params=_params(("parallel",)),
        name="relayout",
    )(a.reshape(BATCH, SEQ, RWKV_WIDTH))
    return out.reshape(SEQ, N_JPAIR, LANES) if time_major else out


def _relayout_back(y):
    return pl.pallas_call(
        _relayout_back_kernel,
        grid=(SEQ // RELAYOUT_T,),
        in_specs=[_TIME_MAJOR_SPEC],
        out_specs=_NATURAL_SPEC,
        out_shape=jax.ShapeDtypeStruct((BATCH, SEQ, RWKV_WIDTH), F32),
        scratch_shapes=_RELAYOUT_SCRATCH,
        compiler_params=_params(("parallel",)),
        name="relayout_back",
    )(y.reshape(SEQ * N_JPAIR, LANES)).reshape(ROWS, RWKV_WIDTH)


def _rwkvpost_kernel(y_ref, bon_ref, g_ref, lw_ref, lb_ref, hg_ref, hs_ref, o_ref):
    y = y_ref[...]
    mean = _head_sum(y, hg_ref, hs_ref) * (1.0 / HEAD_DIM)
    d = y - mean
    var = _head_sum(d * d, hg_ref, hs_ref) * (1.0 / HEAD_DIM)
    yn = d * lax.rsqrt(var + GN_EPS) * lw_ref[...] + lb_ref[...]
    o_ref[...] = (yn + bon_ref[...]) * g_ref[...]


def _rwkvpost(y, bonus, g, lnx_w, lnx_b, head_gather, head_spread):
    tm = 512
    blk = pl.BlockSpec((tm, 1024), lambda i: (i, 0))
    vec = pl.BlockSpec((1, 1024), lambda i: (0, 0))
    return pl.pallas_call(
        _rwkvpost_kernel,
        grid=(ROWS // tm,),
        in_specs=[blk, blk, blk, vec, vec, pl.BlockSpec((RWKV_WIDTH, LANES), lambda i: (0, 0)),
                  pl.BlockSpec((LANES, RWKV_WIDTH), lambda i: (0, 0))],
        out_specs=blk,
        out_shape=jax.ShapeDtypeStruct((ROWS, RWKV_WIDTH), F32),
        compiler_params=_params(("parallel",)),
        name="rwkvpost",
    )(y, bonus, g, lnx_w, lnx_b, head_gather, head_spread)


def _compress_kernel(c_ref, w1_ref, pe_ref, w2_ref, o_ref):
    ch = c_ref[0, 0]
    w1 = w1_ref[0]
    half = CMP_STRIDE * HEAD_DIM
    top = _dot(ch, w1[:half])
    bot = _dot(ch, w1[half:])
    bias = _dot(jnp.broadcast_to(pe_ref[0], (8, CMP_BLOCK * HEAD_DIM)), w1)[0:1]
    hid = top + pltpu.roll(bot, N_CMP_PAD - 1, axis=0) + bias
    c0 = float(np.sqrt(2.0 / np.pi))
    act = 0.5 * hid * (1.0 + jnp.tanh(c0 * (hid + 0.044715 * (hid * hid * hid))))
    o_ref[0, 0] = _dot(act, w2_ref[0])


def _compress(chunks, w1, pe, w2):
    nbg = BATCH * NSA_KV_HEADS
    return pl.pallas_call(
        _compress_kernel,
        grid=(2, nbg),
        in_specs=[pl.BlockSpec((1, 1, N_CMP_PAD, CMP_STRIDE * HEAD_DIM), lambda c, n: (c, n, 0, 0)),
                  pl.BlockSpec((1, CMP_BLOCK * HEAD_DIM, CMP_HIDDEN), lambda c, n: (c, 0, 0)),
                  pl.BlockSpec((1, 1, CMP_BLOCK * HEAD_DIM), lambda c, n: (c, 0, 0)),
                  pl.BlockSpec((1, CMP_HIDDEN, HEAD_DIM), lambda c, n: (c, 0, 0))],
        out_specs=pl.BlockSpec((1, 1, N_CMP_PAD, HEAD_DIM), lambda c, n: (c, n, 0, 0)),
        out_shape=jax.ShapeDtypeStruct((2, nbg, N_CMP_PAD, HEAD_DIM), F32),
        compiler_params=_params(("parallel", "parallel")),
        name="compress",
    )(chunks, w1, pe, w2)


MASK_BIG = float(2.0 ** 100)
KEY_CHUNK = 512
N_FEAT = 64


def _key_features():
    s = np.arange(SEQ)
    f = np.zeros((SEQ, N_FEAT), np.float32)
    f[s, s // SEL_BLOCK] = 1.0
    f[:, 32:35] = (SEL_BLOCK * (s // SEL_BLOCK))[:, None]
    f[:, 35:38] = (s % SEL_BLOCK)[:, None]
    return f


def _split3(x):
    a1 = x.astype(BF16)
    r1 = x - a1.astype(F32)
    a2 = r1.astype(BF16)
    a3 = (r1 - a2.astype(F32)).astype(BF16)
    return a1, a2, a3


def _nsa_kernel(q_ref, kc_ref, vc_ref, ks_ref, vst_ref, kw_ref, vwt_ref, gate_ref, ovt_ref, feat_ref, o_ref,
                gt_ref, slc_ref, *, tq):
    g = pl.program_id(1)
    t0 = pl.program_id(2) * tq
    hg = NSA_GROUP
    nq = hg * tq
    dot = functools.partial(jnp.dot, preferred_element_type=F32)

    qt = (q_ref[...] * HEAD_DIM ** -0.5).T
    qt = jnp.concatenate([qt[h * HEAD_DIM:(h + 1) * HEAD_DIM] for h in range(hg)], axis=1).astype(BF16)
    lane = lax.broadcasted_iota(jnp.int32, (1, nq), 1)
    slope = jnp.exp2(-0.5 * (g * hg + lane // tq + 1).astype(F32))
    qpos = t0 + lane % tq

    n_idx = lax.broadcasted_iota(jnp.int32, (N_CMP_PAD, 1), 0)
    dist_c = qpos.astype(F32) - (n_idx * CMP_STRIDE + (CMP_BLOCK - 1)).astype(F32)
    mask_c = (dist_c >= 0) & (n_idx < N_CMP)
    x_c = jnp.where(mask_c, dot(kc_ref[0, 0].astype(BF16), qt) - slope * dist_c, NEG_INF)
    p_c = jnp.where(mask_c, jnp.exp(x_c - jnp.max(x_c, axis=0, keepdims=True)), 0.0)
    den_c = jnp.maximum(jnp.sum(p_c, axis=0, keepdims=True), TINY)
    o_cmp = dot(vc_ref[0, 0].T.astype(BF16), p_c.astype(BF16)) / den_c

    p_n = p_c / den_c
    p_sum = (p_n[:, 0:tq] + p_n[:, tq:2 * tq]) + (p_n[:, 2 * tq:3 * tq] + p_n[:, 3 * tq:4 * tq])
    ovt = ovt_ref[...]
    imp = sum(dot(ovt, part) for part in _split3(p_sum))
    blk = lax.broadcasted_iota(jnp.int32, (N_SEL, tq), 0)
    cur = (t0 + lax.broadcasted_iota(jnp.int32, (N_SEL, tq), 1)) // SEL_BLOCK
    forced = (blk == 0) | (blk == cur) | (blk == cur - 1)
    score = jnp.where(blk > cur, NEG_INF, jnp.where(forced, -NEG_INF, imp))
    rank = jnp.zeros((N_SEL, tq), F32)
    for jp in range(N_SEL):
        row = score[jp:jp + 1, :]
        beats = (row > score) | ((row == score) & (blk > jp))
        rank = rank + jnp.where(beats, 1.0, 0.0)
    sel_bias = jnp.where((rank < SEL_TOPN) & (blk <= cur), 0.0, -MASK_BIG)
    sel_bias = jnp.concatenate([sel_bias] * hg, axis=1)

    s1, s2, s3 = (piece.astype(F32) for piece in _split3(slope))
    r8 = lax.broadcasted_iota(jnp.int32, (8, nq), 0)
    srow = jnp.where((r8 == 0) | (r8 == 3), s1, jnp.where((r8 == 1) | (r8 == 4), s2, s3))
    srow = jnp.where(r8 < 6, srow, 0.0)
    pad = jnp.zeros((N_FEAT - N_SEL - 8, nq), F32)
    q_slc = jnp.concatenate([qt, jnp.concatenate([sel_bias, srow, pad], axis=0).astype(BF16)], axis=0)
    q_win = jnp.concatenate([qt, jnp.concatenate([jnp.zeros_like(sel_bias), srow, pad], axis=0).astype(BF16)],
                            axis=0)

    def scores(k_ref, start, size, q_aug):
        keys = jnp.concatenate([k_ref[0, pl.ds(start, size), :], feat_ref[pl.ds(start, size), :]], axis=1)
        return dot(keys, q_aug)

    def attend(s, v_t):
        p = jnp.exp(s - jnp.max(s, axis=0, keepdims=True))
        return dot(v_t, p.astype(BF16)) / jnp.maximum(jnp.sum(p, axis=0, keepdims=True), TINY)

    def slc_variant(n_chunks):
        size = n_chunks * KEY_CHUNK
        s = scores(ks_ref, 0, size, q_slc)
        kpos = (size - KEY_CHUNK) + lax.broadcasted_iota(jnp.int32, (KEY_CHUNK, 1), 0)
        last = jnp.where(kpos <= qpos, s[size - KEY_CHUNK:], -MASK_BIG)
        s = last if n_chunks == 1 else jnp.concatenate([s[:size - KEY_CHUNK], last], axis=0)
        slc_ref[...] = attend(s, vst_ref[0, :, 0:size])

    for n_chunks in range(1, SEQ // KEY_CHUNK + 1):
        pl.when(t0 // KEY_CHUNK == n_chunks - 1)(functools.partial(slc_variant, n_chunks))
    o_slc = slc_ref[...]

    span = WINDOW + tq
    start = pl.multiple_of(jnp.maximum(t0 - WINDOW, 0), LANES)
    dist_w = qpos - (start + lax.broadcasted_iota(jnp.int32, (span, 1), 0))
    s_w = jnp.where((dist_w >= 0) & (dist_w < WINDOW), scores(kw_ref, start, span, q_win), -MASK_BIG)
    o_win = attend(s_w, vwt_ref[0, :, pl.ds(start, span)])

    gt_ref[...] = _sigmoid(gate_ref[...]).T
    out = jnp.zeros((HEAD_DIM, nq), F32)
    for c, o in enumerate((o_cmp, o_slc, o_win)):
        g8 = gt_ref[pl.ds(pl.multiple_of(c * NSA_HEADS + (g // 2) * 8, 8), 8), :]
        g4 = jnp.where(g % 2 == 0, g8[0:hg], g8[hg:2 * hg])
        out = out + jnp.concatenate([g4[h:h + 1, :] for h in range(hg)], axis=1) * o
    o_ref[...] = jnp.concatenate([out[:, h * tq:(h + 1) * tq].T for h in range(hg)], axis=1)


def _nsa(proj, kcvc, ks, vst, kw, vwt, overlap_t, feat):
    tq = 256
    nq = SEQ // tq
    G = NSA_KV_HEADS
    kv_n = pl.BlockSpec((1, SEQ, HEAD_DIM), lambda b, g, i: (b * G + g, 0, 0))
    kv_t = pl.BlockSpec((1, HEAD_DIM, SEQ), lambda b, g, i: (b * G + g, 0, 0))
    cmp_spec = lambda c: pl.BlockSpec((1, 1, N_CMP_PAD, HEAD_DIM), lambda b, g, i, c=c: (c, b * G + g, 0, 0))
    return pl.pallas_call(
        functools.partial(_nsa_kernel, tq=tq),
        grid=(BATCH, G, nq),
        in_specs=[pl.BlockSpec((tq, NSA_GROUP * HEAD_DIM), lambda b, g, i: (b * nq + i, COL_Q // 256 + g)),
                  cmp_spec(0), cmp_spec(1), kv_n, kv_t, kv_n, kv_t,
                  pl.BlockSpec((tq, LANES), lambda b, g, i: (b * nq + i, COL_NG // LANES)),
                  pl.BlockSpec((N_SEL, N_CMP_PAD), lambda b, g, i: (0, 0)),
                  pl.BlockSpec((SEQ, N_FEAT), lambda b, g, i: (0, 0))],
        out_specs=pl.BlockSpec((tq, NSA_GROUP * HEAD_DIM), lambda b, g, i: (b * nq + i, g)),
        out_shape=jax.ShapeDtypeStruct((ROWS, NSA_Q_WIDTH), F32),
        scratch_shapes=[pltpu.VMEM((LANES, tq), F32), pltpu.VMEM((HEAD_DIM, NSA_GROUP * tq), F32)],
        compiler_params=_params(("parallel", "parallel", "parallel")),
        name="nsa",
    )(proj, kcvc, kcvc, ks, vst, kw, vwt, proj, overlap_t, feat)


def _mix_kernel(x_ref, ya_ref, yb_ref, ga_ref, gb_ref, wa_ref, wb_ref, wo_ref, h_ref):
    ma = _dot(ya_ref[...], wa_ref[...])
    mb = _dot(yb_ref[...], wb_ref[...])
    mixed = _sigmoid(ga_ref[...]) * ma + _sigmoid(gb_ref[...]) * mb
    h_ref[...] = x_ref[...] + _dot(mixed, wo_ref[...])


def _mix(x2, ya, yb, proj, wa, wb, wo):
    tm = 256
    row = lambda w, c=0: pl.BlockSpec((tm, w), lambda i, c=c: (i, c))
    const = lambda s: pl.BlockSpec(s, lambda i: (0, 0), pipeline_mode=pl.Buffered(1))
    return pl.pallas_call(
        _mix_kernel,
        grid=(ROWS // tm,),
        in_specs=[row(D_MODEL), row(1024), row(1024), row(D_MODEL, COL_GA // D_MODEL), row(D_MODEL, COL_GB // D_MODEL),
                  const((1024, D_MODEL)), const((1024, D_MODEL)), const((D_MODEL, D_MODEL))],
        out_specs=row(D_MODEL),
        out_shape=jax.ShapeDtypeStruct((ROWS, D_MODEL), F32),
        compiler_params=_params(("parallel",)),
        name="mix",
    )(x2, ya, yb, proj, proj, wa, wb, wo)


def _mlp_kernel(h_ref, gm_ref, wu_ref, wd_ref, gf_ref, o_ref, hn_ref, acc_ref):
    f = pl.program_id(1)

    @pl.when(f == 0)
    def _():
        h = h_ref[...]
        ms = jnp.mean(h * h, axis=-1, keepdims=True)
        hn_ref[...] = (h * lax.rsqrt(ms + NORM_EPS) * gm_ref[...]).astype(BF16)
        acc_ref[...] = jnp.zeros_like(acc_ref)

    u = jnp.maximum(jnp.dot(hn_ref[...], wu_ref[...], preferred_element_type=F32), 0.0)
    acc_ref[...] += jnp.dot((u * u).astype(BF16), wd_ref[...], preferred_element_type=F32)

    @pl.when(f == pl.num_programs(1) - 1)
    def _():
        h2 = h_ref[...] + acc_ref[...]
        ms = jnp.mean(h2 * h2, axis=-1, keepdims=True)
        o_ref[...] = h2 * lax.rsqrt(ms + NORM_EPS) * gf_ref[...]


def _mlp(h, g_mlp, w_up, w_down, g_final):
    tm, tf = 512, 1024
    return pl.pallas_call(
        _mlp_kernel,
        grid=(ROWS // tm, D_FF // tf),
        in_specs=[pl.BlockSpec((tm, D_MODEL), lambda i, f: (i, 0)),
                  pl.BlockSpec((1, D_MODEL), lambda i, f: (0, 0)),
                  pl.BlockSpec((D_MODEL, tf), lambda i, f: (0, f)),
                  pl.BlockSpec((tf, D_MODEL), lambda i, f: (f, 0)),
                  pl.BlockSpec((1, D_MODEL), lambda i, f: (0, 0))],
        out_specs=pl.BlockSpec((tm, D_MODEL), lambda i, f: (i, 0)),
        out_shape=jax.ShapeDtypeStruct((ROWS, D_MODEL), F32),
        scratch_shapes=[pltpu.VMEM((tm, D_MODEL), BF16), pltpu.VMEM((tm, D_MODEL), F32)],
        compiler_params=_params(("parallel", "arbitrary")),
        name="mlp",
    )(h, g_mlp, w_up, w_down, g_final)


def _pack_cols(w, pad):
    return jnp.concatenate([
        w[..., 6128:10224],
        w[..., 0:3168], pad(32),
        w[..., 3168:3264], pad(32),
        w[..., 3264:3520],
        w[..., 3520:6080],
        w[..., 6080:6128], pad(464),
    ], axis=-1)


def _overlap_matrix():
    cs = np.arange(N_CMP)[:, None] * CMP_STRIDE
    ss = np.arange(N_SEL)[None, :] * SEL_BLOCK
    ov = np.clip(np.minimum(cs + CMP_BLOCK, ss + SEL_BLOCK) - np.maximum(cs, ss), 0, None) / CMP_BLOCK
    out = np.zeros((N_CMP_PAD, N_SEL), np.float32)
    out[:N_CMP] = ov
    return out


def kernel(x, norm_mix, w_in, rwkv_mu, rwkv_w0, rwkv_w_up, rwkv_a0, rwkv_a_up, rwkv_g_up, rwkv_k_k, rwkv_k_a,
           rwkv_r_k, rwkv_lnx_w, rwkv_lnx_b, cmp_pe_k, cmp_w1_k, cmp_w2_k, cmp_pe_v, cmp_w1_v, cmp_w2_v,
           w_out_rwkv, w_out_nsa, w_o, norm_mlp, mlp_w_up, mlp_w_down, norm_final):
    assert x.shape == (BATCH, SEQ, D_MODEL) and w_in.shape[0] == 1
    l = 0
    G = NSA_KV_HEADS
    x2 = x.reshape(ROWS, D_MODEL)
    row = lambda a: a.reshape(1, -1)

    w_p = _pack_weight(w_in[l])
    proj = _inproj(x2, row(norm_mix[l]), w_p)

    mu = rwkv_mu[l]
    mu_p = jnp.concatenate([mu[0:3168], jnp.zeros((32,), F32), mu[3168:3264], jnp.zeros((32,), F32), mu[3264:3520]])
    zrows = jnp.zeros((32, RWKV_WIDTH), F32)
    head_of_lane = np.arange(RWKV_WIDTH)[:, None] // HEAD_DIM == np.arange(LANES)[None, :]
    head_gather = jnp.asarray(head_of_lane, BF16)
    head_spread = jnp.asarray(head_of_lane.T, BF16)
    r, w, k, v, kk, b, g, bonus = _rwkvprep(
        proj, row(mu_p), row(rwkv_w0[l]), jnp.concatenate([rwkv_w_up[l], zrows]),
        row(rwkv_a0[l]), jnp.concatenate([rwkv_a_up[l], zrows]), rwkv_g_up[l],
        row(rwkv_k_k[l]), row(rwkv_k_a[l]), row(rwkv_r_k[l]), head_gather, head_spread)
    kkn = jnp.concatenate([kk.reshape(BATCH, SEQ, RWKV_WIDTH)[:, 1:], jnp.zeros((BATCH, 1, RWKV_WIDTH), F32)], axis=1)
    y = _scan(_relayout(kkn), _relayout(w), _relayout(b), _relayout(k), _relayout(r), _relayout(v, time_major=True))
    ya = _rwkvpost(_relayout_back(y), bonus, g, row(rwkv_lnx_w[l]), row(rwkv_lnx_b[l]), head_gather, head_spread)

    kv = lambda c0: proj[:, c0:c0 + NSA_KV_WIDTH].reshape(BATCH, SEQ, G, HEAD_DIM)
    chunks = jnp.stack([kv(COL_KC), kv(COL_VC)]).transpose(0, 1, 3, 2, 4)
    chunks = chunks.reshape(2, BATCH * G, N_CMP_PAD, CMP_STRIDE * HEAD_DIM)
    kcvc = _compress(chunks, jnp.stack([cmp_w1_k[l], cmp_w1_v[l]]),
                     jnp.stack([cmp_pe_k[l].reshape(1, -1), cmp_pe_v[l].reshape(1, -1)]),
                     jnp.stack([cmp_w2_k[l], cmp_w2_v[l]]))
    kv_b = lambda c0: lax.optimization_barrier(proj[:, c0:c0 + NSA_KV_WIDTH]).reshape(BATCH, SEQ, G, HEAD_DIM)
    keys = lambda c0: kv_b(c0).transpose(0, 2, 1, 3).reshape(BATCH * G, SEQ, HEAD_DIM).astype(BF16)
    vals_t = lambda c0: kv_b(c0).transpose(0, 2, 3, 1).reshape(BATCH * G, HEAD_DIM, SEQ).astype(BF16)
    yb = _nsa(proj, kcvc, keys(COL_KS), vals_t(COL_VS), keys(COL_KW), vals_t(COL_VW),
              jnp.asarray(_overlap_matrix().T, BF16), jnp.asarray(_key_features(), BF16))

    h = _mix(x2, ya, yb, proj, w_out_rwkv[l].astype(BF16), w_out_nsa[l].astype(BF16), w_o[l].astype(BF16))
    out = _mlp(h, row(norm_mlp[l]), mlp_w_up[l].astype(BF16), mlp_w_down[l].astype(BF16), row(norm_final))
    return out.reshape(BATCH, SEQ, D_MODEL)
```

```python
import functools

import numpy as np
import jax
import jax.numpy as jnp
from jax import lax
from jax.experimental import pallas as pl
from jax.experimental.pallas import tpu as pltpu

F32 = jnp.float32
BF16 = jnp.bfloat16

D_MODEL = 2048
BATCH = 4
SEQ = 2048
ROWS = BATCH * SEQ
RWKV_HEADS = 16
HEAD_DIM = 64
RWKV_WIDTH = RWKV_HEADS * HEAD_DIM
LORA_DECAY = 96
LORA_ICLR = 96
LORA_GATE = 256
GN_EPS = 64e-5
NSA_HEADS = 16
NSA_KV_HEADS = 4
NSA_GROUP = NSA_HEADS // NSA_KV_HEADS
NSA_Q_WIDTH = NSA_HEADS * HEAD_DIM
NSA_KV_WIDTH = NSA_KV_HEADS * HEAD_DIM
CMP_BLOCK = 32
CMP_STRIDE = 16
CMP_HIDDEN = 256
N_CMP = (SEQ - CMP_BLOCK) // CMP_STRIDE + 1
N_CMP_PAD = 128
SEL_BLOCK = 64
SEL_TOPN = 16
N_SEL = SEQ // SEL_BLOCK
WINDOW = 512
D_FF = 4 * D_MODEL
NORM_EPS = 1e-5
NEG_INF = -1e30
TINY = 1e-30

LANES = 128
VMEM_LIMIT = 56 * 1024 * 1024

COL_GA = 0
COL_GB = 2048
COL_R = 4096
COL_K = 5120
COL_V = 6144
COL_LORA = 7168
COL_Q = 7680
COL_KC = 8704
COL_VC = 8960
COL_KS = 9216
COL_VS = 9472
COL_KW = 9728
COL_VW = 9984
COL_NG = 10240
PROJ_COLS = 10752
LORA_PACK = 512


def _dot(a, b):
    return jnp.dot(a.astype(BF16), b.astype(BF16), preferred_element_type=F32)


def _sigmoid(x):
    return 1.0 / (1.0 + jnp.exp(-x))


def _params(sem, limit=VMEM_LIMIT):
    return pltpu.CompilerParams(dimension_semantics=sem, vmem_limit_bytes=limit)


def _inproj_kernel(x_ref, g_ref, w_ref, o_ref, xn_ref):
    @pl.when(pl.program_id(1) == 0)
    def _():
        x = x_ref[...]
        ms = jnp.mean(x * x, axis=-1, keepdims=True)
        xn_ref[...] = (x * lax.rsqrt(ms + NORM_EPS) * g_ref[...]).astype(BF16)

    o_ref[...] = jnp.dot(xn_ref[...], w_ref[...], preferred_element_type=F32)


def _pack_kernel(w_ref, o_ref):
    w = w_ref[...]
    o_ref[...] = _pack_cols(w, lambda n: jnp.zeros((w.shape[0], n), F32)).astype(BF16)


def _pack_weight(w):
    tk = 128
    return pl.pallas_call(
        _pack_kernel,
        grid=(D_MODEL // tk,),
        in_specs=[pl.BlockSpec((tk, w.shape[1]), lambda i: (i, 0))],
        out_specs=pl.BlockSpec((tk, PROJ_COLS), lambda i: (i, 0)),
        out_shape=jax.ShapeDtypeStruct((D_MODEL, PROJ_COLS), BF16),
        compiler_params=_params(("parallel",)),
        name="packw",
    )(w)


def _inproj(x2, g, w_p):
    tm, tn = 1024, 1536
    return pl.pallas_call(
        _inproj_kernel,
        grid=(ROWS // tm, PROJ_COLS // tn),
        in_specs=[pl.BlockSpec((tm, D_MODEL), lambda i, j: (i, 0)),
                  pl.BlockSpec((1, D_MODEL), lambda i, j: (0, 0)),
                  pl.BlockSpec((D_MODEL, tn), lambda i, j: (0, j))],
        out_specs=pl.BlockSpec((tm, tn), lambda i, j: (i, j)),
        out_shape=jax.ShapeDtypeStruct((ROWS, PROJ_COLS), F32),
        scratch_shapes=[pltpu.VMEM((tm, D_MODEL), BF16)],
        compiler_params=_params(("parallel", "arbitrary")),
        name="inproj",
    )(x2, g, w_p)


def _dot_hi_lo(x, m):
    hi = x.astype(BF16)
    lo = (x - hi.astype(F32)).astype(BF16)
    return jnp.dot(hi, m, preferred_element_type=F32) + jnp.dot(lo, m, preferred_element_type=F32)


def _head_sum(x, gather_ref, spread_ref):
    return _dot_hi_lo(_dot_hi_lo(x, gather_ref[...]), spread_ref[...])


def _rwkvprep_kernel(r_ref, k_ref, v_ref, lo_ref, rp_ref, kp_ref, vp_ref, lop_ref,
                     mu_ref, w0_ref, wup_ref, a0_ref, aup_ref, gup_ref, kk_ref, ka_ref, rk_ref, hg_ref, hs_ref,
                     ro_ref, wo_ref, ko_ref, vo_ref, kko_ref, bo_ref, go_ref, bon_ref, *, tm):
    i = pl.program_id(0)
    has_prev = jnp.where((i * tm) % SEQ == 0, 0.0, 1.0).astype(F32)

    def shift(cur_ref, prev_ref, mu):
        z = cur_ref[...]
        zp = pltpu.roll(z, 1, axis=0)
        prev_row = prev_ref[7:8, :] * has_prev
        row = lax.broadcasted_iota(jnp.int32, z.shape, 0)
        zp = jnp.where(row == 0, prev_row, zp)
        return z + (zp - z) * mu

    r = shift(r_ref, rp_ref, mu_ref[:, 0:1024])
    k = shift(k_ref, kp_ref, mu_ref[:, 1024:2048])
    v = shift(v_ref, vp_ref, mu_ref[:, 2048:3072])
    lo = shift(lo_ref, lop_ref, mu_ref[:, 3072:3584])
    w_lo, a_lo, g_lo = lo[:, 0:128], lo[:, 128:256], lo[:, 256:512]

    u = w0_ref[...] + _dot(jnp.tanh(w_lo), wup_ref[...])
    decay = jnp.exp(-(_sigmoid(u) * float(np.exp(-0.5))))
    a = _sigmoid(a0_ref[...] + _dot(a_lo, aup_ref[...]))
    g = _dot(_sigmoid(g_lo), gup_ref[...])

    kk = k * kk_ref[...]
    ss = _head_sum(kk * kk, hg_ref, hs_ref)
    kk = kk * lax.rsqrt(jnp.maximum(ss, 1e-24))
    k = k * (1.0 + (a - 1.0) * ka_ref[...])
    coef = _head_sum(r * k * rk_ref[...], hg_ref, hs_ref)

    ro_ref[...] = r
    wo_ref[...] = decay
    ko_ref[...] = k
    vo_ref[...] = v
    kko_ref[...] = kk
    bo_ref[...] = kk * a
    go_ref[...] = g
    bon_ref[...] = coef * v


def _rwkvprep(proj, mu_p, w0, wup_p, a0, aup_p, gup, k_k, k_a, r_k, head_gather, head_spread):
    tm = 256
    pb = tm // 8
    cur = lambda c: pl.BlockSpec((tm, 1024), lambda i, c=c: (i, c))
    prev = lambda c: pl.BlockSpec((8, 1024), lambda i, c=c: (jnp.maximum(i * pb - 1, 0), c))
    full = lambda s: pl.BlockSpec(s, lambda i: (0,) * len(s))
    out = pl.BlockSpec((tm, 1024), lambda i: (i, 0))
    osh = jax.ShapeDtypeStruct((ROWS, RWKV_WIDTH), F32)
    return pl.pallas_call(
        functools.partial(_rwkvprep_kernel, tm=tm),
        grid=(ROWS // tm,),
        in_specs=[cur(COL_R // 1024), cur(COL_K // 1024), cur(COL_V // 1024),
                  pl.BlockSpec((tm, LORA_PACK), lambda i: (i, COL_LORA // LORA_PACK)),
                  prev(COL_R // 1024), prev(COL_K // 1024), prev(COL_V // 1024),
                  pl.BlockSpec((8, LORA_PACK), lambda i: (jnp.maximum(i * pb - 1, 0), COL_LORA // LORA_PACK)),
                  full((1, 3584)), full((1, 1024)), full((128, 1024)), full((1, 1024)), full((128, 1024)),
                  full((256, 1024)), full((1, 1024)), full((1, 1024)), full((1, 1024)),
                  full((RWKV_WIDTH, LANES)), full((LANES, RWKV_WIDTH))],
        out_specs=[out] * 8,
        out_shape=[osh] * 8,
        compiler_params=_params(("parallel",)),
        name="rwkvprep",
    )(proj, proj, proj, proj, proj, proj, proj, proj,
      mu_p, w0, wup_p, a0, aup_p, gup, k_k, k_a, r_k, head_gather, head_spread)


N_JPAIR = HEAD_DIM // 2
N_JOPS = 5


def _scan_kernel(kkn_ref, w_ref, b_ref, k_ref, r_ref, v_ref, y_ref, s_ref, sa_ref, d_ref, *, tc):
    @pl.when(pl.program_id(0) == 0)
    def _():
        s_ref[...] = jnp.zeros_like(s_ref)
        sa_ref[...] = jnp.zeros_like(sa_ref)

    lo_half = lax.broadcasted_iota(jnp.int32, (N_JPAIR, tc, LANES), 2) < 64
    for n, ref in enumerate((kkn_ref, w_ref, b_ref, k_ref, r_ref)):
        x = ref[...]
        xr = pltpu.roll(x, 64, axis=2)
        d_ref[n, 0:N_JPAIR] = jnp.where(lo_half, x, xr)
        d_ref[n, N_JPAIR:HEAD_DIM] = jnp.where(lo_half, xr, x)

    def tree(parts):
        while len(parts) > 1:
            parts = [parts[n] + parts[n + 1] for n in range(0, len(parts), 2)]
        return parts[0]

    def step(t, sa):
        vt = v_ref[t]
        acc_y = [None] * 4
        acc_s = [None] * 4
        for j in range(HEAD_DIM):
            row = lambda n: d_ref[n, j, pl.ds(t, 1), :]
            sn = s_ref[j] * row(1) - sa * row(2) + vt * row(3)
            s_ref[j] = sn
            ty = sn * row(4)
            ts = sn * row(0)
            acc_y[j % 4] = ty if acc_y[j % 4] is None else acc_y[j % 4] + ty
            acc_s[j % 4] = ts if acc_s[j % 4] is None else acc_s[j % 4] + ts
        y_ref[t] = tree(acc_y)
        return tree(acc_s)

    sa_ref[...] = lax.fori_loop(0, tc, step, sa_ref[...])


def _scan(kkn, w, b, k, r, v):
    tc = 64
    spec = pl.BlockSpec((tc, N_JPAIR, LANES), lambda i: (i, 0, 0))
    jspec = pl.BlockSpec((N_JPAIR, tc, LANES), lambda i: (0, i, 0))
    return pl.pallas_call(
        functools.partial(_scan_kernel, tc=tc),
        grid=(SEQ // tc,),
        in_specs=[jspec] * N_JOPS + [spec],
        out_specs=spec,
        out_shape=jax.ShapeDtypeStruct((SEQ, N_JPAIR, LANES), F32),
        scratch_shapes=[pltpu.VMEM((HEAD_DIM, N_JPAIR, LANES), F32),
                        pltpu.VMEM((N_JPAIR, LANES), F32),
                        pltpu.VMEM((N_JOPS, HEAD_DIM, tc, LANES), F32)],
        compiler_params=_params(("arbitrary",)),
        name="scan",
    )(kkn, w, b, k, r, v)


RELAYOUT_T = 128
_HALF_BATCH = [(half, b) for half in range(2) for b in range(BATCH)]


def _gather_lanes(z_ref, p):
    return jnp.concatenate([z_ref[b, pl.ds(half * N_JPAIR + p, RWKV_HEADS, stride=HEAD_DIM), :]
                            for half, b in _HALF_BATCH], axis=0)


def _relayout_kernel(x_ref, *rest, time_major, shift):
    if shift:
        nxt_ref, o_ref, z_ref = rest
        has_next = jnp.where(pl.program_id(0) == pl.num_programs(0) - 1, 0.0, 1.0).astype(F32)
        last_row = lax.broadcasted_iota(jnp.int32, (RELAYOUT_T, RWKV_WIDTH), 0) == RELAYOUT_T - 1
    else:
        o_ref, z_ref = rest
    for b in range(BATCH):
        x = x_ref[b]
        if shift:
            x = jnp.where(last_row, nxt_ref[b, 0:1, :] * has_next, pltpu.roll(x, RELAYOUT_T - 1, axis=0))
        z_ref[b] = x.T
    for p in range(N_JPAIR):
        tile = _gather_lanes(z_ref, p).T
        if time_major:
            o_ref[pl.ds(p, RELAYOUT_T, stride=N_JPAIR), :] = tile
        else:
            o_ref[p] = tile


def _relayout_back_kernel(y_ref, o_ref, z_ref):
    for p in range(N_JPAIR):
        tile = y_ref[pl.ds(p, RELAYOUT_T, stride=N_JPAIR), :].T
        for n, (half, b) in enumerate(_HALF_BATCH):
            z_ref[b, pl.ds(half * N_JPAIR + p, RWKV_HEADS, stride=HEAD_DIM), :] = (
                tile[n * RWKV_HEADS:(n + 1) * RWKV_HEADS])
    for b in range(BATCH):
        o_ref[b] = z_ref[b].T


_NATURAL_SPEC = pl.BlockSpec((BATCH, RELAYOUT_T, RWKV_WIDTH), lambda i: (0, i, 0))
_TIME_MAJOR_SPEC = pl.BlockSpec((RELAYOUT_T * N_JPAIR, LANES), lambda i: (i, 0))
_RELAYOUT_SCRATCH = [pltpu.VMEM((BATCH, RWKV_WIDTH, RELAYOUT_T), F32)]


def _relayout(a, time_major=False, shift=False):
    a = a.reshape(BATCH, SEQ, RWKV_WIDTH)
    blocks_per_step = RELAYOUT_T // 8
    nxt_spec = pl.BlockSpec((BATCH, 8, RWKV_WIDTH),
                            lambda i: (0, jnp.minimum((i + 1) * blocks_per_step, SEQ // 8 - 1), 0))
    if time_major:
        out_spec, out_shape = _TIME_MAJOR_SPEC, (SEQ * N_JPAIR, LANES)
    else:
        out_spec, out_shape = pl.BlockSpec((N_JPAIR, RELAYOUT_T, LANES), lambda i: (0, i, 0)), (N_JPAIR, SEQ, LANES)
    out = pl.pallas_call(
        functools.partial(_relayout_kernel, time_major=time_major, shift=shift),
        grid=(SEQ // RELAYOUT_T,),
        in_specs=[_NATURAL_SPEC, nxt_spec] if shift else [_NATURAL_SPEC],
        out_specs=out_spec,
        out_shape=jax.ShapeDtypeStruct(out_shape, F32),
        scratch_shapes=_RELAYOUT_SCRATCH,
        compiler_params=_params(("parallel",)),
        name="relayout",
    )(*((a, a) if shift else (a,)))
    return out.reshape(SEQ, N_JPAIR, LANES) if time_major else out


def _relayout_back(y):
    return pl.pallas_call(
        _relayout_back_kernel,
        grid=(SEQ // RELAYOUT_T,),
        in_specs=[_TIME_MAJOR_SPEC],
        out_specs=_NATURAL_SPEC,
        out_shape=jax.ShapeDtypeStruct((BATCH, SEQ, RWKV_WIDTH), F32),
        scratch_shapes=_RELAYOUT_SCRATCH,
        compiler_params=_params(("parallel",)),
        name="relayout_back",
    )(y.reshape(SEQ * N_JPAIR, LANES)).reshape(ROWS, RWKV_WIDTH)


def _rwkvpost_kernel(y_ref, bon_ref, g_ref, lw_ref, lb_ref, hg_ref, hs_ref, o_ref):
    y = y_ref[...]
    mean = _head_sum(y, hg_ref, hs_ref) * (1.0 / HEAD_DIM)
    d = y - mean
    var = _head_sum(d * d, hg_ref, hs_ref) * (1.0 / HEAD_DIM)
    yn = d * lax.rsqrt(var + GN_EPS) * lw_ref[...] + lb_ref[...]
    o_ref[...] = (yn + bon_ref[...]) * g_ref[...]


def _rwkvpost(y, bonus, g, lnx_w, lnx_b, head_gather, head_spread):
    tm = 512
    blk = pl.BlockSpec((tm, 1024), lambda i: (i, 0))
    vec = pl.BlockSpec((1, 1024), lambda i: (0, 0))
    return pl.pallas_call(
        _rwkvpost_kernel,
        grid=(ROWS // tm,),
        in_specs=[blk, blk, blk, vec, vec, pl.BlockSpec((RWKV_WIDTH, LANES), lambda i: (0, 0)),
                  pl.BlockSpec((LANES, RWKV_WIDTH), lambda i: (0, 0))],
        out_specs=blk,
        out_shape=jax.ShapeDtypeStruct((ROWS, RWKV_WIDTH), F32),
        compiler_params=_params(("parallel",)),
        name="rwkvpost",
    )(y, bonus, g, lnx_w, lnx_b, head_gather, head_spread)


KV_T = 512
KV_PACK = 2 * NSA_KV_WIDTH


def _kvprep_kernel(c_ref, s_ref, w_ref, craw_ref, ks_ref, vst_ref, kw_ref, vwt_ref):
    c = c_ref[...]
    for g in range(NSA_KV_HEADS):
        craw_ref[0, 0, g] = c[:, g * HEAD_DIM:(g + 1) * HEAD_DIM]
        craw_ref[1, 0, g] = c[:, NSA_KV_WIDTH + g * HEAD_DIM:NSA_KV_WIDTH + (g + 1) * HEAD_DIM]
    for src_ref, k_out, vt_out in ((s_ref, ks_ref, vst_ref), (w_ref, kw_ref, vwt_ref)):
        x = src_ref[...]
        vt = x[:, NSA_KV_WIDTH:].T
        for g in range(NSA_KV_HEADS):
            k_out[0, g] = x[:, g * HEAD_DIM:(g + 1) * HEAD_DIM].astype(BF16)
            vt_out[0, g] = vt[g * HEAD_DIM:(g + 1) * HEAD_DIM].astype(BF16)


def _kvprep(proj):
    nt = SEQ // KV_T
    G = NSA_KV_HEADS
    src = lambda col: pl.BlockSpec((KV_T, KV_PACK), lambda b, i, col=col: (b * nt + i, col // KV_PACK))
    natural = pl.BlockSpec((1, G, KV_T, HEAD_DIM), lambda b, i: (b, 0, i, 0))
    transposed = pl.BlockSpec((1, G, HEAD_DIM, KV_T), lambda b, i: (b, 0, 0, i))
    nat_shape = jax.ShapeDtypeStruct((BATCH, G, SEQ, HEAD_DIM), BF16)
    tr_shape = jax.ShapeDtypeStruct((BATCH, G, HEAD_DIM, SEQ), BF16)
    return pl.pallas_call(
        _kvprep_kernel,
        grid=(BATCH, nt),
        in_specs=[src(COL_KC), src(COL_KS), src(COL_KW)],
        out_specs=[pl.BlockSpec((2, 1, G, KV_T, HEAD_DIM), lambda b, i: (0, b, 0, i, 0)),
                   natural, transposed, natural, transposed],
        out_shape=[jax.ShapeDtypeStruct((2, BATCH, G, SEQ, HEAD_DIM), F32), nat_shape, tr_shape, nat_shape, tr_shape],
        compiler_params=_params(("parallel", "parallel")),
        name="kvprep",
    )(proj, proj, proj)


def _compress_kernel(kv_ref, w1_ref, pe_ref, w2_ref, o_ref):
    w1 = w1_ref[0]
    top = bot = None
    for l in range(CMP_STRIDE):
        rows = kv_ref[0, 0, 0, pl.ds(l, N_CMP_PAD, stride=CMP_STRIDE), :]
        t_l = _dot(rows, w1[l * HEAD_DIM:(l + 1) * HEAD_DIM])
        b_l = _dot(rows, w1[(CMP_STRIDE + l) * HEAD_DIM:(CMP_STRIDE + l + 1) * HEAD_DIM])
        top = t_l if top is None else top + t_l
        bot = b_l if bot is None else bot + b_l
    bias = _dot(jnp.broadcast_to(pe_ref[0], (8, CMP_BLOCK * HEAD_DIM)), w1)[0:1]
    hid = top + pltpu.roll(bot, N_CMP_PAD - 1, axis=0) + bias
    c0 = float(np.sqrt(2.0 / np.pi))
    act = 0.5 * hid * (1.0 + jnp.tanh(c0 * (hid + 0.044715 * (hid * hid * hid))))
    o_ref[0, 0] = _dot(act, w2_ref[0])


def _compress(craw, w1, pe, w2):
    G = NSA_KV_HEADS
    nbg = BATCH * G
    return pl.pallas_call(
        _compress_kernel,
        grid=(2, nbg),
        in_specs=[pl.BlockSpec((1, 1, 1, SEQ, HEAD_DIM), lambda c, n: (c, n // G, n % G, 0, 0)),
                  pl.BlockSpec((1, CMP_BLOCK * HEAD_DIM, CMP_HIDDEN), lambda c, n: (c, 0, 0)),
                  pl.BlockSpec((1, 1, CMP_BLOCK * HEAD_DIM), lambda c, n: (c, 0, 0)),
                  pl.BlockSpec((1, CMP_HIDDEN, HEAD_DIM), lambda c, n: (c, 0, 0))],
        out_specs=pl.BlockSpec((1, 1, N_CMP_PAD, HEAD_DIM), lambda c, n: (c, n, 0, 0)),
        out_shape=jax.ShapeDtypeStruct((2, nbg, N_CMP_PAD, HEAD_DIM), F32),
        compiler_params=_params(("parallel", "parallel")),
        name="compress",
    )(craw, w1, pe, w2)


MASK_BIG = float(2.0 ** 100)
KEY_CHUNK = 512
N_FEAT = 64


def _key_features():
    s = np.arange(SEQ)
    f = np.zeros((SEQ, N_FEAT), np.float32)
    f[s, s // SEL_BLOCK] = 1.0
    f[:, 32:35] = (SEL_BLOCK * (s // SEL_BLOCK))[:, None]
    f[:, 35:38] = (s % SEL_BLOCK)[:, None]
    return f


def _split3(x):
    a1 = x.astype(BF16)
    r1 = x - a1.astype(F32)
    a2 = r1.astype(BF16)
    a3 = (r1 - a2.astype(F32)).astype(BF16)
    return a1, a2, a3


def _nsa_kernel(q_ref, kc_ref, vc_ref, ks_ref, vst_ref, kw_ref, vwt_ref, gate_ref, ovt_ref, feat_ref, o_ref,
                gt_ref, slc_ref, *, tq):
    g = pl.program_id(1)
    t0 = pl.program_id(2) * tq
    hg = NSA_GROUP
    nq = hg * tq
    dot = functools.partial(jnp.dot, preferred_element_type=F32)

    qt = (q_ref[...] * HEAD_DIM ** -0.5).T
    qt = jnp.concatenate([qt[h * HEAD_DIM:(h + 1) * HEAD_DIM] for h in range(hg)], axis=1).astype(BF16)
    lane = lax.broadcasted_iota(jnp.int32, (1, nq), 1)
    slope = jnp.exp2(-0.5 * (g * hg + lane // tq + 1).astype(F32))
    qpos = t0 + lane % tq

    n_idx = lax.broadcasted_iota(jnp.int32, (N_CMP_PAD, 1), 0)
    dist_c = qpos.astype(F32) - (n_idx * CMP_STRIDE + (CMP_BLOCK - 1)).astype(F32)
    mask_c = (dist_c >= 0) & (n_idx < N_CMP)
    x_c = jnp.where(mask_c, dot(kc_ref[0, 0].astype(BF16), qt) - slope * dist_c, NEG_INF)
    p_c = jnp.where(mask_c, jnp.exp(x_c - jnp.max(x_c, axis=0, keepdims=True)), 0.0)
    den_c = jnp.maximum(jnp.sum(p_c, axis=0, keepdims=True), TINY)
    o_cmp = dot(vc_ref[0, 0].T.astype(BF16), p_c.astype(BF16)) / den_c

    p_n = p_c / den_c
    p_sum = (p_n[:, 0:tq] + p_n[:, tq:2 * tq]) + (p_n[:, 2 * tq:3 * tq] + p_n[:, 3 * tq:4 * tq])
    ovt = ovt_ref[...]
    imp = sum(dot(ovt, part) for part in _split3(p_sum))
    blk = lax.broadcasted_iota(jnp.int32, (N_SEL, tq), 0)
    cur = (t0 + lax.broadcasted_iota(jnp.int32, (N_SEL, tq), 1)) // SEL_BLOCK
    forced = (blk == 0) | (blk == cur) | (blk == cur - 1)
    score = jnp.where(blk > cur, NEG_INF, jnp.where(forced, -NEG_INF, imp))
    rank = jnp.zeros((N_SEL, tq), F32)
    for jp in range(N_SEL):
        row = score[jp:jp + 1, :]
        beats = (row > score) | ((row == score) & (blk > jp))
        rank = rank + jnp.where(beats, 1.0, 0.0)
    sel_bias = jnp.where((rank < SEL_TOPN) & (blk <= cur), 0.0, -MASK_BIG)
    sel_bias = jnp.concatenate([sel_bias] * hg, axis=1)

    s1, s2, s3 = (piece.astype(F32) for piece in _split3(slope))
    r8 = lax.broadcasted_iota(jnp.int32, (8, nq), 0)
    srow = jnp.where((r8 == 0) | (r8 == 3), s1, jnp.where((r8 == 1) | (r8 == 4), s2, s3))
    srow = jnp.where(r8 < 6, srow, 0.0)
    pad = jnp.zeros((N_FEAT - N_SEL - 8, nq), F32)
    q_slc = jnp.concatenate([qt, jnp.concatenate([sel_bias, srow, pad], axis=0).astype(BF16)], axis=0)
    q_win = jnp.concatenate([qt, jnp.concatenate([jnp.zeros_like(sel_bias), srow, pad], axis=0).astype(BF16)],
                            axis=0)

    def scores(k_ref, start, size, q_aug):
        keys = jnp.concatenate([k_ref[0, pl.ds(start, size), :], feat_ref[pl.ds(start, size), :]], axis=1)
        return dot(keys, q_aug)

    def attend(s, v_t):
        p = jnp.exp(s - jnp.max(s, axis=0, keepdims=True))
        return dot(v_t, p.astype(BF16)) / jnp.maximum(jnp.sum(p, axis=0, keepdims=True), TINY)

    def slc_variant(n_chunks):
        size = n_chunks * KEY_CHUNK
        s = scores(ks_ref, 0, size, q_slc)
        kpos = (size - KEY_CHUNK) + lax.broadcasted_iota(jnp.int32, (KEY_CHUNK, 1), 0)
        last = jnp.where(kpos <= qpos, s[size - KEY_CHUNK:], -MASK_BIG)
        s = last if n_chunks == 1 else jnp.concatenate([s[:size - KEY_CHUNK], last], axis=0)
        slc_ref[...] = attend(s, vst_ref[0, :, 0:size])

    for n_chunks in range(1, SEQ // KEY_CHUNK + 1):
        pl.when(t0 // KEY_CHUNK == n_chunks - 1)(functools.partial(slc_variant, n_chunks))
    o_slc = slc_ref[...]

    span = WINDOW + tq
    start = pl.multiple_of(jnp.maximum(t0 - WINDOW, 0), LANES)
    dist_w = qpos - (start + lax.broadcasted_iota(jnp.int32, (span, 1), 0))
    s_w = jnp.where((dist_w >= 0) & (dist_w < WINDOW), scores(kw_ref, start, span, q_win), -MASK_BIG)
    o_win = attend(s_w, vwt_ref[0, :, pl.ds(start, span)])

    gt_ref[...] = _sigmoid(gate_ref[...]).T
    out = jnp.zeros((HEAD_DIM, nq), F32)
    for c, o in enumerate((o_cmp, o_slc, o_win)):
        g8 = gt_ref[pl.ds(pl.multiple_of(c * NSA_HEADS + (g // 2) * 8, 8), 8), :]
        g4 = jnp.where(g % 2 == 0, g8[0:hg], g8[hg:2 * hg])
        out = out + jnp.concatenate([g4[h:h + 1, :] for h in range(hg)], axis=1) * o
    o_ref[...] = jnp.concatenate([out[:, h * tq:(h + 1) * tq].T for h in range(hg)], axis=1)


def _nsa(proj, kcvc, ks, vst, kw, vwt, overlap_t, feat):
    tq = 256
    nq = SEQ // tq
    G = NSA_KV_HEADS
    kv_n = pl.BlockSpec((1, SEQ, HEAD_DIM), lambda b, g, i: (b * G + g, 0, 0))
    kv_t = pl.BlockSpec((1, HEAD_DIM, SEQ), lambda b, g, i: (b * G + g, 0, 0))
    cmp_spec = lambda c: pl.BlockSpec((1, 1, N_CMP_PAD, HEAD_DIM), lambda b, g, i, c=c: (c, b * G + g, 0, 0))
    return pl.pallas_call(
        functools.partial(_nsa_kernel, tq=tq),
        grid=(BATCH, G, nq),
        in_specs=[pl.BlockSpec((tq, NSA_GROUP * HEAD_DIM), lambda b, g, i: (b * nq + i, COL_Q // 256 + g)),
                  cmp_spec(0), cmp_spec(1), kv_n, kv_t, kv_n, kv_t,
                  pl.BlockSpec((tq, LANES), lambda b, g, i: (b * nq + i, COL_NG // LANES)),
                  pl.BlockSpec((N_SEL, N_CMP_PAD), lambda b, g, i: (0, 0)),
                  pl.BlockSpec((SEQ, N_FEAT), lambda b, g, i: (0, 0))],
        out_specs=pl.BlockSpec((tq, NSA_GROUP * HEAD_DIM), lambda b, g, i: (b * nq + i, g)),
        out_shape=jax.ShapeDtypeStruct((ROWS, NSA_Q_WIDTH), F32),
        scratch_shapes=[pltpu.VMEM((LANES, tq), F32), pltpu.VMEM((HEAD_DIM, NSA_GROUP * tq), F32)],
        compiler_params=_params(("parallel", "parallel", "parallel")),
        name="nsa",
    )(proj, kcvc, kcvc, ks, vst, kw, vwt, proj, overlap_t, feat)


def _mix_kernel(x_ref, ya_ref, yb_ref, ga_ref, gb_ref, wa_ref, wb_ref, wo_ref, h_ref):
    ma = _dot(ya_ref[...], wa_ref[...])
    mb = _dot(yb_ref[...], wb_ref[...])
    mixed = _sigmoid(ga_ref[...]) * ma + _sigmoid(gb_ref[...]) * mb
    h_ref[...] = x_ref[...] + _dot(mixed, wo_ref[...])


def _mix(x2, ya, yb, proj, wa, wb, wo):
    tm = 256
    row = lambda w, c=0: pl.BlockSpec((tm, w), lambda i, c=c: (i, c))
    const = lambda s: pl.BlockSpec(s, lambda i: (0, 0), pipeline_mode=pl.Buffered(1))
    return pl.pallas_call(
        _mix_kernel,
        grid=(ROWS // tm,),
        in_specs=[row(D_MODEL), row(1024), row(1024), row(D_MODEL, COL_GA // D_MODEL), row(D_MODEL, COL_GB // D_MODEL),
                  const((1024, D_MODEL)), const((1024, D_MODEL)), const((D_MODEL, D_MODEL))],
        out_specs=row(D_MODEL),
        out_shape=jax.ShapeDtypeStruct((ROWS, D_MODEL), F32),
        compiler_params=_params(("parallel",)),
        name="mix",
    )(x2, ya, yb, proj, proj, wa, wb, wo)


def _mlp_kernel(h_ref, gm_ref, wu_ref, wd_ref, gf_ref, o_ref, hn_ref, acc_ref):
    f = pl.program_id(1)

    @pl.when(f == 0)
    def _():
        h = h_ref[...]
        ms = jnp.mean(h * h, axis=-1, keepdims=True)
        hn_ref[...] = (h * lax.rsqrt(ms + NORM_EPS) * gm_ref[...]).astype(BF16)
        acc_ref[...] = jnp.zeros_like(acc_ref)

    u = jnp.maximum(jnp.dot(hn_ref[...], wu_ref[...], preferred_element_type=F32), 0.0)
    acc_ref[...] += jnp.dot((u * u).astype(BF16), wd_ref[...], preferred_element_type=F32)

    @pl.when(f == pl.num_programs(1) - 1)
    def _():
        h2 = h_ref[...] + acc_ref[...]
        ms = jnp.mean(h2 * h2, axis=-1, keepdims=True)
        o_ref[...] = h2 * lax.rsqrt(ms + NORM_EPS) * gf_ref[...]


def _mlp(h, g_mlp, w_up, w_down, g_final):
    tm, tf = 512, 1024
    return pl.pallas_call(
        _mlp_kernel,
        grid=(ROWS // tm, D_FF // tf),
        in_specs=[pl.BlockSpec((tm, D_MODEL), lambda i, f: (i, 0)),
                  pl.BlockSpec((1, D_MODEL), lambda i, f: (0, 0)),
                  pl.BlockSpec((D_MODEL, tf), lambda i, f: (0, f)),
                  pl.BlockSpec((tf, D_MODEL), lambda i, f: (f, 0)),
                  pl.BlockSpec((1, D_MODEL), lambda i, f: (0, 0))],
        out_specs=pl.BlockSpec((tm, D_MODEL), lambda i, f: (i, 0)),
        out_shape=jax.ShapeDtypeStruct((ROWS, D_MODEL), F32),
        scratch_shapes=[pltpu.VMEM((tm, D_MODEL), BF16), pltpu.VMEM((tm, D_MODEL), F32)],
        compiler_params=_params(("parallel", "arbitrary")),
        name="mlp",
    )(h, g_mlp, w_up, w_down, g_final)


def _pack_cols(w, pad):
    return jnp.concatenate([
        w[..., 6128:10224],
        w[..., 0:3168], pad(32),
        w[..., 3168:3264], pad(32),
        w[..., 3264:3520],
        w[..., 3520:6080],
        w[..., 6080:6128], pad(464),
    ], axis=-1)


def _overlap_matrix():
    cs = np.arange(N_CMP)[:, None] * CMP_STRIDE
    ss = np.arange(N_SEL)[None, :] * SEL_BLOCK
    ov = np.clip(np.minimum(cs + CMP_BLOCK, ss + SEL_BLOCK) - np.maximum(cs, ss), 0, None) / CMP_BLOCK
    out = np.zeros((N_CMP_PAD, N_SEL), np.float32)
    out[:N_CMP] = ov
    return out


def kernel(x, norm_mix, w_in, rwkv_mu, rwkv_w0, rwkv_w_up, rwkv_a0, rwkv_a_up, rwkv_g_up, rwkv_k_k, rwkv_k_a,
           rwkv_r_k, rwkv_lnx_w, rwkv_lnx_b, cmp_pe_k, cmp_w1_k, cmp_w2_k, cmp_pe_v, cmp_w1_v, cmp_w2_v,
           w_out_rwkv, w_out_nsa, w_o, norm_mlp, mlp_w_up, mlp_w_down, norm_final):
    assert x.shape == (BATCH, SEQ, D_MODEL) and w_in.shape[0] == 1
    l = 0
    G = NSA_KV_HEADS
    x2 = x.reshape(ROWS, D_MODEL)
    row = lambda a: a.reshape(1, -1)

    w_p = _pack_weight(w_in[l])
    proj = _inproj(x2, row(norm_mix[l]), w_p)

    mu = rwkv_mu[l]
    mu_p = jnp.concatenate([mu[0:3168], jnp.zeros((32,), F32), mu[3168:3264], jnp.zeros((32,), F32), mu[3264:3520]])
    zrows = jnp.zeros((32, RWKV_WIDTH), F32)
    head_of_lane = np.arange(RWKV_WIDTH)[:, None] // HEAD_DIM == np.arange(LANES)[None, :]
    head_gather = jnp.asarray(head_of_lane, BF16)
    head_spread = jnp.asarray(head_of_lane.T, BF16)
    r, w, k, v, kk, b, g, bonus = _rwkvprep(
        proj, row(mu_p), row(rwkv_w0[l]), jnp.concatenate([rwkv_w_up[l], zrows]),
        row(rwkv_a0[l]), jnp.concatenate([rwkv_a_up[l], zrows]), rwkv_g_up[l],
        row(rwkv_k_k[l]), row(rwkv_k_a[l]), row(rwkv_r_k[l]), head_gather, head_spread)
    y = _scan(_relayout(kk, shift=True), _relayout(w), _relayout(b), _relayout(k), _relayout(r), _relayout(v, time_major=True))
    ya = _rwkvpost(_relayout_back(y), bonus, g, row(rwkv_lnx_w[l]), row(rwkv_lnx_b[l]), head_gather, head_spread)

    craw, ks, vst, kw, vwt = _kvprep(proj)
    kcvc = _compress(craw, jnp.stack([cmp_w1_k[l], cmp_w1_v[l]]),
                     jnp.stack([cmp_pe_k[l].reshape(1, -1), cmp_pe_v[l].reshape(1, -1)]),
                     jnp.stack([cmp_w2_k[l], cmp_w2_v[l]]))
    per_group = lambda a: a.reshape((BATCH * G,) + a.shape[2:])
    yb = _nsa(proj, kcvc, per_group(ks), per_group(vst), per_group(kw), per_group(vwt),
              jnp.asarray(_overlap_matrix().T, BF16), jnp.asarray(_key_features(), BF16))

    h = _mix(x2, ya, yb, proj, w_out_rwkv[l].astype(BF16), w_out_nsa[l].astype(BF16), w_o[l].astype(BF16))
    out = _mlp(h, row(norm_mlp[l]), mlp_w_up[l].astype(BF16), mlp_w_down[l].astype(BF16), row(norm_final))
    return out.reshape(BATCH, SEQ, D_MODEL)
```

```python
import functools

import numpy as np
import jax
import jax.numpy as jnp
from jax import lax
from jax.experimental import pallas as pl
from jax.experimental.pallas import tpu as pltpu

F32 = jnp.float32
BF16 = jnp.bfloat16

D_MODEL = 2048
BATCH = 4
SEQ = 2048
ROWS = BATCH * SEQ
RWKV_HEADS = 16
HEAD_DIM = 64
RWKV_WIDTH = RWKV_HEADS * HEAD_DIM
LORA_DECAY = 96
LORA_ICLR = 96
LORA_GATE = 256
GN_EPS = 64e-5
NSA_HEADS = 16
NSA_KV_HEADS = 4
NSA_GROUP = NSA_HEADS // NSA_KV_HEADS
NSA_Q_WIDTH = NSA_HEADS * HEAD_DIM
NSA_KV_WIDTH = NSA_KV_HEADS * HEAD_DIM
CMP_BLOCK = 32
CMP_STRIDE = 16
CMP_HIDDEN = 256
N_CMP = (SEQ - CMP_BLOCK) // CMP_STRIDE + 1
N_CMP_PAD = 128
SEL_BLOCK = 64
SEL_TOPN = 16
N_SEL = SEQ // SEL_BLOCK
WINDOW = 512
D_FF = 4 * D_MODEL
NORM_EPS = 1e-5
NEG_INF = -1e30
TINY = 1e-30

LANES = 128
VMEM_LIMIT = 56 * 1024 * 1024

COL_GA = 0
COL_GB = 2048
COL_R = 4096
COL_K = 5120
COL_V = 6144
COL_LORA = 7168
COL_Q = 7680
COL_KC = 8704
COL_VC = 8960
COL_KS = 9216
COL_VS = 9472
COL_KW = 9728
COL_VW = 9984
COL_NG = 10240
PROJ_COLS = 10752
LORA_PACK = 512


def _dot(a, b):
    return jnp.dot(a.astype(BF16), b.astype(BF16), preferred_element_type=F32)


def _sigmoid(x):
    return 1.0 / (1.0 + jnp.exp(-x))


def _params(sem, limit=VMEM_LIMIT):
    return pltpu.CompilerParams(dimension_semantics=sem, vmem_limit_bytes=limit)


def _inproj_kernel(x_ref, g_ref, w_ref, o_ref, xn_ref):
    @pl.when(pl.program_id(1) == 0)
    def _():
        x = x_ref[...]
        ms = jnp.mean(x * x, axis=-1, keepdims=True)
        xn_ref[...] = (x * lax.rsqrt(ms + NORM_EPS) * g_ref[...]).astype(BF16)

    o_ref[...] = jnp.dot(xn_ref[...], w_ref[...], preferred_element_type=F32)


def _pack_kernel(wt_ref, o_ref):
    wt = wt_ref[...]
    packed = _pack_rows(wt, lambda n: jnp.zeros((n, wt.shape[1]), F32))
    o_ref[...] = packed.T.astype(BF16)


def _pack_weight(w_t):
    tk = 128
    return pl.pallas_call(
        _pack_kernel,
        grid=(D_MODEL // tk,),
        in_specs=[pl.BlockSpec((w_t.shape[0], tk), lambda i: (0, i))],
        out_specs=pl.BlockSpec((tk, PROJ_COLS), lambda i: (i, 0)),
        out_shape=jax.ShapeDtypeStruct((D_MODEL, PROJ_COLS), BF16),
        compiler_params=_params(("parallel",)),
        name="packw",
    )(w_t)


def _inproj(x2, g, w_p):
    tm, tn = 1024, 1536
    return pl.pallas_call(
        _inproj_kernel,
        grid=(ROWS // tm, PROJ_COLS // tn),
        in_specs=[pl.BlockSpec((tm, D_MODEL), lambda i, j: (i, 0)),
                  pl.BlockSpec((1, D_MODEL), lambda i, j: (0, 0)),
                  pl.BlockSpec((D_MODEL, tn), lambda i, j: (0, j))],
        out_specs=pl.BlockSpec((tm, tn), lambda i, j: (i, j)),
        out_shape=jax.ShapeDtypeStruct((ROWS, PROJ_COLS), F32),
        scratch_shapes=[pltpu.VMEM((tm, D_MODEL), BF16)],
        compiler_params=_params(("parallel", "arbitrary")),
        name="inproj",
    )(x2, g, w_p)


def _dot_hi_lo(x, m):
    hi = x.astype(BF16)
    lo = (x - hi.astype(F32)).astype(BF16)
    return jnp.dot(hi, m, preferred_element_type=F32) + jnp.dot(lo, m, preferred_element_type=F32)


def _head_sum(x, gather_ref, spread_ref):
    return _dot_hi_lo(_dot_hi_lo(x, gather_ref[...]), spread_ref[...])


def _rwkvprep_kernel(r_ref, k_ref, v_ref, lo_ref, rp_ref, kp_ref, vp_ref, lop_ref,
                     mu_ref, w0_ref, wup_ref, a0_ref, aup_ref, gup_ref, kk_ref, ka_ref, rk_ref, hg_ref, hs_ref,
                     ro_ref, wo_ref, ko_ref, vo_ref, kko_ref, bo_ref, go_ref, bon_ref, *, tm):
    i = pl.program_id(0)
    has_prev = jnp.where((i * tm) % SEQ == 0, 0.0, 1.0).astype(F32)

    def shift(cur_ref, prev_ref, mu):
        z = cur_ref[...]
        zp = pltpu.roll(z, 1, axis=0)
        prev_row = prev_ref[7:8, :] * has_prev
        row = lax.broadcasted_iota(jnp.int32, z.shape, 0)
        zp = jnp.where(row == 0, prev_row, zp)
        return z + (zp - z) * mu

    r = shift(r_ref, rp_ref, mu_ref[:, 0:1024])
    k = shift(k_ref, kp_ref, mu_ref[:, 1024:2048])
    v = shift(v_ref, vp_ref, mu_ref[:, 2048:3072])
    lo = shift(lo_ref, lop_ref, mu_ref[:, 3072:3584])
    w_lo, a_lo, g_lo = lo[:, 0:128], lo[:, 128:256], lo[:, 256:512]

    u = w0_ref[...] + _dot(jnp.tanh(w_lo), wup_ref[...])
    decay = jnp.exp(-(_sigmoid(u) * float(np.exp(-0.5))))
    a = _sigmoid(a0_ref[...] + _dot(a_lo, aup_ref[...]))
    g = _dot(_sigmoid(g_lo), gup_ref[...])

    kk = k * kk_ref[...]
    ss = _head_sum(kk * kk, hg_ref, hs_ref)
    kk = kk * lax.rsqrt(jnp.maximum(ss, 1e-24))
    k = k * (1.0 + (a - 1.0) * ka_ref[...])
    coef = _head_sum(r * k * rk_ref[...], hg_ref, hs_ref)

    ro_ref[...] = r
    wo_ref[...] = decay
    ko_ref[...] = k
    vo_ref[...] = v
    kko_ref[...] = kk
    bo_ref[...] = kk * a
    go_ref[...] = g
    bon_ref[...] = coef * v


def _rwkvprep(proj, mu_p, w0, wup_p, a0, aup_p, gup, k_k, k_a, r_k, head_gather, head_spread):
    tm = 256
    pb = tm // 8
    cur = lambda c: pl.BlockSpec((tm, 1024), lambda i, c=c: (i, c))
    prev = lambda c: pl.BlockSpec((8, 1024), lambda i, c=c: (jnp.maximum(i * pb - 1, 0), c))
    full = lambda s: pl.BlockSpec(s, lambda i: (0,) * len(s))
    out = pl.BlockSpec((tm, 1024), lambda i: (i, 0))
    osh = jax.ShapeDtypeStruct((ROWS, RWKV_WIDTH), F32)
    return pl.pallas_call(
        functools.partial(_rwkvprep_kernel, tm=tm),
        grid=(ROWS // tm,),
        in_specs=[cur(COL_R // 1024), cur(COL_K // 1024), cur(COL_V // 1024),
                  pl.BlockSpec((tm, LORA_PACK), lambda i: (i, COL_LORA // LORA_PACK)),
                  prev(COL_R // 1024), prev(COL_K // 1024), prev(COL_V // 1024),
                  pl.BlockSpec((8, LORA_PACK), lambda i: (jnp.maximum(i * pb - 1, 0), COL_LORA // LORA_PACK)),
                  full((1, 3584)), full((1, 1024)), full((128, 1024)), full((1, 1024)), full((128, 1024)),
                  full((256, 1024)), full((1, 1024)), full((1, 1024)), full((1, 1024)),
                  full((RWKV_WIDTH, LANES)), full((LANES, RWKV_WIDTH))],
        out_specs=[out] * 8,
        out_shape=[osh] * 8,
        compiler_params=_params(("parallel",)),
        name="rwkvprep",
    )(proj, proj, proj, proj, proj, proj, proj, proj,
      mu_p, w0, wup_p, a0, aup_p, gup, k_k, k_a, r_k, head_gather, head_spread)


N_JPAIR = HEAD_DIM // 2
N_JOPS = 5


def _scan_kernel(kkn_ref, w_ref, b_ref, k_ref, r_ref, v_ref, y_ref, s_ref, sa_ref, d_ref, *, tc):
    @pl.when(pl.program_id(0) == 0)
    def _():
        s_ref[...] = jnp.zeros_like(s_ref)
        sa_ref[...] = jnp.zeros_like(sa_ref)

    lo_half = lax.broadcasted_iota(jnp.int32, (N_JPAIR, tc, LANES), 2) < 64
    for n, ref in enumerate((kkn_ref, w_ref, b_ref, k_ref, r_ref)):
        x = ref[...]
        xr = pltpu.roll(x, 64, axis=2)
        d_ref[n, 0:N_JPAIR] = jnp.where(lo_half, x, xr)
        d_ref[n, N_JPAIR:HEAD_DIM] = jnp.where(lo_half, xr, x)

    def tree(parts):
        while len(parts) > 1:
            parts = [parts[n] + parts[n + 1] for n in range(0, len(parts), 2)]
        return parts[0]

    def step(t, sa):
        vt = v_ref[t]
        acc_y = [None] * 4
        acc_s = [None] * 4
        for j in range(HEAD_DIM):
            row = lambda n: d_ref[n, j, pl.ds(t, 1), :]
            sn = s_ref[j] * row(1) - sa * row(2) + vt * row(3)
            s_ref[j] = sn
            ty = sn * row(4)
            ts = sn * row(0)
            acc_y[j % 4] = ty if acc_y[j % 4] is None else acc_y[j % 4] + ty
            acc_s[j % 4] = ts if acc_s[j % 4] is None else acc_s[j % 4] + ts
        y_ref[t] = tree(acc_y)
        return tree(acc_s)

    sa_ref[...] = lax.fori_loop(0, tc, step, sa_ref[...])


def _scan(kkn, w, b, k, r, v):
    tc = 64
    spec = pl.BlockSpec((tc, N_JPAIR, LANES), lambda i: (i, 0, 0))
    jspec = pl.BlockSpec((N_JPAIR, tc, LANES), lambda i: (0, i, 0))
    return pl.pallas_call(
        functools.partial(_scan_kernel, tc=tc),
        grid=(SEQ // tc,),
        in_specs=[jspec] * N_JOPS + [spec],
        out_specs=spec,
        out_shape=jax.ShapeDtypeStruct((SEQ, N_JPAIR, LANES), F32),
        scratch_shapes=[pltpu.VMEM((HEAD_DIM, N_JPAIR, LANES), F32),
                        pltpu.VMEM((N_JPAIR, LANES), F32),
                        pltpu.VMEM((N_JOPS, HEAD_DIM, tc, LANES), F32)],
        compiler_params=_params(("arbitrary",)),
        name="scan",
    )(kkn, w, b, k, r, v)


RELAYOUT_T = 128
_HALF_BATCH = [(half, b) for half in range(2) for b in range(BATCH)]


def _gather_lanes(z_ref, p):
    return jnp.concatenate([z_ref[b, pl.ds(half * N_JPAIR + p, RWKV_HEADS, stride=HEAD_DIM), :]
                            for half, b in _HALF_BATCH], axis=0)


def _relayout_kernel(x_ref, *rest, time_major, shift):
    if shift:
        nxt_ref, o_ref, z_ref = rest
        has_next = jnp.where(pl.program_id(0) == pl.num_programs(0) - 1, 0.0, 1.0).astype(F32)
        last_row = lax.broadcasted_iota(jnp.int32, (RELAYOUT_T, RWKV_WIDTH), 0) == RELAYOUT_T - 1
    else:
        o_ref, z_ref = rest
    for b in range(BATCH):
        x = x_ref[b]
        if shift:
            x = jnp.where(last_row, nxt_ref[b, 0:1, :] * has_next, pltpu.roll(x, RELAYOUT_T - 1, axis=0))
        z_ref[b] = x.T
    for p in range(N_JPAIR):
        tile = _gather_lanes(z_ref, p).T
        if time_major:
            o_ref[pl.ds(p, RELAYOUT_T, stride=N_JPAIR), :] = tile
        else:
            o_ref[p] = tile


def _relayout_back_kernel(y_ref, o_ref, z_ref):
    for p in range(N_JPAIR):
        tile = y_ref[pl.ds(p, RELAYOUT_T, stride=N_JPAIR), :].T
        for n, (half, b) in enumerate(_HALF_BATCH):
            z_ref[b, pl.ds(half * N_JPAIR + p, RWKV_HEADS, stride=HEAD_DIM), :] = (
                tile[n * RWKV_HEADS:(n + 1) * RWKV_HEADS])
    for b in range(BATCH):
        o_ref[b] = z_ref[b].T


_NATURAL_SPEC = pl.BlockSpec((BATCH, RELAYOUT_T, RWKV_WIDTH), lambda i: (0, i, 0))
_TIME_MAJOR_SPEC = pl.BlockSpec((RELAYOUT_T * N_JPAIR, LANES), lambda i: (i, 0))
_RELAYOUT_SCRATCH = [pltpu.VMEM((BATCH, RWKV_WIDTH, RELAYOUT_T), F32)]


def _relayout(a, time_major=False, shift=False):
    a = a.reshape(BATCH, SEQ, RWKV_WIDTH)
    blocks_per_step = RELAYOUT_T // 8
    nxt_spec = pl.BlockSpec((BATCH, 8, RWKV_WIDTH),
                            lambda i: (0, jnp.minimum((i + 1) * blocks_per_step, SEQ // 8 - 1), 0))
    if time_major:
        out_spec, out_shape = _TIME_MAJOR_SPEC, (SEQ * N_JPAIR, LANES)
    else:
        out_spec, out_shape = pl.BlockSpec((N_JPAIR, RELAYOUT_T, LANES), lambda i: (0, i, 0)), (N_JPAIR, SEQ, LANES)
    out = pl.pallas_call(
        functools.partial(_relayout_kernel, time_major=time_major, shift=shift),
        grid=(SEQ // RELAYOUT_T,),
        in_specs=[_NATURAL_SPEC, nxt_spec] if shift else [_NATURAL_SPEC],
        out_specs=out_spec,
        out_shape=jax.ShapeDtypeStruct(out_shape, F32),
        scratch_shapes=_RELAYOUT_SCRATCH,
        compiler_params=_params(("parallel",)),
        name="relayout",
    )(*((a, a) if shift else (a,)))
    return out.reshape(SEQ, N_JPAIR, LANES) if time_major else out


def _relayout_back(y):
    return pl.pallas_call(
        _relayout_back_kernel,
        grid=(SEQ // RELAYOUT_T,),
        in_specs=[_TIME_MAJOR_SPEC],
        out_specs=_NATURAL_SPEC,
        out_shape=jax.ShapeDtypeStruct((BATCH, SEQ, RWKV_WIDTH), F32),
        scratch_shapes=_RELAYOUT_SCRATCH,
        compiler_params=_params(("parallel",)),
        name="relayout_back",
    )(y.reshape(SEQ * N_JPAIR, LANES)).reshape(ROWS, RWKV_WIDTH)


def _rwkvpost_kernel(y_ref, bon_ref, g_ref, lw_ref, lb_ref, hg_ref, hs_ref, o_ref):
    y = y_ref[...]
    mean = _head_sum(y, hg_ref, hs_ref) * (1.0 / HEAD_DIM)
    d = y - mean
    var = _head_sum(d * d, hg_ref, hs_ref) * (1.0 / HEAD_DIM)
    yn = d * lax.rsqrt(var + GN_EPS) * lw_ref[...] + lb_ref[...]
    o_ref[...] = (yn + bon_ref[...]) * g_ref[...]


def _rwkvpost(y, bonus, g, lnx_w, lnx_b, head_gather, head_spread):
    tm = 512
    blk = pl.BlockSpec((tm, 1024), lambda i: (i, 0))
    vec = pl.BlockSpec((1, 1024), lambda i: (0, 0))
    return pl.pallas_call(
        _rwkvpost_kernel,
        grid=(ROWS // tm,),
        in_specs=[blk, blk, blk, vec, vec, pl.BlockSpec((RWKV_WIDTH, LANES), lambda i: (0, 0)),
                  pl.BlockSpec((LANES, RWKV_WIDTH), lambda i: (0, 0))],
        out_specs=blk,
        out_shape=jax.ShapeDtypeStruct((ROWS, RWKV_WIDTH), F32),
        compiler_params=_params(("parallel",)),
        name="rwkvpost",
    )(y, bonus, g, lnx_w, lnx_b, head_gather, head_spread)


KV_T = 512
KV_PACK = 2 * NSA_KV_WIDTH


def _kvprep_kernel(c_ref, s_ref, w_ref, craw_ref, ks_ref, vst_ref, kw_ref, vwt_ref):
    c = c_ref[...]
    for g in range(NSA_KV_HEADS):
        craw_ref[0, 0, g] = c[:, g * HEAD_DIM:(g + 1) * HEAD_DIM]
        craw_ref[1, 0, g] = c[:, NSA_KV_WIDTH + g * HEAD_DIM:NSA_KV_WIDTH + (g + 1) * HEAD_DIM]
    for src_ref, k_out, vt_out in ((s_ref, ks_ref, vst_ref), (w_ref, kw_ref, vwt_ref)):
        x = src_ref[...]
        vt = x[:, NSA_KV_WIDTH:].T
        for g in range(NSA_KV_HEADS):
            k_out[0, g] = x[:, g * HEAD_DIM:(g + 1) * HEAD_DIM].astype(BF16)
            vt_out[0, g] = vt[g * HEAD_DIM:(g + 1) * HEAD_DIM].astype(BF16)


def _kvprep(proj):
    nt = SEQ // KV_T
    G = NSA_KV_HEADS
    src = lambda col: pl.BlockSpec((KV_T, KV_PACK), lambda b, i, col=col: (b * nt + i, col // KV_PACK))
    natural = pl.BlockSpec((1, G, KV_T, HEAD_DIM), lambda b, i: (b, 0, i, 0))
    transposed = pl.BlockSpec((1, G, HEAD_DIM, KV_T), lambda b, i: (b, 0, 0, i))
    nat_shape = jax.ShapeDtypeStruct((BATCH, G, SEQ, HEAD_DIM), BF16)
    tr_shape = jax.ShapeDtypeStruct((BATCH, G, HEAD_DIM, SEQ), BF16)
    return pl.pallas_call(
        _kvprep_kernel,
        grid=(BATCH, nt),
        in_specs=[src(COL_KC), src(COL_KS), src(COL_KW)],
        out_specs=[pl.BlockSpec((2, 1, G, KV_T, HEAD_DIM), lambda b, i: (0, b, 0, i, 0)),
                   natural, transposed, natural, transposed],
        out_shape=[jax.ShapeDtypeStruct((2, BATCH, G, SEQ, HEAD_DIM), F32), nat_shape, tr_shape, nat_shape, tr_shape],
        compiler_params=_params(("parallel", "parallel")),
        name="kvprep",
    )(proj, proj, proj)


def _compress_kernel(kv_ref, w1_ref, pe_ref, w2_ref, o_ref):
    w1 = w1_ref[0]
    top = bot = None
    for l in range(CMP_STRIDE):
        rows = kv_ref[0, 0, 0, pl.ds(l, N_CMP_PAD, stride=CMP_STRIDE), :]
        t_l = _dot(rows, w1[l * HEAD_DIM:(l + 1) * HEAD_DIM])
        b_l = _dot(rows, w1[(CMP_STRIDE + l) * HEAD_DIM:(CMP_STRIDE + l + 1) * HEAD_DIM])
        top = t_l if top is None else top + t_l
        bot = b_l if bot is None else bot + b_l
    bias = _dot(jnp.broadcast_to(pe_ref[0], (8, CMP_BLOCK * HEAD_DIM)), w1)[0:1]
    hid = top + pltpu.roll(bot, N_CMP_PAD - 1, axis=0) + bias
    c0 = float(np.sqrt(2.0 / np.pi))
    act = 0.5 * hid * (1.0 + jnp.tanh(c0 * (hid + 0.044715 * (hid * hid * hid))))
    o_ref[0, 0] = _dot(act, w2_ref[0])


def _compress(craw, w1, pe, w2):
    G = NSA_KV_HEADS
    nbg = BATCH * G
    return pl.pallas_call(
        _compress_kernel,
        grid=(2, nbg),
        in_specs=[pl.BlockSpec((1, 1, 1, SEQ, HEAD_DIM), lambda c, n: (c, n // G, n % G, 0, 0)),
                  pl.BlockSpec((1, CMP_BLOCK * HEAD_DIM, CMP_HIDDEN), lambda c, n: (c, 0, 0)),
                  pl.BlockSpec((1, 1, CMP_BLOCK * HEAD_DIM), lambda c, n: (c, 0, 0)),
                  pl.BlockSpec((1, CMP_HIDDEN, HEAD_DIM), lambda c, n: (c, 0, 0))],
        out_specs=pl.BlockSpec((1, 1, N_CMP_PAD, HEAD_DIM), lambda c, n: (c, n, 0, 0)),
        out_shape=jax.ShapeDtypeStruct((2, nbg, N_CMP_PAD, HEAD_DIM), F32),
        compiler_params=_params(("parallel", "parallel")),
        name="compress",
    )(craw, w1, pe, w2)


MASK_BIG = float(2.0 ** 100)
KEY_CHUNK = 512
N_FEAT = 64


def _key_features():
    s = np.arange(SEQ)
    f = np.zeros((SEQ, N_FEAT), np.float32)
    f[s, s // SEL_BLOCK] = 1.0
    f[:, 32:35] = (SEL_BLOCK * (s // SEL_BLOCK))[:, None]
    f[:, 35:38] = (s % SEL_BLOCK)[:, None]
    return f


def _split3(x):
    a1 = x.astype(BF16)
    r1 = x - a1.astype(F32)
    a2 = r1.astype(BF16)
    a3 = (r1 - a2.astype(F32)).astype(BF16)
    return a1, a2, a3


def _nsa_kernel(q_ref, kc_ref, vc_ref, ks_ref, vst_ref, kw_ref, vwt_ref, gate_ref, ovt_ref, feat_ref,
                cmask_ref, wmask_ref, o_ref, gt_ref, slc_ref, *, tq):
    g = pl.program_id(1)
    t0 = pl.program_id(2) * tq
    hg = NSA_GROUP
    nq = hg * tq
    dot = functools.partial(jnp.dot, preferred_element_type=F32)

    qt = (q_ref[...] * HEAD_DIM ** -0.5).T
    qt = jnp.concatenate([qt[h * HEAD_DIM:(h + 1) * HEAD_DIM] for h in range(hg)], axis=1).astype(BF16)
    lane = lax.broadcasted_iota(jnp.int32, (1, nq), 1)
    slope = jnp.exp2(-0.5 * (g * hg + lane // tq + 1).astype(F32))
    qpos = t0 + lane % tq

    n_idx = lax.broadcasted_iota(jnp.int32, (N_CMP_PAD, 1), 0)
    dist_c = qpos.astype(F32) - (n_idx * CMP_STRIDE + (CMP_BLOCK - 1)).astype(F32)
    mask_c = (dist_c >= 0) & (n_idx < N_CMP)
    x_c = jnp.where(mask_c, dot(kc_ref[0, 0].astype(BF16), qt) - slope * dist_c, NEG_INF)
    p_c = jnp.where(mask_c, jnp.exp(x_c - jnp.max(x_c, axis=0, keepdims=True)), 0.0)
    den_c = jnp.maximum(jnp.sum(p_c, axis=0, keepdims=True), TINY)
    o_cmp = dot(vc_ref[0, 0].T.astype(BF16), p_c.astype(BF16)) / den_c

    p_n = p_c / den_c
    p_sum = (p_n[:, 0:tq] + p_n[:, tq:2 * tq]) + (p_n[:, 2 * tq:3 * tq] + p_n[:, 3 * tq:4 * tq])
    ovt = ovt_ref[...]
    imp = sum(dot(ovt, part) for part in _split3(p_sum))
    blk = lax.broadcasted_iota(jnp.int32, (N_SEL, tq), 0)
    cur = (t0 + lax.broadcasted_iota(jnp.int32, (N_SEL, tq), 1)) // SEL_BLOCK
    forced = (blk == 0) | (blk == cur) | (blk == cur - 1)
    score = jnp.where(blk > cur, NEG_INF, jnp.where(forced, -NEG_INF, imp))
    rank = jnp.zeros((N_SEL, tq), F32)
    for jp in range(N_SEL):
        row = score[jp:jp + 1, :]
        beats = (row > score) | ((row == score) & (blk > jp))
        rank = rank + jnp.where(beats, 1.0, 0.0)
    sel_bias = jnp.where((rank < SEL_TOPN) & (blk <= cur), 0.0, -MASK_BIG)
    sel_bias = jnp.concatenate([sel_bias] * hg, axis=1)

    s1, s2, s3 = (piece.astype(F32) for piece in _split3(slope))
    r8 = lax.broadcasted_iota(jnp.int32, (8, nq), 0)
    srow = jnp.where((r8 == 0) | (r8 == 3), s1, jnp.where((r8 == 1) | (r8 == 4), s2, s3))
    srow = jnp.where(r8 < 6, srow, 0.0)
    pad = jnp.zeros((N_FEAT - N_SEL - 8, nq), F32)
    q_slc = jnp.concatenate([qt, jnp.concatenate([sel_bias, srow, pad], axis=0).astype(BF16)], axis=0)
    q_win = jnp.concatenate([qt, jnp.concatenate([jnp.zeros_like(sel_bias), srow, pad], axis=0).astype(BF16)],
                            axis=0)

    def scores(k_ref, start, size, q_aug):
        keys = jnp.concatenate([k_ref[0, pl.ds(start, size), :], feat_ref[pl.ds(start, size), :]], axis=1)
        return dot(keys, q_aug)

    def attend(s, v_t):
        p = jnp.exp(s - jnp.max(s, axis=0, keepdims=True))
        return dot(v_t, p.astype(BF16)) / jnp.maximum(jnp.sum(p, axis=0, keepdims=True), TINY)

    def slc_variant(n_chunks):
        size = n_chunks * KEY_CHUNK
        s = scores(ks_ref, 0, size, q_slc)
        last = s[size - KEY_CHUNK:] + cmask_ref[0]
        s = last if n_chunks == 1 else jnp.concatenate([s[:size - KEY_CHUNK], last], axis=0)
        slc_ref[...] = attend(s, vst_ref[0, :, 0:size])

    for n_chunks in range(1, SEQ // KEY_CHUNK + 1):
        pl.when(t0 // KEY_CHUNK == n_chunks - 1)(functools.partial(slc_variant, n_chunks))
    o_slc = slc_ref[...]

    span = WINDOW + tq
    start = pl.multiple_of(jnp.maximum(t0 - WINDOW, 0), LANES)
    o_win = attend(scores(kw_ref, start, span, q_win) + wmask_ref[0], vwt_ref[0, :, pl.ds(start, span)])

    gt_ref[...] = _sigmoid(gate_ref[...]).T
    out = jnp.zeros((HEAD_DIM, nq), F32)
    for c, o in enumerate((o_cmp, o_slc, o_win)):
        g8 = gt_ref[pl.ds(pl.multiple_of(c * NSA_HEADS + (g // 2) * 8, 8), 8), :]
        g4 = jnp.where(g % 2 == 0, g8[0:hg], g8[hg:2 * hg])
        out = out + jnp.concatenate([g4[h:h + 1, :] for h in range(hg)], axis=1) * o
    o_ref[...] = jnp.concatenate([out[:, h * tq:(h + 1) * tq].T for h in range(hg)], axis=1)


def _nsa_masks(tq):
    qq = np.arange(NSA_GROUP * tq)[None, :] % tq
    k_c = np.arange(KEY_CHUNK)[:, None]
    causal = np.stack([k_c <= o * tq + qq for o in range(KEY_CHUNK // tq)])
    k_w = np.arange(WINDOW + tq)[:, None]
    dist = [min(o * tq, WINDOW) + qq - k_w for o in range(WINDOW // tq + 1)]
    window = np.stack([(d >= 0) & (d < WINDOW) for d in dist])
    to_bias = lambda ok: jnp.where(jnp.asarray(ok), 0.0, -MASK_BIG).astype(F32)
    return to_bias(causal), to_bias(window)


def _nsa(proj, kcvc, ks, vst, kw, vwt, overlap_t, feat):
    tq = 256
    nq = SEQ // tq
    cmask, wmask = _nsa_masks(tq)
    G = NSA_KV_HEADS
    kv_n = pl.BlockSpec((1, SEQ, HEAD_DIM), lambda b, g, i: (b * G + g, 0, 0))
    kv_t = pl.BlockSpec((1, HEAD_DIM, SEQ), lambda b, g, i: (b * G + g, 0, 0))
    cmp_spec = lambda c: pl.BlockSpec((1, 1, N_CMP_PAD, HEAD_DIM), lambda b, g, i, c=c: (c, b * G + g, 0, 0))
    return pl.pallas_call(
        functools.partial(_nsa_kernel, tq=tq),
        grid=(BATCH, G, nq),
        in_specs=[pl.BlockSpec((tq, NSA_GROUP * HEAD_DIM), lambda b, g, i: (b * nq + i, COL_Q // 256 + g)),
                  cmp_spec(0), cmp_spec(1), kv_n, kv_t, kv_n, kv_t,
                  pl.BlockSpec((tq, LANES), lambda b, g, i: (b * nq + i, COL_NG // LANES)),
                  pl.BlockSpec((N_SEL, N_CMP_PAD), lambda b, g, i: (0, 0)),
                  pl.BlockSpec((SEQ, N_FEAT), lambda b, g, i: (0, 0)),
                  pl.BlockSpec((1,) + cmask.shape[1:], lambda b, g, i: (i % (KEY_CHUNK // tq), 0, 0)),
                  pl.BlockSpec((1,) + wmask.shape[1:], lambda b, g, i: (jnp.minimum(i, WINDOW // tq), 0, 0))],
        out_specs=pl.BlockSpec((tq, NSA_GROUP * HEAD_DIM), lambda b, g, i: (b * nq + i, g)),
        out_shape=jax.ShapeDtypeStruct((ROWS, NSA_Q_WIDTH), F32),
        scratch_shapes=[pltpu.VMEM((LANES, tq), F32), pltpu.VMEM((HEAD_DIM, NSA_GROUP * tq), F32)],
        compiler_params=_params(("parallel", "parallel", "parallel")),
        name="nsa",
    )(proj, kcvc, kcvc, ks, vst, kw, vwt, proj, overlap_t, feat, cmask, wmask)


def _mix_kernel(x_ref, ya_ref, yb_ref, ga_ref, gb_ref, wa_ref, wb_ref, wo_ref, h_ref):
    ma = _dot(ya_ref[...], wa_ref[...])
    mb = _dot(yb_ref[...], wb_ref[...])
    mixed = _sigmoid(ga_ref[...]) * ma + _sigmoid(gb_ref[...]) * mb
    h_ref[...] = x_ref[...] + _dot(mixed, wo_ref[...])


def _mix(x2, ya, yb, proj, wa, wb, wo):
    tm = 256
    row = lambda w, c=0: pl.BlockSpec((tm, w), lambda i, c=c: (i, c))
    const = lambda s: pl.BlockSpec(s, lambda i: (0, 0), pipeline_mode=pl.Buffered(1))
    return pl.pallas_call(
        _mix_kernel,
        grid=(ROWS // tm,),
        in_specs=[row(D_MODEL), row(1024), row(1024), row(D_MODEL, COL_GA // D_MODEL), row(D_MODEL, COL_GB // D_MODEL),
                  const((1024, D_MODEL)), const((1024, D_MODEL)), const((D_MODEL, D_MODEL))],
        out_specs=row(D_MODEL),
        out_shape=jax.ShapeDtypeStruct((ROWS, D_MODEL), F32),
        compiler_params=_params(("parallel",)),
        name="mix",
    )(x2, ya, yb, proj, proj, wa, wb, wo)


def _mlp_kernel(h_ref, gm_ref, wu_ref, wd_ref, gf_ref, o_ref, hn_ref, acc_ref):
    f = pl.program_id(1)

    @pl.when(f == 0)
    def _():
        h = h_ref[...]
        ms = jnp.mean(h * h, axis=-1, keepdims=True)
        hn_ref[...] = (h * lax.rsqrt(ms + NORM_EPS) * gm_ref[...]).astype(BF16)
        acc_ref[...] = jnp.zeros_like(acc_ref)

    u = jnp.maximum(jnp.dot(hn_ref[...], wu_ref[...], preferred_element_type=F32), 0.0)
    acc_ref[...] += jnp.dot((u * u).astype(BF16), wd_ref[...], preferred_element_type=F32)

    @pl.when(f == pl.num_programs(1) - 1)
    def _():
        h2 = h_ref[...] + acc_ref[...]
        ms = jnp.mean(h2 * h2, axis=-1, keepdims=True)
        o_ref[...] = h2 * lax.rsqrt(ms + NORM_EPS) * gf_ref[...]


def _mlp(h, g_mlp, w_up, w_down, g_final):
    tm, tf = 512, 1024
    return pl.pallas_call(
        _mlp_kernel,
        grid=(ROWS // tm, D_FF // tf),
        in_specs=[pl.BlockSpec((tm, D_MODEL), lambda i, f: (i, 0)),
                  pl.BlockSpec((1, D_MODEL), lambda i, f: (0, 0)),
                  pl.BlockSpec((D_MODEL, tf), lambda i, f: (0, f)),
                  pl.BlockSpec((tf, D_MODEL), lambda i, f: (f, 0)),
                  pl.BlockSpec((1, D_MODEL), lambda i, f: (0, 0))],
        out_specs=pl.BlockSpec((tm, D_MODEL), lambda i, f: (i, 0)),
        out_shape=jax.ShapeDtypeStruct((ROWS, D_MODEL), F32),
        scratch_shapes=[pltpu.VMEM((tm, D_MODEL), BF16), pltpu.VMEM((tm, D_MODEL), F32)],
        compiler_params=_params(("parallel", "arbitrary")),
        name="mlp",
    )(h, g_mlp, w_up, w_down, g_final)


def _pack_rows(wt, pad):
    return jnp.concatenate([
        wt[6128:10224],
        wt[0:3168], pad(32),
        wt[3168:3264], pad(32),
        wt[3264:3520],
        wt[3520:6080],
        wt[6080:6128], pad(464),
    ], axis=0)


def _overlap_matrix():
    cs = np.arange(N_CMP)[:, None] * CMP_STRIDE
    ss = np.arange(N_SEL)[None, :] * SEL_BLOCK
    ov = np.clip(np.minimum(cs + CMP_BLOCK, ss + SEL_BLOCK) - np.maximum(cs, ss), 0, None) / CMP_BLOCK
    out = np.zeros((N_CMP_PAD, N_SEL), np.float32)
    out[:N_CMP] = ov
    return out


def kernel(x, norm_mix, w_in, rwkv_mu, rwkv_w0, rwkv_w_up, rwkv_a0, rwkv_a_up, rwkv_g_up, rwkv_k_k, rwkv_k_a,
           rwkv_r_k, rwkv_lnx_w, rwkv_lnx_b, cmp_pe_k, cmp_w1_k, cmp_w2_k, cmp_pe_v, cmp_w1_v, cmp_w2_v,
           w_out_rwkv, w_out_nsa, w_o, norm_mlp, mlp_w_up, mlp_w_down, norm_final):
    assert x.shape == (BATCH, SEQ, D_MODEL) and w_in.shape[0] == 1
    l = 0
    G = NSA_KV_HEADS
    x2 = x.reshape(ROWS, D_MODEL)
    row = lambda a: a.reshape(1, -1)

    w_p = _pack_weight(w_in[l].T)
    proj = _inproj(x2, row(norm_mix[l]), w_p)

    mu = rwkv_mu[l]
    mu_p = jnp.concatenate([mu[0:3168], jnp.zeros((32,), F32), mu[3168:3264], jnp.zeros((32,), F32), mu[3264:3520]])
    zrows = jnp.zeros((32, RWKV_WIDTH), F32)
    head_of_lane = np.arange(RWKV_WIDTH)[:, None] // HEAD_DIM == np.arange(LANES)[None, :]
    head_gather = jnp.asarray(head_of_lane, BF16)
    head_spread = jnp.asarray(head_of_lane.T, BF16)
    r, w, k, v, kk, b, g, bonus = _rwkvprep(
        proj, row(mu_p), row(rwkv_w0[l]), jnp.concatenate([rwkv_w_up[l], zrows]),
        row(rwkv_a0[l]), jnp.concatenate([rwkv_a_up[l], zrows]), rwkv_g_up[l],
        row(rwkv_k_k[l]), row(rwkv_k_a[l]), row(rwkv_r_k[l]), head_gather, head_spread)
    y = _scan(_relayout(kk, shift=True), _relayout(w), _relayout(b), _relayout(k), _relayout(r), _relayout(v, time_major=True))
    ya = _rwkvpost(_relayout_back(y), bonus, g, row(rwkv_lnx_w[l]), row(rwkv_lnx_b[l]), head_gather, head_spread)

    craw, ks, vst, kw, vwt = _kvprep(proj)
    kcvc = _compress(craw, jnp.stack([cmp_w1_k[l], cmp_w1_v[l]]),
                     jnp.stack([cmp_pe_k[l].reshape(1, -1), cmp_pe_v[l].reshape(1, -1)]),
                     jnp.stack([cmp_w2_k[l], cmp_w2_v[l]]))
    per_group = lambda a: a.reshape((BATCH * G,) + a.shape[2:])
    yb = _nsa(proj, kcvc, per_group(ks), per_group(vst), per_group(kw), per_group(vwt),
              jnp.asarray(_overlap_matrix().T, BF16), jnp.asarray(_key_features(), BF16))

    h = _mix(x2, ya, yb, proj, w_out_rwkv[l].astype(BF16), w_out_nsa[l].astype(BF16), w_o[l].astype(BF16))
    out = _mlp(h, row(norm_mlp[l]), mlp_w_up[l].astype(BF16), mlp_w_down[l].astype(BF16), row(norm_final))
    return out.reshape(BATCH, SEQ, D_MODEL)
```

```python
import functools

import numpy as np
import jax
import jax.numpy as jnp
from jax import lax
from jax.experimental import pallas as pl
from jax.experimental.pallas import tpu as pltpu

F32 = jnp.float32
BF16 = jnp.bfloat16

D_MODEL = 2048
BATCH = 4
SEQ = 2048
ROWS = BATCH * SEQ
RWKV_HEADS = 16
HEAD_DIM = 64
RWKV_WIDTH = RWKV_HEADS * HEAD_DIM
LORA_DECAY = 96
LORA_ICLR = 96
LORA_GATE = 256
GN_EPS = 64e-5
NSA_HEADS = 16
NSA_KV_HEADS = 4
NSA_GROUP = NSA_HEADS // NSA_KV_HEADS
NSA_Q_WIDTH = NSA_HEADS * HEAD_DIM
NSA_KV_WIDTH = NSA_KV_HEADS * HEAD_DIM
CMP_BLOCK = 32
CMP_STRIDE = 16
CMP_HIDDEN = 256
N_CMP = (SEQ - CMP_BLOCK) // CMP_STRIDE + 1
N_CMP_PAD = 128
SEL_BLOCK = 64
SEL_TOPN = 16
N_SEL = SEQ // SEL_BLOCK
WINDOW = 512
D_FF = 4 * D_MODEL
NORM_EPS = 1e-5
NEG_INF = -1e30
TINY = 1e-30

LANES = 128
VMEM_LIMIT = 56 * 1024 * 1024

COL_GA = 0
COL_GB = 2048
COL_R = 4096
COL_K = 5120
COL_V = 6144
COL_LORA = 7168
COL_Q = 7680
COL_KC = 8704
COL_VC = 8960
COL_KS = 9216
COL_VS = 9472
COL_KW = 9728
COL_VW = 9984
COL_NG = 10240
PROJ_COLS = 10752
LORA_PACK = 512


def _dot(a, b):
    return jnp.dot(a.astype(BF16), b.astype(BF16), preferred_element_type=F32)


def _sigmoid(x):
    return 1.0 / (1.0 + jnp.exp(-x))


def _params(sem, limit=VMEM_LIMIT):
    return pltpu.CompilerParams(dimension_semantics=sem, vmem_limit_bytes=limit)


def _inproj_kernel(x_ref, g_ref, w_ref, o_ref, xn_ref):
    @pl.when(pl.program_id(1) == 0)
    def _():
        x = x_ref[...]
        ms = jnp.mean(x * x, axis=-1, keepdims=True)
        xn_ref[...] = (x * lax.rsqrt(ms + NORM_EPS) * g_ref[...]).astype(BF16)

    o_ref[...] = jnp.dot(xn_ref[...], w_ref[...], preferred_element_type=F32)


def _pack_kernel(wt_ref, o_ref):
    wt = wt_ref[...]
    packed = _pack_rows(wt, lambda n: jnp.zeros((n, wt.shape[1]), F32))
    o_ref[...] = packed.T.astype(BF16)


def _pack_weight(w_t):
    tk = 128
    return pl.pallas_call(
        _pack_kernel,
        grid=(D_MODEL // tk,),
        in_specs=[pl.BlockSpec((w_t.shape[0], tk), lambda i: (0, i))],
        out_specs=pl.BlockSpec((tk, PROJ_COLS), lambda i: (i, 0)),
        out_shape=jax.ShapeDtypeStruct((D_MODEL, PROJ_COLS), BF16),
        compiler_params=_params(("parallel",)),
        name="packw",
    )(w_t)


def _inproj(x2, g, w_p):
    tm, tn = 1024, 1536
    return pl.pallas_call(
        _inproj_kernel,
        grid=(ROWS // tm, PROJ_COLS // tn),
        in_specs=[pl.BlockSpec((tm, D_MODEL), lambda i, j: (i, 0)),
                  pl.BlockSpec((1, D_MODEL), lambda i, j: (0, 0)),
                  pl.BlockSpec((D_MODEL, tn), lambda i, j: (0, j))],
        out_specs=pl.BlockSpec((tm, tn), lambda i, j: (i, j)),
        out_shape=jax.ShapeDtypeStruct((ROWS, PROJ_COLS), F32),
        scratch_shapes=[pltpu.VMEM((tm, D_MODEL), BF16)],
        compiler_params=_params(("parallel", "arbitrary")),
        name="inproj",
    )(x2, g, w_p)


def _dot_hi_lo(x, m):
    hi = x.astype(BF16)
    lo = (x - hi.astype(F32)).astype(BF16)
    return jnp.dot(hi, m, preferred_element_type=F32) + jnp.dot(lo, m, preferred_element_type=F32)


def _head_sum(x, gather_ref, spread_ref):
    return _dot_hi_lo(_dot_hi_lo(x, gather_ref[...]), spread_ref[...])


def _rwkvprep_kernel(r_ref, k_ref, v_ref, lo_ref, rp_ref, kp_ref, vp_ref, lop_ref,
                     mu_ref, w0_ref, wup_ref, a0_ref, aup_ref, gup_ref, kk_ref, ka_ref, rk_ref, hg_ref, hs_ref,
                     ro_ref, wo_ref, ko_ref, vo_ref, kko_ref, bo_ref, go_ref, bon_ref, *, tm):
    i = pl.program_id(0)
    has_prev = jnp.where((i * tm) % SEQ == 0, 0.0, 1.0).astype(F32)

    def shift(cur_ref, prev_ref, mu):
        z = cur_ref[...]
        zp = pltpu.roll(z, 1, axis=0)
        prev_row = prev_ref[7:8, :] * has_prev
        row = lax.broadcasted_iota(jnp.int32, z.shape, 0)
        zp = jnp.where(row == 0, prev_row, zp)
        return z + (zp - z) * mu

    r = shift(r_ref, rp_ref, mu_ref[:, 0:1024])
    k = shift(k_ref, kp_ref, mu_ref[:, 1024:2048])
    v = shift(v_ref, vp_ref, mu_ref[:, 2048:3072])
    lo = shift(lo_ref, lop_ref, mu_ref[:, 3072:3584])
    w_lo, a_lo, g_lo = lo[:, 0:128], lo[:, 128:256], lo[:, 256:512]

    u = w0_ref[...] + _dot(jnp.tanh(w_lo), wup_ref[...])
    decay = jnp.exp(-(_sigmoid(u) * float(np.exp(-0.5))))
    a = _sigmoid(a0_ref[...] + _dot(a_lo, aup_ref[...]))
    g = _dot(_sigmoid(g_lo), gup_ref[...])

    kk = k * kk_ref[...]
    ss = _head_sum(kk * kk, hg_ref, hs_ref)
    kk = kk * lax.rsqrt(jnp.maximum(ss, 1e-24))
    k = k * (1.0 + (a - 1.0) * ka_ref[...])
    coef = _head_sum(r * k * rk_ref[...], hg_ref, hs_ref)

    ro_ref[...] = r
    wo_ref[...] = decay
    ko_ref[...] = k
    vo_ref[...] = v
    kko_ref[...] = kk
    bo_ref[...] = kk * a
    go_ref[...] = g
    bon_ref[...] = coef * v


def _rwkvprep(proj, mu_p, w0, wup_p, a0, aup_p, gup, k_k, k_a, r_k, head_gather, head_spread):
    tm = 256
    pb = tm // 8
    cur = lambda c: pl.BlockSpec((tm, 1024), lambda i, c=c: (i, c))
    prev = lambda c: pl.BlockSpec((8, 1024), lambda i, c=c: (jnp.maximum(i * pb - 1, 0), c))
    full = lambda s: pl.BlockSpec(s, lambda i: (0,) * len(s))
    out = pl.BlockSpec((tm, 1024), lambda i: (i, 0))
    osh = jax.ShapeDtypeStruct((ROWS, RWKV_WIDTH), F32)
    return pl.pallas_call(
        functools.partial(_rwkvprep_kernel, tm=tm),
        grid=(ROWS // tm,),
        in_specs=[cur(COL_R // 1024), cur(COL_K // 1024), cur(COL_V // 1024),
                  pl.BlockSpec((tm, LORA_PACK), lambda i: (i, COL_LORA // LORA_PACK)),
                  prev(COL_R // 1024), prev(COL_K // 1024), prev(COL_V // 1024),
                  pl.BlockSpec((8, LORA_PACK), lambda i: (jnp.maximum(i * pb - 1, 0), COL_LORA // LORA_PACK)),
                  full((1, 3584)), full((1, 1024)), full((128, 1024)), full((1, 1024)), full((128, 1024)),
                  full((256, 1024)), full((1, 1024)), full((1, 1024)), full((1, 1024)),
                  full((RWKV_WIDTH, LANES)), full((LANES, RWKV_WIDTH))],
        out_specs=[out] * 8,
        out_shape=[osh] * 8,
        compiler_params=_params(("parallel",)),
        name="rwkvprep",
    )(proj, proj, proj, proj, proj, proj, proj, proj,
      mu_p, w0, wup_p, a0, aup_p, gup, k_k, k_a, r_k, head_gather, head_spread)


N_JPAIR = HEAD_DIM // 2
N_JOPS = 5


def _scan_kernel(kkn_ref, w_ref, b_ref, k_ref, r_ref, v_ref, y_ref, s_ref, sa_ref, d_ref, *, tc):
    @pl.when(pl.program_id(0) == 0)
    def _():
        s_ref[...] = jnp.zeros_like(s_ref)
        sa_ref[...] = jnp.zeros_like(sa_ref)

    lo_half = lax.broadcasted_iota(jnp.int32, (N_JPAIR, tc, LANES), 2) < 64
    for n, ref in enumerate((kkn_ref, w_ref, b_ref, k_ref, r_ref)):
        x = ref[...]
        xr = pltpu.roll(x, 64, axis=2)
        d_ref[n, 0:N_JPAIR] = jnp.where(lo_half, x, xr)
        d_ref[n, N_JPAIR:HEAD_DIM] = jnp.where(lo_half, xr, x)

    def tree(parts):
        while len(parts) > 1:
            parts = [parts[n] + parts[n + 1] for n in range(0, len(parts), 2)]
        return parts[0]

    def step(t, sa):
        vt = v_ref[t]
        acc_y = [None] * 4
        acc_s = [None] * 4
        for j in range(HEAD_DIM):
            row = lambda n: d_ref[n, j, pl.ds(t, 1), :]
            sn = s_ref[j] * row(1) - sa * row(2) + vt * row(3)
            s_ref[j] = sn
            ty = sn * row(4)
            ts = sn * row(0)
            acc_y[j % 4] = ty if acc_y[j % 4] is None else acc_y[j % 4] + ty
            acc_s[j % 4] = ts if acc_s[j % 4] is None else acc_s[j % 4] + ts
        y_ref[t] = tree(acc_y)
        return tree(acc_s)

    sa_ref[...] = lax.fori_loop(0, tc, step, sa_ref[...])


def _scan(kkn, w, b, k, r, v):
    tc = 64
    spec = pl.BlockSpec((tc, N_JPAIR, LANES), lambda i: (i, 0, 0))
    jspec = pl.BlockSpec((N_JPAIR, tc, LANES), lambda i: (0, i, 0))
    return pl.pallas_call(
        functools.partial(_scan_kernel, tc=tc),
        grid=(SEQ // tc,),
        in_specs=[jspec] * N_JOPS + [spec],
        out_specs=spec,
        out_shape=jax.ShapeDtypeStruct((SEQ, N_JPAIR, LANES), F32),
        scratch_shapes=[pltpu.VMEM((HEAD_DIM, N_JPAIR, LANES), F32),
                        pltpu.VMEM((N_JPAIR, LANES), F32),
                        pltpu.VMEM((N_JOPS, HEAD_DIM, tc, LANES), F32)],
        compiler_params=_params(("arbitrary",)),
        name="scan",
    )(kkn, w, b, k, r, v)


RELAYOUT_T = 128
_HALF_BATCH = [(half, b) for half in range(2) for b in range(BATCH)]


def _gather_lanes(z_ref, p):
    return jnp.concatenate([z_ref[b, pl.ds(half * N_JPAIR + p, RWKV_HEADS, stride=HEAD_DIM), :]
                            for half, b in _HALF_BATCH], axis=0)


def _relayout_kernel(x_ref, *rest, time_major, shift):
    if shift:
        nxt_ref, o_ref, z_ref = rest
        has_next = jnp.where(pl.program_id(0) == pl.num_programs(0) - 1, 0.0, 1.0).astype(F32)
        last_row = lax.broadcasted_iota(jnp.int32, (RELAYOUT_T, RWKV_WIDTH), 0) == RELAYOUT_T - 1
    else:
        o_ref, z_ref = rest
    for b in range(BATCH):
        x = x_ref[b]
        if shift:
            x = jnp.where(last_row, nxt_ref[b, 0:1, :] * has_next, pltpu.roll(x, RELAYOUT_T - 1, axis=0))
        z_ref[b] = x.T
    for p in range(N_JPAIR):
        tile = _gather_lanes(z_ref, p).T
        if time_major:
            o_ref[pl.ds(p, RELAYOUT_T, stride=N_JPAIR), :] = tile
        else:
            o_ref[p] = tile


def _rwkv_out_kernel(y_ref, bon_ref, g_ref, lw_ref, lb_ref, hg_ref, hs_ref, o_ref, z_ref):
    for p in range(N_JPAIR):
        tile = y_ref[pl.ds(p, RELAYOUT_T, stride=N_JPAIR), :].T
        for n, (half, b) in enumerate(_HALF_BATCH):
            z_ref[b, pl.ds(half * N_JPAIR + p, RWKV_HEADS, stride=HEAD_DIM), :] = (
                tile[n * RWKV_HEADS:(n + 1) * RWKV_HEADS])
    for b in range(BATCH):
        y = z_ref[b].T
        mean = _head_sum(y, hg_ref, hs_ref) * (1.0 / HEAD_DIM)
        d = y - mean
        var = _head_sum(d * d, hg_ref, hs_ref) * (1.0 / HEAD_DIM)
        yn = d * lax.rsqrt(var + GN_EPS) * lw_ref[...] + lb_ref[...]
        o_ref[b] = (yn + bon_ref[b]) * g_ref[b]


_NATURAL_SPEC = pl.BlockSpec((BATCH, RELAYOUT_T, RWKV_WIDTH), lambda i: (0, i, 0))
_TIME_MAJOR_SPEC = pl.BlockSpec((RELAYOUT_T * N_JPAIR, LANES), lambda i: (i, 0))
_RELAYOUT_SCRATCH = [pltpu.VMEM((BATCH, RWKV_WIDTH, RELAYOUT_T), F32)]


def _relayout(a, time_major=False, shift=False):
    a = a.reshape(BATCH, SEQ, RWKV_WIDTH)
    blocks_per_step = RELAYOUT_T // 8
    nxt_spec = pl.BlockSpec((BATCH, 8, RWKV_WIDTH),
                            lambda i: (0, jnp.minimum((i + 1) * blocks_per_step, SEQ // 8 - 1), 0))
    if time_major:
        out_spec, out_shape = _TIME_MAJOR_SPEC, (SEQ * N_JPAIR, LANES)
    else:
        out_spec, out_shape = pl.BlockSpec((N_JPAIR, RELAYOUT_T, LANES), lambda i: (0, i, 0)), (N_JPAIR, SEQ, LANES)
    out = pl.pallas_call(
        functools.partial(_relayout_kernel, time_major=time_major, shift=shift),
        grid=(SEQ // RELAYOUT_T,),
        in_specs=[_NATURAL_SPEC, nxt_spec] if shift else [_NATURAL_SPEC],
        out_specs=out_spec,
        out_shape=jax.ShapeDtypeStruct(out_shape, F32),
        scratch_shapes=_RELAYOUT_SCRATCH,
        compiler_params=_params(("parallel",)),
        name="relayout",
    )(*((a, a) if shift else (a,)))
    return out.reshape(SEQ, N_JPAIR, LANES) if time_major else out


def _rwkv_out(y, bonus, g, lnx_w, lnx_b, head_gather, head_spread):
    vec = pl.BlockSpec((1, RWKV_WIDTH), lambda i: (0, 0))
    nat = lambda a: a.reshape(BATCH, SEQ, RWKV_WIDTH)
    return pl.pallas_call(
        _rwkv_out_kernel,
        grid=(SEQ // RELAYOUT_T,),
        in_specs=[_TIME_MAJOR_SPEC, _NATURAL_SPEC, _NATURAL_SPEC, vec, vec,
                  pl.BlockSpec((RWKV_WIDTH, LANES), lambda i: (0, 0)),
                  pl.BlockSpec((LANES, RWKV_WIDTH), lambda i: (0, 0))],
        out_specs=_NATURAL_SPEC,
        out_shape=jax.ShapeDtypeStruct((BATCH, SEQ, RWKV_WIDTH), F32),
        scratch_shapes=_RELAYOUT_SCRATCH,
        compiler_params=_params(("parallel",)),
        name="rwkvout",
    )(y.reshape(SEQ * N_JPAIR, LANES), nat(bonus), nat(g), lnx_w, lnx_b, head_gather, head_spread
      ).reshape(ROWS, RWKV_WIDTH)


KV_T = 512
KV_PACK = 2 * NSA_KV_WIDTH


def _kvprep_kernel(c_ref, s_ref, w_ref, craw_ref, ks_ref, vst_ref, kw_ref, vwt_ref):
    c = c_ref[...]
    for g in range(NSA_KV_HEADS):
        craw_ref[0, 0, g] = c[:, g * HEAD_DIM:(g + 1) * HEAD_DIM]
        craw_ref[1, 0, g] = c[:, NSA_KV_WIDTH + g * HEAD_DIM:NSA_KV_WIDTH + (g + 1) * HEAD_DIM]
    for src_ref, k_out, vt_out in ((s_ref, ks_ref, vst_ref), (w_ref, kw_ref, vwt_ref)):
        x = src_ref[...]
        vt = x[:, NSA_KV_WIDTH:].T
        for g in range(NSA_KV_HEADS):
            k_out[0, g] = x[:, g * HEAD_DIM:(g + 1) * HEAD_DIM].astype(BF16)
            vt_out[0, g] = vt[g * HEAD_DIM:(g + 1) * HEAD_DIM].astype(BF16)


def _kvprep(proj):
    nt = SEQ // KV_T
    G = NSA_KV_HEADS
    src = lambda col: pl.BlockSpec((KV_T, KV_PACK), lambda b, i, col=col: (b * nt + i, col // KV_PACK))
    natural = pl.BlockSpec((1, G, KV_T, HEAD_DIM), lambda b, i: (b, 0, i, 0))
    transposed = pl.BlockSpec((1, G, HEAD_DIM, KV_T), lambda b, i: (b, 0, 0, i))
    nat_shape = jax.ShapeDtypeStruct((BATCH, G, SEQ, HEAD_DIM), BF16)
    tr_shape = jax.ShapeDtypeStruct((BATCH, G, HEAD_DIM, SEQ), BF16)
    return pl.pallas_call(
        _kvprep_kernel,
        grid=(BATCH, nt),
        in_specs=[src(COL_KC), src(COL_KS), src(COL_KW)],
        out_specs=[pl.BlockSpec((2, 1, G, KV_T, HEAD_DIM), lambda b, i: (0, b, 0, i, 0)),
                   natural, transposed, natural, transposed],
        out_shape=[jax.ShapeDtypeStruct((2, BATCH, G, SEQ, HEAD_DIM), F32), nat_shape, tr_shape, nat_shape, tr_shape],
        compiler_params=_params(("parallel", "parallel")),
        name="kvprep",
    )(proj, proj, proj)


def _compress_kernel(kv_ref, w1_ref, pe_ref, w2_ref, o_ref):
    w1 = w1_ref[0]
    top = bot = None
    for l in range(CMP_STRIDE):
        rows = kv_ref[0, 0, 0, pl.ds(l, N_CMP_PAD, stride=CMP_STRIDE), :]
        t_l = _dot(rows, w1[l * HEAD_DIM:(l + 1) * HEAD_DIM])
        b_l = _dot(rows, w1[(CMP_STRIDE + l) * HEAD_DIM:(CMP_STRIDE + l + 1) * HEAD_DIM])
        top = t_l if top is None else top + t_l
        bot = b_l if bot is None else bot + b_l
    bias = _dot(jnp.broadcast_to(pe_ref[0], (8, CMP_BLOCK * HEAD_DIM)), w1)[0:1]
    hid = top + pltpu.roll(bot, N_CMP_PAD - 1, axis=0) + bias
    c0 = float(np.sqrt(2.0 / np.pi))
    act = 0.5 * hid * (1.0 + jnp.tanh(c0 * (hid + 0.044715 * (hid * hid * hid))))
    o_ref[0, 0] = _dot(act, w2_ref[0])


def _compress(craw, w1, pe, w2):
    G = NSA_KV_HEADS
    nbg = BATCH * G
    return pl.pallas_call(
        _compress_kernel,
        grid=(2, nbg),
        in_specs=[pl.BlockSpec((1, 1, 1, SEQ, HEAD_DIM), lambda c, n: (c, n // G, n % G, 0, 0)),
                  pl.BlockSpec((1, CMP_BLOCK * HEAD_DIM, CMP_HIDDEN), lambda c, n: (c, 0, 0)),
                  pl.BlockSpec((1, 1, CMP_BLOCK * HEAD_DIM), lambda c, n: (c, 0, 0)),
                  pl.BlockSpec((1, CMP_HIDDEN, HEAD_DIM), lambda c, n: (c, 0, 0))],
        out_specs=pl.BlockSpec((1, 1, N_CMP_PAD, HEAD_DIM), lambda c, n: (c, n, 0, 0)),
        out_shape=jax.ShapeDtypeStruct((2, nbg, N_CMP_PAD, HEAD_DIM), F32),
        compiler_params=_params(("parallel", "parallel")),
        name="compress",
    )(craw, w1, pe, w2)


MASK_BIG = float(2.0 ** 100)
KEY_CHUNK = 512
N_FEAT = 64


def _key_features():
    s = np.arange(SEQ)
    f = np.zeros((SEQ, N_FEAT), np.float32)
    f[s, s // SEL_BLOCK] = 1.0
    f[:, 32:35] = (SEL_BLOCK * (s // SEL_BLOCK))[:, None]
    f[:, 35:38] = (s % SEL_BLOCK)[:, None]
    return f


def _split3(x):
    a1 = x.astype(BF16)
    r1 = x - a1.astype(F32)
    a2 = r1.astype(BF16)
    a3 = (r1 - a2.astype(F32)).astype(BF16)
    return a1, a2, a3


def _nsa_kernel(q_ref, kc_ref, vc_ref, ks_ref, vst_ref, kw_ref, vwt_ref, gate_ref, ovt_ref, feat_ref,
                cmask_ref, wmask_ref, o_ref, gt_ref, slc_ref, *, tq):
    g = pl.program_id(1)
    t0 = pl.program_id(2) * tq
    hg = NSA_GROUP
    nq = hg * tq
    dot = functools.partial(jnp.dot, preferred_element_type=F32)

    qt = (q_ref[...] * HEAD_DIM ** -0.5).T
    qt = jnp.concatenate([qt[h * HEAD_DIM:(h + 1) * HEAD_DIM] for h in range(hg)], axis=1).astype(BF16)
    lane = lax.broadcasted_iota(jnp.int32, (1, nq), 1)
    slope = jnp.exp2(-0.5 * (g * hg + lane // tq + 1).astype(F32))
    qpos = t0 + lane % tq

    n_idx = lax.broadcasted_iota(jnp.int32, (N_CMP_PAD, 1), 0)
    dist_c = qpos.astype(F32) - (n_idx * CMP_STRIDE + (CMP_BLOCK - 1)).astype(F32)
    mask_c = (dist_c >= 0) & (n_idx < N_CMP)
    x_c = jnp.where(mask_c, dot(kc_ref[0, 0].astype(BF16), qt) - slope * dist_c, NEG_INF)
    p_c = jnp.where(mask_c, jnp.exp(x_c - jnp.max(x_c, axis=0, keepdims=True)), 0.0)
    den_c = jnp.maximum(jnp.sum(p_c, axis=0, keepdims=True), TINY)
    o_cmp = dot(vc_ref[0, 0].T.astype(BF16), p_c.astype(BF16)) / den_c

    p_n = p_c / den_c
    p_sum = (p_n[:, 0:tq] + p_n[:, tq:2 * tq]) + (p_n[:, 2 * tq:3 * tq] + p_n[:, 3 * tq:4 * tq])
    ovt = ovt_ref[...]
    imp = sum(dot(ovt, part) for part in _split3(p_sum))
    blk = lax.broadcasted_iota(jnp.int32, (N_SEL, tq), 0)
    cur = (t0 + lax.broadcasted_iota(jnp.int32, (N_SEL, tq), 1)) // SEL_BLOCK
    forced = (blk == 0) | (blk == cur) | (blk == cur - 1)
    score = jnp.where(blk > cur, NEG_INF, jnp.where(forced, -NEG_INF, imp))
    rank = jnp.zeros((N_SEL, tq), F32)
    for jp in range(N_SEL):
        row = score[jp:jp + 1, :]
        beats = (row > score) | ((row == score) & (blk > jp))
        rank = rank + jnp.where(beats, 1.0, 0.0)
    sel_bias = jnp.where((rank < SEL_TOPN) & (blk <= cur), 0.0, -MASK_BIG)
    sel_bias = jnp.concatenate([sel_bias] * hg, axis=1)

    s1, s2, s3 = (piece.astype(F32) for piece in _split3(slope))
    r8 = lax.broadcasted_iota(jnp.int32, (8, nq), 0)
    srow = jnp.where((r8 == 0) | (r8 == 3), s1, jnp.where((r8 == 1) | (r8 == 4), s2, s3))
    srow = jnp.where(r8 < 6, srow, 0.0)
    pad = jnp.zeros((N_FEAT - N_SEL - 8, nq), F32)
    q_slc = jnp.concatenate([qt, jnp.concatenate([sel_bias, srow, pad], axis=0).astype(BF16)], axis=0)
    q_win = jnp.concatenate([qt, jnp.concatenate([jnp.zeros_like(sel_bias), srow, pad], axis=0).astype(BF16)],
                            axis=0)

    def scores(k_ref, start, size, q_aug):
        keys = jnp.concatenate([k_ref[0, pl.ds(start, size), :], feat_ref[pl.ds(start, size), :]], axis=1)
        return dot(keys, q_aug)

    def attend(s, v_t):
        p = jnp.exp(s - jnp.max(s, axis=0, keepdims=True))
        return dot(v_t, p.astype(BF16)) / jnp.maximum(jnp.sum(p, axis=0, keepdims=True), TINY)

    def slc_variant(n_chunks):
        size = n_chunks * KEY_CHUNK
        s = scores(ks_ref, 0, size, q_slc)
        last = s[size - KEY_CHUNK:] + cmask_ref[0]
        s = last if n_chunks == 1 else jnp.concatenate([s[:size - KEY_CHUNK], last], axis=0)
        slc_ref[...] = attend(s, vst_ref[0, :, 0:size])

    for n_chunks in range(1, SEQ // KEY_CHUNK + 1):
        pl.when(t0 // KEY_CHUNK == n_chunks - 1)(functools.partial(slc_variant, n_chunks))
    o_slc = slc_ref[...]

    span = WINDOW + tq
    start = pl.multiple_of(jnp.maximum(t0 - WINDOW, 0), LANES)
    o_win = attend(scores(kw_ref, start, span, q_win) + wmask_ref[0], vwt_ref[0, :, pl.ds(start, span)])

    gt_ref[...] = _sigmoid(gate_ref[...]).T
    out = jnp.zeros((HEAD_DIM, nq), F32)
    for c, o in enumerate((o_cmp, o_slc, o_win)):
        g8 = gt_ref[pl.ds(pl.multiple_of(c * NSA_HEADS + (g // 2) * 8, 8), 8), :]
        g4 = jnp.where(g % 2 == 0, g8[0:hg], g8[hg:2 * hg])
        out = out + jnp.concatenate([g4[h:h + 1, :] for h in range(hg)], axis=1) * o
    o_ref[...] = jnp.concatenate([out[:, h * tq:(h + 1) * tq].T for h in range(hg)], axis=1)


def _nsa_masks(tq):
    qq = np.arange(NSA_GROUP * tq)[None, :] % tq
    k_c = np.arange(KEY_CHUNK)[:, None]
    causal = np.stack([k_c <= o * tq + qq for o in range(KEY_CHUNK // tq)])
    k_w = np.arange(WINDOW + tq)[:, None]
    dist = [min(o * tq, WINDOW) + qq - k_w for o in range(WINDOW // tq + 1)]
    window = np.stack([(d >= 0) & (d < WINDOW) for d in dist])
    to_bias = lambda ok: jnp.where(jnp.asarray(ok), 0.0, -MASK_BIG).astype(F32)
    return to_bias(causal), to_bias(window)


def _nsa(proj, kcvc, ks, vst, kw, vwt, overlap_t, feat):
    tq = 256
    nq = SEQ // tq
    cmask, wmask = _nsa_masks(tq)
    G = NSA_KV_HEADS
    kv_n = pl.BlockSpec((1, SEQ, HEAD_DIM), lambda b, g, i: (b * G + g, 0, 0))
    kv_t = pl.BlockSpec((1, HEAD_DIM, SEQ), lambda b, g, i: (b * G + g, 0, 0))
    cmp_spec = lambda c: pl.BlockSpec((1, 1, N_CMP_PAD, HEAD_DIM), lambda b, g, i, c=c: (c, b * G + g, 0, 0))
    return pl.pallas_call(
        functools.partial(_nsa_kernel, tq=tq),
        grid=(BATCH, G, nq),
        in_specs=[pl.BlockSpec((tq, NSA_GROUP * HEAD_DIM), lambda b, g, i: (b * nq + i, COL_Q // 256 + g)),
                  cmp_spec(0), cmp_spec(1), kv_n, kv_t, kv_n, kv_t,
                  pl.BlockSpec((tq, LANES), lambda b, g, i: (b * nq + i, COL_NG // LANES)),
                  pl.BlockSpec((N_SEL, N_CMP_PAD), lambda b, g, i: (0, 0)),
                  pl.BlockSpec((SEQ, N_FEAT), lambda b, g, i: (0, 0)),
                  pl.BlockSpec((1,) + cmask.shape[1:], lambda b, g, i: (i % (KEY_CHUNK // tq), 0, 0)),
                  pl.BlockSpec((1,) + wmask.shape[1:], lambda b, g, i: (jnp.minimum(i, WINDOW // tq), 0, 0))],
        out_specs=pl.BlockSpec((tq, NSA_GROUP * HEAD_DIM), lambda b, g, i: (b * nq + i, g)),
        out_shape=jax.ShapeDtypeStruct((ROWS, NSA_Q_WIDTH), F32),
        scratch_shapes=[pltpu.VMEM((LANES, tq), F32), pltpu.VMEM((HEAD_DIM, NSA_GROUP * tq), F32)],
        compiler_params=_params(("parallel", "parallel", "parallel")),
        name="nsa",
    )(proj, kcvc, kcvc, ks, vst, kw, vwt, proj, overlap_t, feat, cmask, wmask)


def _mix_kernel(x_ref, ya_ref, yb_ref, ga_ref, gb_ref, wa_ref, wb_ref, wo_ref, h_ref):
    ma = _dot(ya_ref[...], wa_ref[...])
    mb = _dot(yb_ref[...], wb_ref[...])
    mixed = _sigmoid(ga_ref[...]) * ma + _sigmoid(gb_ref[...]) * mb
    h_ref[...] = x_ref[...] + _dot(mixed, wo_ref[...])


def _mix(x2, ya, yb, proj, wa, wb, wo):
    tm = 256
    row = lambda w, c=0: pl.BlockSpec((tm, w), lambda i, c=c: (i, c))
    const = lambda s: pl.BlockSpec(s, lambda i: (0, 0), pipeline_mode=pl.Buffered(1))
    return pl.pallas_call(
        _mix_kernel,
        grid=(ROWS // tm,),
        in_specs=[row(D_MODEL), row(1024), row(1024), row(D_MODEL, COL_GA // D_MODEL), row(D_MODEL, COL_GB // D_MODEL),
                  const((1024, D_MODEL)), const((1024, D_MODEL)), const((D_MODEL, D_MODEL))],
        out_specs=row(D_MODEL),
        out_shape=jax.ShapeDtypeStruct((ROWS, D_MODEL), F32),
        compiler_params=_params(("parallel",)),
        name="mix",
    )(x2, ya, yb, proj, proj, wa, wb, wo)


def _mlp_kernel(h_ref, gm_ref, wu_ref, wd_ref, gf_ref, o_ref, hn_ref, acc_ref):
    f = pl.program_id(1)

    @pl.when(f == 0)
    def _():
        h = h_ref[...]
        ms = jnp.mean(h * h, axis=-1, keepdims=True)
        hn_ref[...] = (h * lax.rsqrt(ms + NORM_EPS) * gm_ref[...]).astype(BF16)
        acc_ref[...] = jnp.zeros_like(acc_ref)

    u = jnp.maximum(jnp.dot(hn_ref[...], wu_ref[...], preferred_element_type=F32), 0.0)
    acc_ref[...] += jnp.dot((u * u).astype(BF16), wd_ref[...], preferred_element_type=F32)

    @pl.when(f == pl.num_programs(1) - 1)
    def _():
        h2 = h_ref[...] + acc_ref[...]
        ms = jnp.mean(h2 * h2, axis=-1, keepdims=True)
        o_ref[...] = h2 * lax.rsqrt(ms + NORM_EPS) * gf_ref[...]


def _mlp(h, g_mlp, w_up, w_down, g_final):
    tm, tf = 512, 1024
    return pl.pallas_call(
        _mlp_kernel,
        grid=(ROWS // tm, D_FF // tf),
        in_specs=[pl.BlockSpec((tm, D_MODEL), lambda i, f: (i, 0)),
                  pl.BlockSpec((1, D_MODEL), lambda i, f: (0, 0)),
                  pl.BlockSpec((D_MODEL, tf), lambda i, f: (0, f)),
                  pl.BlockSpec((tf, D_MODEL), lambda i, f: (f, 0)),
                  pl.BlockSpec((1, D_MODEL), lambda i, f: (0, 0))],
        out_specs=pl.BlockSpec((tm, D_MODEL), lambda i, f: (i, 0)),
        out_shape=jax.ShapeDtypeStruct((ROWS, D_MODEL), F32),
        scratch_shapes=[pltpu.VMEM((tm, D_MODEL), BF16), pltpu.VMEM((tm, D_MODEL), F32)],
        compiler_params=_params(("parallel", "arbitrary")),
        name="mlp",
    )(h, g_mlp, w_up, w_down, g_final)


def _pack_rows(wt, pad):
    return jnp.concatenate([
        wt[6128:10224],
        wt[0:3168], pad(32),
        wt[3168:3264], pad(32),
        wt[3264:3520],
        wt[3520:6080],
        wt[6080:6128], pad(464),
    ], axis=0)


def _overlap_matrix():
    cs = np.arange(N_CMP)[:, None] * CMP_STRIDE
    ss = np.arange(N_SEL)[None, :] * SEL_BLOCK
    ov = np.clip(np.minimum(cs + CMP_BLOCK, ss + SEL_BLOCK) - np.maximum(cs, ss), 0, None) / CMP_BLOCK
    out = np.zeros((N_CMP_PAD, N_SEL), np.float32)
    out[:N_CMP] = ov
    return out


def kernel(x, norm_mix, w_in, rwkv_mu, rwkv_w0, rwkv_w_up, rwkv_a0, rwkv_a_up, rwkv_g_up, rwkv_k_k, rwkv_k_a,
           rwkv_r_k, rwkv_lnx_w, rwkv_lnx_b, cmp_pe_k, cmp_w1_k, cmp_w2_k, cmp_pe_v, cmp_w1_v, cmp_w2_v,
           w_out_rwkv, w_out_nsa, w_o, norm_mlp, mlp_w_up, mlp_w_down, norm_final):
    assert x.shape == (BATCH, SEQ, D_MODEL) and w_in.shape[0] == 1
    l = 0
    G = NSA_KV_HEADS
    x2 = x.reshape(ROWS, D_MODEL)
    row = lambda a: a.reshape(1, -1)

    w_p = _pack_weight(w_in[l].T)
    proj = _inproj(x2, row(norm_mix[l]), w_p)

    mu = rwkv_mu[l]
    mu_p = jnp.concatenate([mu[0:3168], jnp.zeros((32,), F32), mu[3168:3264], jnp.zeros((32,), F32), mu[3264:3520]])
    zrows = jnp.zeros((32, RWKV_WIDTH), F32)
    head_of_lane = np.arange(RWKV_WIDTH)[:, None] // HEAD_DIM == np.arange(LANES)[None, :]
    head_gather = jnp.asarray(head_of_lane, BF16)
    head_spread = jnp.asarray(head_of_lane.T, BF16)
    r, w, k, v, kk, b, g, bonus = _rwkvprep(
        proj, row(mu_p), row(rwkv_w0[l]), jnp.concatenate([rwkv_w_up[l], zrows]),
        row(rwkv_a0[l]), jnp.concatenate([rwkv_a_up[l], zrows]), rwkv_g_up[l],
        row(rwkv_k_k[l]), row(rwkv_k_a[l]), row(rwkv_r_k[l]), head_gather, head_spread)
    y = _scan(_relayout(kk, shift=True), _relayout(w), _relayout(b), _relayout(k), _relayout(r), _relayout(v, time_major=True))
    ya = _rwkv_out(y, bonus, g, row(rwkv_lnx_w[l]), row(rwkv_lnx_b[l]), head_gather, head_spread)

    craw, ks, vst, kw, vwt = _kvprep(proj)
    kcvc = _compress(craw, jnp.stack([cmp_w1_k[l], cmp_w1_v[l]]),
                     jnp.stack([cmp_pe_k[l].reshape(1, -1), cmp_pe_v[l].reshape(1, -1)]),
                     jnp.stack([cmp_w2_k[l], cmp_w2_v[l]]))
    per_group = lambda a: a.reshape((BATCH * G,) + a.shape[2:])
    yb = _nsa(proj, kcvc, per_group(ks), per_group(vst), per_group(kw), per_group(vwt),
              jnp.asarray(_overlap_matrix().T, BF16), jnp.asarray(_key_features(), BF16))

    h = _mix(x2, ya, yb, proj, w_out_rwkv[l].astype(BF16), w_out_nsa[l].astype(BF16), w_o[l].astype(BF16))
    out = _mlp(h, row(norm_mlp[l]), mlp_w_up[l].astype(BF16), mlp_w_down[l].astype(BF16), row(norm_final))
    return out.reshape(BATCH, SEQ, D_MODEL)
```

```python
import functools

import numpy as np
import jax
import jax.numpy as jnp
from jax import lax
from jax.experimental import pallas as pl
from jax.experimental.pallas import tpu as pltpu

F32 = jnp.float32
BF16 = jnp.bfloat16

D_MODEL = 2048
BATCH = 4
SEQ = 2048
ROWS = BATCH * SEQ
RWKV_HEADS = 16
HEAD_DIM = 64
RWKV_WIDTH = RWKV_HEADS * HEAD_DIM
LORA_DECAY = 96
LORA_ICLR = 96
LORA_GATE = 256
GN_EPS = 64e-5
NSA_HEADS = 16
NSA_KV_HEADS = 4
NSA_GROUP = NSA_HEADS // NSA_KV_HEADS
NSA_Q_WIDTH = NSA_HEADS * HEAD_DIM
NSA_KV_WIDTH = NSA_KV_HEADS * HEAD_DIM
CMP_BLOCK = 32
CMP_STRIDE = 16
CMP_HIDDEN = 256
N_CMP = (SEQ - CMP_BLOCK) // CMP_STRIDE + 1
N_CMP_PAD = 128
SEL_BLOCK = 64
SEL_TOPN = 16
N_SEL = SEQ // SEL_BLOCK
WINDOW = 512
D_FF = 4 * D_MODEL
NORM_EPS = 1e-5
NEG_INF = -1e30
TINY = 1e-30

LANES = 128
VMEM_LIMIT = 56 * 1024 * 1024

COL_GA = 0
COL_GB = 2048
COL_R = 4096
COL_K = 5120
COL_V = 6144
COL_LORA = 7168
COL_Q = 7680
COL_KC = 8704
COL_VC = 8960
COL_KS = 9216
COL_VS = 9472
COL_KW = 9728
COL_VW = 9984
COL_NG = 10240
PROJ_COLS = 10752
LORA_PACK = 512


def _dot(a, b):
    return jnp.dot(a.astype(BF16), b.astype(BF16), preferred_element_type=F32)


def _sigmoid(x):
    return 1.0 / (1.0 + jnp.exp(-x))


def _params(sem, limit=VMEM_LIMIT):
    return pltpu.CompilerParams(dimension_semantics=sem, vmem_limit_bytes=limit)


def _inproj_kernel(x_ref, g_ref, w_ref, o_ref, xn_ref):
    @pl.when(pl.program_id(1) == 0)
    def _():
        x = x_ref[...]
        ms = jnp.mean(x * x, axis=-1, keepdims=True)
        xn_ref[...] = (x * lax.rsqrt(ms + NORM_EPS) * g_ref[...]).astype(BF16)

    o_ref[...] = jnp.dot(xn_ref[...], w_ref[...], preferred_element_type=F32)


def _pack_kernel(wt_ref, o_ref):
    wt = wt_ref[...]
    packed = _pack_rows(wt, lambda n: jnp.zeros((n, wt.shape[1]), F32))
    o_ref[...] = packed.T.astype(BF16)


def _pack_weight(w_t):
    tk = 128
    return pl.pallas_call(
        _pack_kernel,
        grid=(D_MODEL // tk,),
        in_specs=[pl.BlockSpec((w_t.shape[0], tk), lambda i: (0, i))],
        out_specs=pl.BlockSpec((tk, PROJ_COLS), lambda i: (i, 0)),
        out_shape=jax.ShapeDtypeStruct((D_MODEL, PROJ_COLS), BF16),
        compiler_params=_params(("parallel",)),
        name="packw",
    )(w_t)


def _inproj(x2, g, w_p):
    tm, tn = 1024, 1536
    return pl.pallas_call(
        _inproj_kernel,
        grid=(ROWS // tm, PROJ_COLS // tn),
        in_specs=[pl.BlockSpec((tm, D_MODEL), lambda i, j: (i, 0)),
                  pl.BlockSpec((1, D_MODEL), lambda i, j: (0, 0)),
                  pl.BlockSpec((D_MODEL, tn), lambda i, j: (0, j))],
        out_specs=pl.BlockSpec((tm, tn), lambda i, j: (i, j)),
        out_shape=jax.ShapeDtypeStruct((ROWS, PROJ_COLS), F32),
        scratch_shapes=[pltpu.VMEM((tm, D_MODEL), BF16)],
        compiler_params=_params(("parallel", "arbitrary")),
        name="inproj",
    )(x2, g, w_p)


def _dot_hi_lo(x, m):
    hi = x.astype(BF16)
    lo = (x - hi.astype(F32)).astype(BF16)
    return jnp.dot(hi, m, preferred_element_type=F32) + jnp.dot(lo, m, preferred_element_type=F32)


def _head_sum(x, gather_ref, spread_ref):
    return _dot_hi_lo(_dot_hi_lo(x, gather_ref[...]), spread_ref[...])


def _rwkvprep_kernel(r_ref, k_ref, v_ref, lo_ref, rp_ref, kp_ref, vp_ref, lop_ref,
                     mu_ref, w0_ref, wup_ref, a0_ref, aup_ref, gup_ref, kk_ref, ka_ref, rk_ref, hg_ref, hs_ref,
                     ro_ref, wo_ref, ko_ref, vo_ref, kko_ref, bo_ref, go_ref, bon_ref, *, tm):
    i = pl.program_id(0)
    has_prev = jnp.where((i * tm) % SEQ == 0, 0.0, 1.0).astype(F32)

    def shift(cur_ref, prev_ref, mu):
        z = cur_ref[...]
        zp = pltpu.roll(z, 1, axis=0)
        prev_row = prev_ref[7:8, :] * has_prev
        row = lax.broadcasted_iota(jnp.int32, z.shape, 0)
        zp = jnp.where(row == 0, prev_row, zp)
        return z + (zp - z) * mu

    r = shift(r_ref, rp_ref, mu_ref[:, 0:1024])
    k = shift(k_ref, kp_ref, mu_ref[:, 1024:2048])
    v = shift(v_ref, vp_ref, mu_ref[:, 2048:3072])
    lo = shift(lo_ref, lop_ref, mu_ref[:, 3072:3584])
    w_lo, a_lo, g_lo = lo[:, 0:128], lo[:, 128:256], lo[:, 256:512]

    u = w0_ref[...] + _dot(jnp.tanh(w_lo), wup_ref[...])
    decay = jnp.exp(-(_sigmoid(u) * float(np.exp(-0.5))))
    a = _sigmoid(a0_ref[...] + _dot(a_lo, aup_ref[...]))
    g = _dot(_sigmoid(g_lo), gup_ref[...])

    kk = k * kk_ref[...]
    ss = _head_sum(kk * kk, hg_ref, hs_ref)
    kk = kk * lax.rsqrt(jnp.maximum(ss, 1e-24))
    k = k * (1.0 + (a - 1.0) * ka_ref[...])
    coef = _head_sum(r * k * rk_ref[...], hg_ref, hs_ref)

    ro_ref[...] = r
    wo_ref[...] = decay
    ko_ref[...] = k
    vo_ref[...] = v
    kko_ref[...] = kk
    bo_ref[...] = kk * a
    go_ref[...] = g
    bon_ref[...] = coef * v


def _rwkvprep(proj, mu_p, w0, wup_p, a0, aup_p, gup, k_k, k_a, r_k, head_gather, head_spread):
    tm = 256
    pb = tm // 8
    cur = lambda c: pl.BlockSpec((tm, 1024), lambda i, c=c: (i, c))
    prev = lambda c: pl.BlockSpec((8, 1024), lambda i, c=c: (jnp.maximum(i * pb - 1, 0), c))
    full = lambda s: pl.BlockSpec(s, lambda i: (0,) * len(s))
    out = pl.BlockSpec((tm, 1024), lambda i: (i, 0))
    osh = jax.ShapeDtypeStruct((ROWS, RWKV_WIDTH), F32)
    return pl.pallas_call(
        functools.partial(_rwkvprep_kernel, tm=tm),
        grid=(ROWS // tm,),
        in_specs=[cur(COL_R // 1024), cur(COL_K // 1024), cur(COL_V // 1024),
                  pl.BlockSpec((tm, LORA_PACK), lambda i: (i, COL_LORA // LORA_PACK)),
                  prev(COL_R // 1024), prev(COL_K // 1024), prev(COL_V // 1024),
                  pl.BlockSpec((8, LORA_PACK), lambda i: (jnp.maximum(i * pb - 1, 0), COL_LORA // LORA_PACK)),
                  full((1, 3584)), full((1, 1024)), full((128, 1024)), full((1, 1024)), full((128, 1024)),
                  full((256, 1024)), full((1, 1024)), full((1, 1024)), full((1, 1024)),
                  full((RWKV_WIDTH, LANES)), full((LANES, RWKV_WIDTH))],
        out_specs=[out] * 8,
        out_shape=[osh] * 8,
        compiler_params=_params(("parallel",)),
        name="rwkvprep",
    )(proj, proj, proj, proj, proj, proj, proj, proj,
      mu_p, w0, wup_p, a0, aup_p, gup, k_k, k_a, r_k, head_gather, head_spread)


N_JPAIR = HEAD_DIM // 2
N_JOPS = 5


def _scan_kernel(kkn_ref, w_ref, b_ref, k_ref, r_ref, v_ref, y_ref, s_ref, sa_ref, d_ref, *, tc):
    @pl.when(pl.program_id(0) == 0)
    def _():
        s_ref[...] = jnp.zeros_like(s_ref)
        sa_ref[...] = jnp.zeros_like(sa_ref)

    lo_half = lax.broadcasted_iota(jnp.int32, (N_JPAIR, tc, LANES), 2) < 64
    for n, ref in enumerate((kkn_ref, w_ref, b_ref, k_ref, r_ref)):
        x = ref[...]
        xr = pltpu.roll(x, 64, axis=2)
        d_ref[n, 0:N_JPAIR] = jnp.where(lo_half, x, xr)
        d_ref[n, N_JPAIR:HEAD_DIM] = jnp.where(lo_half, xr, x)

    def tree(parts):
        while len(parts) > 1:
            parts = [parts[n] + parts[n + 1] for n in range(0, len(parts), 2)]
        return parts[0]

    def step(t, sa):
        vt = v_ref[t]
        acc_y = [None] * 4
        acc_s = [None] * 4
        for j in range(HEAD_DIM):
            row = lambda n: d_ref[n, j, pl.ds(t, 1), :]
            sn = s_ref[j] * row(1) - sa * row(2) + vt * row(3)
            s_ref[j] = sn
            ty = sn * row(4)
            ts = sn * row(0)
            acc_y[j % 4] = ty if acc_y[j % 4] is None else acc_y[j % 4] + ty
            acc_s[j % 4] = ts if acc_s[j % 4] is None else acc_s[j % 4] + ts
        y_ref[t] = tree(acc_y)
        return tree(acc_s)

    sa_ref[...] = lax.fori_loop(0, tc, step, sa_ref[...])


def _scan(kkn, w, b, k, r, v):
    tc = 64
    spec = pl.BlockSpec((tc, N_JPAIR, LANES), lambda i: (i, 0, 0))
    jspec = pl.BlockSpec((N_JPAIR, tc, LANES), lambda i: (0, i, 0))
    return pl.pallas_call(
        functools.partial(_scan_kernel, tc=tc),
        grid=(SEQ // tc,),
        in_specs=[jspec] * N_JOPS + [spec],
        out_specs=spec,
        out_shape=jax.ShapeDtypeStruct((SEQ, N_JPAIR, LANES), F32),
        scratch_shapes=[pltpu.VMEM((HEAD_DIM, N_JPAIR, LANES), F32),
                        pltpu.VMEM((N_JPAIR, LANES), F32),
                        pltpu.VMEM((N_JOPS, HEAD_DIM, tc, LANES), F32)],
        compiler_params=_params(("arbitrary",)),
        name="scan",
    )(kkn, w, b, k, r, v)


RELAYOUT_T = 128
_HALF_BATCH = [(half, b) for half in range(2) for b in range(BATCH)]


def _gather_lanes(z_ref, p):
    return jnp.concatenate([z_ref[b, pl.ds(half * N_JPAIR + p, RWKV_HEADS, stride=HEAD_DIM), :]
                            for half, b in _HALF_BATCH], axis=0)


def _relayout_kernel(*refs, modes):
    n_in = len(modes) + modes.count("shift")
    in_refs, out_refs, z_ref = iter(refs[:n_in]), refs[n_in:-1], refs[-1]
    has_next = jnp.where(pl.program_id(0) == pl.num_programs(0) - 1, 0.0, 1.0).astype(F32)
    last_row = lax.broadcasted_iota(jnp.int32, (RELAYOUT_T, RWKV_WIDTH), 0) == RELAYOUT_T - 1
    for mode, o_ref in zip(modes, out_refs):
        x_ref = next(in_refs)
        nxt_ref = next(in_refs) if mode == "shift" else None
        for b in range(BATCH):
            x = x_ref[b]
            if mode == "shift":
                x = jnp.where(last_row, nxt_ref[b, 0:1, :] * has_next, pltpu.roll(x, RELAYOUT_T - 1, axis=0))
            z_ref[b] = x.T
        for p in range(N_JPAIR):
            tile = _gather_lanes(z_ref, p).T
            if mode == "time_major":
                o_ref[pl.ds(p, RELAYOUT_T, stride=N_JPAIR), :] = tile
            else:
                o_ref[p] = tile


def _rwkv_out_kernel(y_ref, bon_ref, g_ref, lw_ref, lb_ref, hg_ref, hs_ref, o_ref, z_ref):
    for p in range(N_JPAIR):
        tile = y_ref[pl.ds(p, RELAYOUT_T, stride=N_JPAIR), :].T
        for n, (half, b) in enumerate(_HALF_BATCH):
            z_ref[b, pl.ds(half * N_JPAIR + p, RWKV_HEADS, stride=HEAD_DIM), :] = (
                tile[n * RWKV_HEADS:(n + 1) * RWKV_HEADS])
    for b in range(BATCH):
        y = z_ref[b].T
        mean = _head_sum(y, hg_ref, hs_ref) * (1.0 / HEAD_DIM)
        d = y - mean
        var = _head_sum(d * d, hg_ref, hs_ref) * (1.0 / HEAD_DIM)
        yn = d * lax.rsqrt(var + GN_EPS) * lw_ref[...] + lb_ref[...]
        o_ref[b] = (yn + bon_ref[b]) * g_ref[b]


_NATURAL_SPEC = pl.BlockSpec((BATCH, RELAYOUT_T, RWKV_WIDTH), lambda i: (0, i, 0))
_TIME_MAJOR_SPEC = pl.BlockSpec((RELAYOUT_T * N_JPAIR, LANES), lambda i: (i, 0))
_RELAYOUT_SCRATCH = [pltpu.VMEM((BATCH, RWKV_WIDTH, RELAYOUT_T), F32)]


def _relayout(arrays, modes):
    blocks_per_step = RELAYOUT_T // 8
    nxt_spec = pl.BlockSpec((BATCH, 8, RWKV_WIDTH),
                            lambda i: (0, jnp.minimum((i + 1) * blocks_per_step, SEQ // 8 - 1), 0))
    rows_spec = pl.BlockSpec((N_JPAIR, RELAYOUT_T, LANES), lambda i: (0, i, 0))
    operands, in_specs, out_specs, out_shapes = [], [], [], []
    for a, mode in zip(arrays, modes):
        a = a.reshape(BATCH, SEQ, RWKV_WIDTH)
        operands += [a, a] if mode == "shift" else [a]
        in_specs += [_NATURAL_SPEC, nxt_spec] if mode == "shift" else [_NATURAL_SPEC]
        out_specs.append(_TIME_MAJOR_SPEC if mode == "time_major" else rows_spec)
        out_shapes.append(jax.ShapeDtypeStruct(
            (SEQ * N_JPAIR, LANES) if mode == "time_major" else (N_JPAIR, SEQ, LANES), F32))
    outs = pl.pallas_call(
        functools.partial(_relayout_kernel, modes=tuple(modes)),
        grid=(SEQ // RELAYOUT_T,),
        in_specs=in_specs,
        out_specs=out_specs,
        out_shape=out_shapes,
        scratch_shapes=_RELAYOUT_SCRATCH,
        compiler_params=_params(("parallel",)),
        name="relayout",
    )(*operands)
    return [o.reshape(SEQ, N_JPAIR, LANES) if mode == "time_major" else o for o, mode in zip(outs, modes)]


def _rwkv_out(y, bonus, g, lnx_w, lnx_b, head_gather, head_spread):
    vec = pl.BlockSpec((1, RWKV_WIDTH), lambda i: (0, 0))
    nat = lambda a: a.reshape(BATCH, SEQ, RWKV_WIDTH)
    return pl.pallas_call(
        _rwkv_out_kernel,
        grid=(SEQ // RELAYOUT_T,),
        in_specs=[_TIME_MAJOR_SPEC, _NATURAL_SPEC, _NATURAL_SPEC, vec, vec,
                  pl.BlockSpec((RWKV_WIDTH, LANES), lambda i: (0, 0)),
                  pl.BlockSpec((LANES, RWKV_WIDTH), lambda i: (0, 0))],
        out_specs=_NATURAL_SPEC,
        out_shape=jax.ShapeDtypeStruct((BATCH, SEQ, RWKV_WIDTH), F32),
        scratch_shapes=_RELAYOUT_SCRATCH,
        compiler_params=_params(("parallel",)),
        name="rwkvout",
    )(y.reshape(SEQ * N_JPAIR, LANES), nat(bonus), nat(g), lnx_w, lnx_b, head_gather, head_spread
      ).reshape(ROWS, RWKV_WIDTH)


KV_T = 512
KV_PACK = 2 * NSA_KV_WIDTH


def _kvprep_kernel(c_ref, s_ref, w_ref, craw_ref, ks_ref, vst_ref, kw_ref, vwt_ref):
    c = c_ref[...]
    for g in range(NSA_KV_HEADS):
        craw_ref[0, 0, g] = c[:, g * HEAD_DIM:(g + 1) * HEAD_DIM]
        craw_ref[1, 0, g] = c[:, NSA_KV_WIDTH + g * HEAD_DIM:NSA_KV_WIDTH + (g + 1) * HEAD_DIM]
    for src_ref, k_out, vt_out in ((s_ref, ks_ref, vst_ref), (w_ref, kw_ref, vwt_ref)):
        x = src_ref[...]
        vt = x[:, NSA_KV_WIDTH:].T
        for g in range(NSA_KV_HEADS):
            k_out[0, g] = x[:, g * HEAD_DIM:(g + 1) * HEAD_DIM].astype(BF16)
            vt_out[0, g] = vt[g * HEAD_DIM:(g + 1) * HEAD_DIM].astype(BF16)


def _kvprep(proj):
    nt = SEQ // KV_T
    G = NSA_KV_HEADS
    src = lambda col: pl.BlockSpec((KV_T, KV_PACK), lambda b, i, col=col: (b * nt + i, col // KV_PACK))
    natural = pl.BlockSpec((1, G, KV_T, HEAD_DIM), lambda b, i: (b, 0, i, 0))
    transposed = pl.BlockSpec((1, G, HEAD_DIM, KV_T), lambda b, i: (b, 0, 0, i))
    nat_shape = jax.ShapeDtypeStruct((BATCH, G, SEQ, HEAD_DIM), BF16)
    tr_shape = jax.ShapeDtypeStruct((BATCH, G, HEAD_DIM, SEQ), BF16)
    return pl.pallas_call(
        _kvprep_kernel,
        grid=(BATCH, nt),
        in_specs=[src(COL_KC), src(COL_KS), src(COL_KW)],
        out_specs=[pl.BlockSpec((2, 1, G, KV_T, HEAD_DIM), lambda b, i: (0, b, 0, i, 0)),
                   natural, transposed, natural, transposed],
        out_shape=[jax.ShapeDtypeStruct((2, BATCH, G, SEQ, HEAD_DIM), F32), nat_shape, tr_shape, nat_shape, tr_shape],
        compiler_params=_params(("parallel", "parallel")),
        name="kvprep",
    )(proj, proj, proj)


def _compress_kernel(kv_ref, w1_ref, pe_ref, w2_ref, o_ref):
    w1 = w1_ref[0]
    top = bot = None
    for l in range(CMP_STRIDE):
        rows = kv_ref[0, 0, 0, pl.ds(l, N_CMP_PAD, stride=CMP_STRIDE), :]
        t_l = _dot(rows, w1[l * HEAD_DIM:(l + 1) * HEAD_DIM])
        b_l = _dot(rows, w1[(CMP_STRIDE + l) * HEAD_DIM:(CMP_STRIDE + l + 1) * HEAD_DIM])
        top = t_l if top is None else top + t_l
        bot = b_l if bot is None else bot + b_l
    bias = _dot(jnp.broadcast_to(pe_ref[0], (8, CMP_BLOCK * HEAD_DIM)), w1)[0:1]
    hid = top + pltpu.roll(bot, N_CMP_PAD - 1, axis=0) + bias
    c0 = float(np.sqrt(2.0 / np.pi))
    act = 0.5 * hid * (1.0 + jnp.tanh(c0 * (hid + 0.044715 * (hid * hid * hid))))
    o_ref[0, 0] = _dot(act, w2_ref[0])


def _compress(craw, w1, pe, w2):
    G = NSA_KV_HEADS
    nbg = BATCH * G
    return pl.pallas_call(
        _compress_kernel,
        grid=(2, nbg),
        in_specs=[pl.BlockSpec((1, 1, 1, SEQ, HEAD_DIM), lambda c, n: (c, n // G, n % G, 0, 0)),
                  pl.BlockSpec((1, CMP_BLOCK * HEAD_DIM, CMP_HIDDEN), lambda c, n: (c, 0, 0)),
                  pl.BlockSpec((1, 1, CMP_BLOCK * HEAD_DIM), lambda c, n: (c, 0, 0)),
                  pl.BlockSpec((1, CMP_HIDDEN, HEAD_DIM), lambda c, n: (c, 0, 0))],
        out_specs=pl.BlockSpec((1, 1, N_CMP_PAD, HEAD_DIM), lambda c, n: (c, n, 0, 0)),
        out_shape=jax.ShapeDtypeStruct((2, nbg, N_CMP_PAD, HEAD_DIM), F32),
        compiler_params=_params(("parallel", "parallel")),
        name="compress",
    )(craw, w1, pe, w2)


MASK_BIG = float(2.0 ** 100)
KEY_CHUNK = 512
N_FEAT = 64


def _key_features():
    s = np.arange(SEQ)
    f = np.zeros((SEQ, N_FEAT), np.float32)
    f[s, s // SEL_BLOCK] = 1.0
    f[:, 32:35] = (SEL_BLOCK * (s // SEL_BLOCK))[:, None]
    f[:, 35:38] = (s % SEL_BLOCK)[:, None]
    return f


def _split3(x):
    a1 = x.astype(BF16)
    r1 = x - a1.astype(F32)
    a2 = r1.astype(BF16)
    a3 = (r1 - a2.astype(F32)).astype(BF16)
    return a1, a2, a3


def _nsa_kernel(q_ref, kc_ref, vc_ref, ks_ref, vst_ref, kw_ref, vwt_ref, gate_ref, ovt_ref, feat_ref,
                cmask_ref, wmask_ref, o_ref, gt_ref, slc_ref, *, tq):
    g = pl.program_id(1)
    t0 = pl.program_id(2) * tq
    hg = NSA_GROUP
    nq = hg * tq
    dot = functools.partial(jnp.dot, preferred_element_type=F32)

    qt = (q_ref[...] * HEAD_DIM ** -0.5).T
    qt = jnp.concatenate([qt[h * HEAD_DIM:(h + 1) * HEAD_DIM] for h in range(hg)], axis=1).astype(BF16)
    lane = lax.broadcasted_iota(jnp.int32, (1, nq), 1)
    slope = jnp.exp2(-0.5 * (g * hg + lane // tq + 1).astype(F32))
    qpos = t0 + lane % tq

    n_idx = lax.broadcasted_iota(jnp.int32, (N_CMP_PAD, 1), 0)
    dist_c = qpos.astype(F32) - (n_idx * CMP_STRIDE + (CMP_BLOCK - 1)).astype(F32)
    mask_c = (dist_c >= 0) & (n_idx < N_CMP)
    x_c = jnp.where(mask_c, dot(kc_ref[0, 0].astype(BF16), qt) - slope * dist_c, NEG_INF)
    p_c = jnp.where(mask_c, jnp.exp(x_c - jnp.max(x_c, axis=0, keepdims=True)), 0.0)
    den_c = jnp.maximum(jnp.sum(p_c, axis=0, keepdims=True), TINY)
    o_cmp = dot(vc_ref[0, 0].T.astype(BF16), p_c.astype(BF16)) / den_c

    p_n = p_c / den_c
    p_sum = (p_n[:, 0:tq] + p_n[:, tq:2 * tq]) + (p_n[:, 2 * tq:3 * tq] + p_n[:, 3 * tq:4 * tq])
    ovt = ovt_ref[...]
    imp = sum(dot(ovt, part) for part in _split3(p_sum))
    blk = lax.broadcasted_iota(jnp.int32, (N_SEL, tq), 0)
    cur = (t0 + lax.broadcasted_iota(jnp.int32, (N_SEL, tq), 1)) // SEL_BLOCK
    forced = (blk == 0) | (blk == cur) | (blk == cur - 1)
    score = jnp.where(blk > cur, NEG_INF, jnp.where(forced, -NEG_INF, imp))
    rank = jnp.zeros((N_SEL, tq), F32)
    for jp in range(N_SEL):
        row = score[jp:jp + 1, :]
        beats = (row > score) | ((row == score) & (blk > jp))
        rank = rank + jnp.where(beats, 1.0, 0.0)
    sel_bias = jnp.where((rank < SEL_TOPN) & (blk <= cur), 0.0, -MASK_BIG)
    sel_bias = jnp.concatenate([sel_bias] * hg, axis=1)

    s1, s2, s3 = (piece.astype(F32) for piece in _split3(slope))
    r8 = lax.broadcasted_iota(jnp.int32, (8, nq), 0)
    srow = jnp.where((r8 == 0) | (r8 == 3), s1, jnp.where((r8 == 1) | (r8 == 4), s2, s3))
    srow = jnp.where(r8 < 6, srow, 0.0)
    pad = jnp.zeros((N_FEAT - N_SEL - 8, nq), F32)
    q_slc = jnp.concatenate([qt, jnp.concatenate([sel_bias, srow, pad], axis=0).astype(BF16)], axis=0)
    q_win = jnp.concatenate([qt, jnp.concatenate([jnp.zeros_like(sel_bias), srow, pad], axis=0).astype(BF16)],
                            axis=0)

    def scores(k_ref, start, size, q_aug):
        keys = jnp.concatenate([k_ref[0, pl.ds(start, size), :], feat_ref[pl.ds(start, size), :]], axis=1)
        return dot(keys, q_aug)

    def attend(s, v_t):
        p = jnp.exp(s - jnp.max(s, axis=0, keepdims=True))
        return dot(v_t, p.astype(BF16)) / jnp.maximum(jnp.sum(p, axis=0, keepdims=True), TINY)

    def slc_variant(n_chunks):
        size = n_chunks * KEY_CHUNK
        s = scores(ks_ref, 0, size, q_slc)
        last = s[size - KEY_CHUNK:] + cmask_ref[0]
        s = last if n_chunks == 1 else jnp.concatenate([s[:size - KEY_CHUNK], last], axis=0)
        slc_ref[...] = attend(s, vst_ref[0, :, 0:size])

    for n_chunks in range(1, SEQ // KEY_CHUNK + 1):
        pl.when(t0 // KEY_CHUNK == n_chunks - 1)(functools.partial(slc_variant, n_chunks))
    o_slc = slc_ref[...]

    span = WINDOW + tq
    start = pl.multiple_of(jnp.maximum(t0 - WINDOW, 0), LANES)
    o_win = attend(scores(kw_ref, start, span, q_win) + wmask_ref[0], vwt_ref[0, :, pl.ds(start, span)])

    gt_ref[...] = _sigmoid(gate_ref[...]).T
    out = jnp.zeros((HEAD_DIM, nq), F32)
    for c, o in enumerate((o_cmp, o_slc, o_win)):
        g8 = gt_ref[pl.ds(pl.multiple_of(c * NSA_HEADS + (g // 2) * 8, 8), 8), :]
        g4 = jnp.where(g % 2 == 0, g8[0:hg], g8[hg:2 * hg])
        out = out + jnp.concatenate([g4[h:h + 1, :] for h in range(hg)], axis=1) * o
    o_ref[...] = jnp.concatenate([out[:, h * tq:(h + 1) * tq].T for h in range(hg)], axis=1)


def _nsa_masks(tq):
    qq = np.arange(NSA_GROUP * tq)[None, :] % tq
    k_c = np.arange(KEY_CHUNK)[:, None]
    causal = np.stack([k_c <= o * tq + qq for o in range(KEY_CHUNK // tq)])
    k_w = np.arange(WINDOW + tq)[:, None]
    dist = [min(o * tq, WINDOW) + qq - k_w for o in range(WINDOW // tq + 1)]
    window = np.stack([(d >= 0) & (d < WINDOW) for d in dist])
    to_bias = lambda ok: jnp.where(jnp.asarray(ok), 0.0, -MASK_BIG).astype(F32)
    return to_bias(causal), to_bias(window)


def _nsa(proj, kcvc, ks, vst, kw, vwt, overlap_t, feat):
    tq = 256
    nq = SEQ // tq
    cmask, wmask = _nsa_masks(tq)
    G = NSA_KV_HEADS
    kv_n = pl.BlockSpec((1, SEQ, HEAD_DIM), lambda b, g, i: (b * G + g, 0, 0))
    kv_t = pl.BlockSpec((1, HEAD_DIM, SEQ), lambda b, g, i: (b * G + g, 0, 0))
    cmp_spec = lambda c: pl.BlockSpec((1, 1, N_CMP_PAD, HEAD_DIM), lambda b, g, i, c=c: (c, b * G + g, 0, 0))
    return pl.pallas_call(
        functools.partial(_nsa_kernel, tq=tq),
        grid=(BATCH, G, nq),
        in_specs=[pl.BlockSpec((tq, NSA_GROUP * HEAD_DIM), lambda b, g, i: (b * nq + i, COL_Q // 256 + g)),
                  cmp_spec(0), cmp_spec(1), kv_n, kv_t, kv_n, kv_t,
                  pl.BlockSpec((tq, LANES), lambda b, g, i: (b * nq + i, COL_NG // LANES)),
                  pl.BlockSpec((N_SEL, N_CMP_PAD), lambda b, g, i: (0, 0)),
                  pl.BlockSpec((SEQ, N_FEAT), lambda b, g, i: (0, 0)),
                  pl.BlockSpec((1,) + cmask.shape[1:], lambda b, g, i: (i % (KEY_CHUNK // tq), 0, 0)),
                  pl.BlockSpec((1,) + wmask.shape[1:], lambda b, g, i: (jnp.minimum(i, WINDOW // tq), 0, 0))],
        out_specs=pl.BlockSpec((tq, NSA_GROUP * HEAD_DIM), lambda b, g, i: (b * nq + i, g)),
        out_shape=jax.ShapeDtypeStruct((ROWS, NSA_Q_WIDTH), F32),
        scratch_shapes=[pltpu.VMEM((LANES, tq), F32), pltpu.VMEM((HEAD_DIM, NSA_GROUP * tq), F32)],
        compiler_params=_params(("parallel", "parallel", "parallel")),
        name="nsa",
    )(proj, kcvc, kcvc, ks, vst, kw, vwt, proj, overlap_t, feat, cmask, wmask)


def _mix_kernel(x_ref, ya_ref, yb_ref, ga_ref, gb_ref, wa_ref, wb_ref, wo_ref, h_ref):
    ma = _dot(ya_ref[...], wa_ref[...])
    mb = _dot(yb_ref[...], wb_ref[...])
    mixed = _sigmoid(ga_ref[...]) * ma + _sigmoid(gb_ref[...]) * mb
    h_ref[...] = x_ref[...] + _dot(mixed, wo_ref[...])


def _mix(x2, ya, yb, proj, wa, wb, wo):
    tm = 256
    row = lambda w, c=0: pl.BlockSpec((tm, w), lambda i, c=c: (i, c))
    const = lambda s: pl.BlockSpec(s, lambda i: (0, 0), pipeline_mode=pl.Buffered(1))
    return pl.pallas_call(
        _mix_kernel,
        grid=(ROWS // tm,),
        in_specs=[row(D_MODEL), row(1024), row(1024), row(D_MODEL, COL_GA // D_MODEL), row(D_MODEL, COL_GB // D_MODEL),
                  const((1024, D_MODEL)), const((1024, D_MODEL)), const((D_MODEL, D_MODEL))],
        out_specs=row(D_MODEL),
        out_shape=jax.ShapeDtypeStruct((ROWS, D_MODEL), F32),
        compiler_params=_params(("parallel",)),
        name="mix",
    )(x2, ya, yb, proj, proj, wa, wb, wo)


def _mlp_kernel(h_ref, gm_ref, wu_ref, wd_ref, gf_ref, o_ref, hn_ref, acc_ref):
    f = pl.program_id(1)

    @pl.when(f == 0)
    def _():
        h = h_ref[...]
        ms = jnp.mean(h * h, axis=-1, keepdims=True)
        hn_ref[...] = (h * lax.rsqrt(ms + NORM_EPS) * gm_ref[...]).astype(BF16)
        acc_ref[...] = jnp.zeros_like(acc_ref)

    u = jnp.maximum(jnp.dot(hn_ref[...], wu_ref[...], preferred_element_type=F32), 0.0)
    acc_ref[...] += jnp.dot((u * u).astype(BF16), wd_ref[...], preferred_element_type=F32)

    @pl.when(f == pl.num_programs(1) - 1)
    def _():
        h2 = h_ref[...] + acc_ref[...]
        ms = jnp.mean(h2 * h2, axis=-1, keepdims=True)
        o_ref[...] = h2 * lax.rsqrt(ms + NORM_EPS) * gf_ref[...]


def _mlp(h, g_mlp, w_up, w_down, g_final):
    tm, tf = 512, 1024
    return pl.pallas_call(
        _mlp_kernel,
        grid=(ROWS // tm, D_FF // tf),
        in_specs=[pl.BlockSpec((tm, D_MODEL), lambda i, f: (i, 0)),
                  pl.BlockSpec((1, D_MODEL), lambda i, f: (0, 0)),
                  pl.BlockSpec((D_MODEL, tf), lambda i, f: (0, f)),
                  pl.BlockSpec((tf, D_MODEL), lambda i, f: (f, 0)),
                  pl.BlockSpec((1, D_MODEL), lambda i, f: (0, 0))],
        out_specs=pl.BlockSpec((tm, D_MODEL), lambda i, f: (i, 0)),
        out_shape=jax.ShapeDtypeStruct((ROWS, D_MODEL), F32),
        scratch_shapes=[pltpu.VMEM((tm, D_MODEL), BF16), pltpu.VMEM((tm, D_MODEL), F32)],
        compiler_params=_params(("parallel", "arbitrary")),
        name="mlp",
    )(h, g_mlp, w_up, w_down, g_final)


def _pack_rows(wt, pad):
    return jnp.concatenate([
        wt[6128:10224],
        wt[0:3168], pad(32),
        wt[3168:3264], pad(32),
        wt[3264:3520],
        wt[3520:6080],
        wt[6080:6128], pad(464),
    ], axis=0)


def _overlap_matrix():
    cs = np.arange(N_CMP)[:, None] * CMP_STRIDE
    ss = np.arange(N_SEL)[None, :] * SEL_BLOCK
    ov = np.clip(np.minimum(cs + CMP_BLOCK, ss + SEL_BLOCK) - np.maximum(cs, ss), 0, None) / CMP_BLOCK
    out = np.zeros((N_CMP_PAD, N_SEL), np.float32)
    out[:N_CMP] = ov
    return out


def kernel(x, norm_mix, w_in, rwkv_mu, rwkv_w0, rwkv_w_up, rwkv_a0, rwkv_a_up, rwkv_g_up, rwkv_k_k, rwkv_k_a,
           rwkv_r_k, rwkv_lnx_w, rwkv_lnx_b, cmp_pe_k, cmp_w1_k, cmp_w2_k, cmp_pe_v, cmp_w1_v, cmp_w2_v,
           w_out_rwkv, w_out_nsa, w_o, norm_mlp, mlp_w_up, mlp_w_down, norm_final):
    assert x.shape == (BATCH, SEQ, D_MODEL) and w_in.shape[0] == 1
    l = 0
    G = NSA_KV_HEADS
    x2 = x.reshape(ROWS, D_MODEL)
    row = lambda a: a.reshape(1, -1)

    w_p = _pack_weight(w_in[l].T)
    proj = _inproj(x2, row(norm_mix[l]), w_p)

    mu = rwkv_mu[l]
    mu_p = jnp.concatenate([mu[0:3168], jnp.zeros((32,), F32), mu[3168:3264], jnp.zeros((32,), F32), mu[3264:3520]])
    zrows = jnp.zeros((32, RWKV_WIDTH), F32)
    head_of_lane = np.arange(RWKV_WIDTH)[:, None] // HEAD_DIM == np.arange(LANES)[None, :]
    head_gather = jnp.asarray(head_of_lane, BF16)
    head_spread = jnp.asarray(head_of_lane.T, BF16)
    r, w, k, v, kk, b, g, bonus = _rwkvprep(
        proj, row(mu_p), row(rwkv_w0[l]), jnp.concatenate([rwkv_w_up[l], zrows]),
        row(rwkv_a0[l]), jnp.concatenate([rwkv_a_up[l], zrows]), rwkv_g_up[l],
        row(rwkv_k_k[l]), row(rwkv_k_a[l]), row(rwkv_r_k[l]), head_gather, head_spread)
    kkn_s, w_s, b_s = _relayout([kk, w, b], ("shift", "rows", "rows"))
    k_s, r_s, v_s = _relayout([k, r, v], ("rows", "rows", "time_major"))
    y = _scan(kkn_s, w_s, b_s, k_s, r_s, v_s)
    ya = _rwkv_out(y, bonus, g, row(rwkv_lnx_w[l]), row(rwkv_lnx_b[l]), head_gather, head_spread)

    craw, ks, vst, kw, vwt = _kvprep(proj)
    kcvc = _compress(craw, jnp.stack([cmp_w1_k[l], cmp_w1_v[l]]),
                     jnp.stack([cmp_pe_k[l].reshape(1, -1), cmp_pe_v[l].reshape(1, -1)]),
                     jnp.stack([cmp_w2_k[l], cmp_w2_v[l]]))
    per_group = lambda a: a.reshape((BATCH * G,) + a.shape[2:])
    yb = _nsa(proj, kcvc, per_group(ks), per_group(vst), per_group(kw), per_group(vwt),
              jnp.asarray(_overlap_matrix().T, BF16), jnp.asarray(_key_features(), BF16))

    h = _mix(x2, ya, yb, proj, w_out_rwkv[l].astype(BF16), w_out_nsa[l].astype(BF16), w_o[l].astype(BF16))
    out = _mlp(h, row(norm_mlp[l]), mlp_w_up[l].astype(BF16), mlp_w_down[l].astype(BF16), row(norm_final))
    return out.reshape(BATCH, SEQ, D_MODEL)
```

```python
import functools

import numpy as np
import jax
import jax.numpy as jnp
from jax import lax
from jax.experimental import pallas as pl
from jax.experimental.pallas import tpu as pltpu

F32 = jnp.float32
BF16 = jnp.bfloat16

D_MODEL = 2048
BATCH = 4
SEQ = 2048
ROWS = BATCH * SEQ
RWKV_HEADS = 16
HEAD_DIM = 64
RWKV_WIDTH = RWKV_HEADS * HEAD_DIM
LORA_DECAY = 96
LORA_ICLR = 96
LORA_GATE = 256
GN_EPS = 64e-5
NSA_HEADS = 16
NSA_KV_HEADS = 4
NSA_GROUP = NSA_HEADS // NSA_KV_HEADS
NSA_Q_WIDTH = NSA_HEADS * HEAD_DIM
NSA_KV_WIDTH = NSA_KV_HEADS * HEAD_DIM
CMP_BLOCK = 32
CMP_STRIDE = 16
CMP_HIDDEN = 256
N_CMP = (SEQ - CMP_BLOCK) // CMP_STRIDE + 1
N_CMP_PAD = 128
SEL_BLOCK = 64
SEL_TOPN = 16
N_SEL = SEQ // SEL_BLOCK
WINDOW = 512
D_FF = 4 * D_MODEL
NORM_EPS = 1e-5
NEG_INF = -1e30
TINY = 1e-30

LANES = 128
VMEM_LIMIT = 56 * 1024 * 1024

COL_GA = 0
COL_GB = 2048
COL_R = 4096
COL_K = 5120
COL_V = 6144
COL_LORA = 7168
COL_Q = 7680
COL_KC = 8704
COL_VC = 8960
COL_KS = 9216
COL_VS = 9472
COL_KW = 9728
COL_VW = 9984
COL_NG = 10240
PROJ_COLS = 10752
LORA_PACK = 512


def _dot(a, b):
    return jnp.dot(a.astype(BF16), b.astype(BF16), preferred_element_type=F32)


def _sigmoid(x):
    return 1.0 / (1.0 + jnp.exp(-x))


def _params(sem, limit=VMEM_LIMIT):
    return pltpu.CompilerParams(dimension_semantics=sem, vmem_limit_bytes=limit)


def _inproj_kernel(x_ref, g_ref, w_ref, o_ref, xn_ref):
    @pl.when(pl.program_id(1) == 0)
    def _():
        x = x_ref[...]
        ms = jnp.mean(x * x, axis=-1, keepdims=True)
        xn_ref[...] = (x * lax.rsqrt(ms + NORM_EPS) * g_ref[...]).astype(BF16)

    o_ref[...] = jnp.dot(xn_ref[...], w_ref[...], preferred_element_type=F32)


def _pack_kernel(wt_ref, o_ref):
    rows = lambda a, b: wt_ref[a:b, :]
    pad = lambda n: jnp.zeros((n, wt_ref.shape[1]), F32)
    rkv = [wt_ref[pl.ds(blk * RWKV_WIDTH + q, RWKV_HEADS, stride=HEAD_DIM), :]
           for blk in range(3) for q in range(HEAD_DIM)]
    packed = jnp.concatenate(
        [rows(6128, 10224)]
        + rkv
        + [rows(3072, 3168), pad(32),
           rows(3168, 3264), pad(32),
           rows(3264, 3520),
           rows(3520, 6080),
           rows(6080, 6128), pad(464)], axis=0)
    o_ref[...] = packed.T.astype(BF16)


def _pack_weight(w_t):
    tk = 128
    return pl.pallas_call(
        _pack_kernel,
        grid=(D_MODEL // tk,),
        in_specs=[pl.BlockSpec((w_t.shape[0], tk), lambda i: (0, i))],
        out_specs=pl.BlockSpec((tk, PROJ_COLS), lambda i: (i, 0)),
        out_shape=jax.ShapeDtypeStruct((D_MODEL, PROJ_COLS), BF16),
        compiler_params=_params(("parallel",)),
        name="packw",
    )(w_t)


def _inproj(x2, g, w_p):
    tm, tn = 1024, 1536
    return pl.pallas_call(
        _inproj_kernel,
        grid=(ROWS // tm, PROJ_COLS // tn),
        in_specs=[pl.BlockSpec((tm, D_MODEL), lambda i, j: (i, 0)),
                  pl.BlockSpec((1, D_MODEL), lambda i, j: (0, 0)),
                  pl.BlockSpec((D_MODEL, tn), lambda i, j: (0, j))],
        out_specs=pl.BlockSpec((tm, tn), lambda i, j: (i, j)),
        out_shape=jax.ShapeDtypeStruct((ROWS, PROJ_COLS), F32),
        scratch_shapes=[pltpu.VMEM((tm, D_MODEL), BF16)],
        compiler_params=_params(("parallel", "arbitrary")),
        name="inproj",
    )(x2, g, w_p)


def _dot_hi_lo(x, m):
    hi = x.astype(BF16)
    lo = (x - hi.astype(F32)).astype(BF16)
    return jnp.dot(hi, m, preferred_element_type=F32) + jnp.dot(lo, m, preferred_element_type=F32)


def _head_sum(x, gather_ref, spread_ref):
    return _dot_hi_lo(_dot_hi_lo(x, gather_ref[...]), spread_ref[...])


def _rwkvprep_kernel(r_ref, k_ref, v_ref, lo_ref, rp_ref, kp_ref, vp_ref, lop_ref,
                     mu_ref, w0_ref, wup_ref, a0_ref, aup_ref, gup_ref, kk_ref, ka_ref, rk_ref, hg_ref, hs_ref,
                     ro_ref, wo_ref, ko_ref, vo_ref, kko_ref, bo_ref, go_ref, bon_ref, *, tm):
    i = pl.program_id(0)
    has_prev = jnp.where((i * tm) % SEQ == 0, 0.0, 1.0).astype(F32)

    def shift(cur_ref, prev_ref, mu):
        z = cur_ref[...]
        zp = pltpu.roll(z, 1, axis=0)
        prev_row = prev_ref[7:8, :] * has_prev
        row = lax.broadcasted_iota(jnp.int32, z.shape, 0)
        zp = jnp.where(row == 0, prev_row, zp)
        return z + (zp - z) * mu

    r = shift(r_ref, rp_ref, mu_ref[:, 0:1024])
    k = shift(k_ref, kp_ref, mu_ref[:, 1024:2048])
    v = shift(v_ref, vp_ref, mu_ref[:, 2048:3072])
    lo = shift(lo_ref, lop_ref, mu_ref[:, 3072:3584])
    w_lo, a_lo, g_lo = lo[:, 0:128], lo[:, 128:256], lo[:, 256:512]

    u = w0_ref[...] + _dot(jnp.tanh(w_lo), wup_ref[...])
    decay = jnp.exp(-(_sigmoid(u) * float(np.exp(-0.5))))
    a = _sigmoid(a0_ref[...] + _dot(a_lo, aup_ref[...]))
    g = _dot(_sigmoid(g_lo), gup_ref[...])

    kk = k * kk_ref[...]
    ss = _head_sum(kk * kk, hg_ref, hs_ref)
    kk = kk * lax.rsqrt(jnp.maximum(ss, 1e-24))
    k = k * (1.0 + (a - 1.0) * ka_ref[...])
    coef = _head_sum(r * k * rk_ref[...], hg_ref, hs_ref)

    ro_ref[...] = r
    wo_ref[...] = decay
    ko_ref[...] = k
    vo_ref[...] = v
    kko_ref[...] = kk
    bo_ref[...] = kk * a
    go_ref[...] = g
    bon_ref[...] = coef * v


def _rwkvprep(proj, mu_p, w0, wup_p, a0, aup_p, gup, k_k, k_a, r_k, head_gather, head_spread):
    tm = 256
    pb = tm // 8
    cur = lambda c: pl.BlockSpec((tm, 1024), lambda i, c=c: (i, c))
    prev = lambda c: pl.BlockSpec((8, 1024), lambda i, c=c: (jnp.maximum(i * pb - 1, 0), c))
    full = lambda s: pl.BlockSpec(s, lambda i: (0,) * len(s))
    out = pl.BlockSpec((tm, 1024), lambda i: (i, 0))
    osh = jax.ShapeDtypeStruct((ROWS, RWKV_WIDTH), F32)
    return pl.pallas_call(
        functools.partial(_rwkvprep_kernel, tm=tm),
        grid=(ROWS // tm,),
        in_specs=[cur(COL_R // 1024), cur(COL_K // 1024), cur(COL_V // 1024),
                  pl.BlockSpec((tm, LORA_PACK), lambda i: (i, COL_LORA // LORA_PACK)),
                  prev(COL_R // 1024), prev(COL_K // 1024), prev(COL_V // 1024),
                  pl.BlockSpec((8, LORA_PACK), lambda i: (jnp.maximum(i * pb - 1, 0), COL_LORA // LORA_PACK)),
                  full((1, 3584)), full((1, 1024)), full((128, 1024)), full((1, 1024)), full((128, 1024)),
                  full((256, 1024)), full((1, 1024)), full((1, 1024)), full((1, 1024)),
                  full((RWKV_WIDTH, LANES)), full((LANES, RWKV_WIDTH))],
        out_specs=[out] * 8,
        out_shape=[osh] * 8,
        compiler_params=_params(("parallel",)),
        name="rwkvprep",
    )(proj, proj, proj, proj, proj, proj, proj, proj,
      mu_p, w0, wup_p, a0, aup_p, gup, k_k, k_a, r_k, head_gather, head_spread)


N_JPAIR = HEAD_DIM // 2
N_JOPS = 5


def _scan_kernel(kkn_ref, w_ref, b_ref, k_ref, r_ref, v_ref, y_ref, s_ref, sa_ref, d_ref, *, tc):
    @pl.when(pl.program_id(0) == 0)
    def _():
        s_ref[...] = jnp.zeros_like(s_ref)
        sa_ref[...] = jnp.zeros_like(sa_ref)

    lo_half = lax.broadcasted_iota(jnp.int32, (N_JPAIR, tc, LANES), 2) < 64
    for n, ref in enumerate((kkn_ref, w_ref, b_ref, k_ref, r_ref)):
        x = ref[...]
        xr = pltpu.roll(x, 64, axis=2)
        d_ref[n, 0:N_JPAIR] = jnp.where(lo_half, x, xr)
        d_ref[n, N_JPAIR:HEAD_DIM] = jnp.where(lo_half, xr, x)

    def tree(parts):
        while len(parts) > 1:
            parts = [parts[n] + parts[n + 1] for n in range(0, len(parts), 2)]
        return parts[0]

    def step(t, sa):
        vt = v_ref[t]
        acc_y = [None] * 4
        acc_s = [None] * 4
        for j in range(HEAD_DIM):
            row = lambda n: d_ref[n, j, pl.ds(t, 1), :]
            sn = s_ref[j] * row(1) - sa * row(2) + vt * row(3)
            s_ref[j] = sn
            ty = sn * row(4)
            ts = sn * row(0)
            acc_y[j % 4] = ty if acc_y[j % 4] is None else acc_y[j % 4] + ty
            acc_s[j % 4] = ts if acc_s[j % 4] is None else acc_s[j % 4] + ts
        y_ref[t] = tree(acc_y)
        return tree(acc_s)

    sa_ref[...] = lax.fori_loop(0, tc, step, sa_ref[...])


def _scan(kkn, w, b, k, r, v):
    tc = 64
    spec = pl.BlockSpec((tc, N_JPAIR, LANES), lambda i: (i, 0, 0))
    jspec = pl.BlockSpec((N_JPAIR, tc, LANES), lambda i: (0, i, 0))
    return pl.pallas_call(
        functools.partial(_scan_kernel, tc=tc),
        grid=(SEQ // tc,),
        in_specs=[jspec] * N_JOPS + [spec],
        out_specs=spec,
        out_shape=jax.ShapeDtypeStruct((SEQ, N_JPAIR, LANES), F32),
        scratch_shapes=[pltpu.VMEM((HEAD_DIM, N_JPAIR, LANES), F32),
                        pltpu.VMEM((N_JPAIR, LANES), F32),
                        pltpu.VMEM((N_JOPS, HEAD_DIM, tc, LANES), F32)],
        compiler_params=_params(("arbitrary",)),
        name="scan",
    )(kkn, w, b, k, r, v)


RELAYOUT_T = 128
_HALF_BATCH = [(half, b) for half in range(2) for b in range(BATCH)]


def _gather_lanes(z_ref, p):
    return jnp.concatenate([z_ref[b, pl.ds((half * N_JPAIR + p) * RWKV_HEADS, RWKV_HEADS), :]
                            for half, b in _HALF_BATCH], axis=0)


def _relayout_kernel(*refs, modes):
    n_in = len(modes) + modes.count("shift")
    in_refs, out_refs, z_ref = iter(refs[:n_in]), refs[n_in:-1], refs[-1]
    has_next = jnp.where(pl.program_id(0) == pl.num_programs(0) - 1, 0.0, 1.0).astype(F32)
    last_row = lax.broadcasted_iota(jnp.int32, (RELAYOUT_T, RWKV_WIDTH), 0) == RELAYOUT_T - 1
    for mode, o_ref in zip(modes, out_refs):
        x_ref = next(in_refs)
        nxt_ref = next(in_refs) if mode == "shift" else None
        for b in range(BATCH):
            x = x_ref[b]
            if mode == "shift":
                x = jnp.where(last_row, nxt_ref[b, 0:1, :] * has_next, pltpu.roll(x, RELAYOUT_T - 1, axis=0))
            z_ref[b] = x.T
        for p in range(N_JPAIR):
            tile = _gather_lanes(z_ref, p).T
            if mode == "time_major":
                o_ref[pl.ds(p, RELAYOUT_T, stride=N_JPAIR), :] = tile
            else:
                o_ref[p] = tile


def _rwkv_out_kernel(y_ref, bon_ref, g_ref, lw_ref, lb_ref, hg_ref, hs_ref, o_ref, z_ref):
    for p in range(N_JPAIR):
        tile = y_ref[pl.ds(p, RELAYOUT_T, stride=N_JPAIR), :].T
        for n, (half, b) in enumerate(_HALF_BATCH):
            z_ref[b, pl.ds((half * N_JPAIR + p) * RWKV_HEADS, RWKV_HEADS), :] = (
                tile[n * RWKV_HEADS:(n + 1) * RWKV_HEADS])
    for b in range(BATCH):
        y = z_ref[b].T
        mean = _head_sum(y, hg_ref, hs_ref) * (1.0 / HEAD_DIM)
        d = y - mean
        var = _head_sum(d * d, hg_ref, hs_ref) * (1.0 / HEAD_DIM)
        yn = d * lax.rsqrt(var + GN_EPS) * lw_ref[...] + lb_ref[...]
        o_ref[b] = (yn + bon_ref[b]) * g_ref[b]


_NATURAL_SPEC = pl.BlockSpec((BATCH, RELAYOUT_T, RWKV_WIDTH), lambda i: (0, i, 0))
_TIME_MAJOR_SPEC = pl.BlockSpec((RELAYOUT_T * N_JPAIR, LANES), lambda i: (i, 0))
_RELAYOUT_SCRATCH = [pltpu.VMEM((BATCH, RWKV_WIDTH, RELAYOUT_T), F32)]


def _relayout(arrays, modes):
    blocks_per_step = RELAYOUT_T // 8
    nxt_spec = pl.BlockSpec((BATCH, 8, RWKV_WIDTH),
                            lambda i: (0, jnp.minimum((i + 1) * blocks_per_step, SEQ // 8 - 1), 0))
    rows_spec = pl.BlockSpec((N_JPAIR, RELAYOUT_T, LANES), lambda i: (0, i, 0))
    operands, in_specs, out_specs, out_shapes = [], [], [], []
    for a, mode in zip(arrays, modes):
        a = a.reshape(BATCH, SEQ, RWKV_WIDTH)
        operands += [a, a] if mode == "shift" else [a]
        in_specs += [_NATURAL_SPEC, nxt_spec] if mode == "shift" else [_NATURAL_SPEC]
        out_specs.append(_TIME_MAJOR_SPEC if mode == "time_major" else rows_spec)
        out_shapes.append(jax.ShapeDtypeStruct(
            (SEQ * N_JPAIR, LANES) if mode == "time_major" else (N_JPAIR, SEQ, LANES), F32))
    outs = pl.pallas_call(
        functools.partial(_relayout_kernel, modes=tuple(modes)),
        grid=(SEQ // RELAYOUT_T,),
        in_specs=in_specs,
        out_specs=out_specs,
        out_shape=out_shapes,
        scratch_shapes=_RELAYOUT_SCRATCH,
        compiler_params=_params(("parallel",)),
        name="relayout",
    )(*operands)
    return [o.reshape(SEQ, N_JPAIR, LANES) if mode == "time_major" else o for o, mode in zip(outs, modes)]


def _rwkv_out(y, bonus, g, lnx_w, lnx_b, head_gather, head_spread):
    vec = pl.BlockSpec((1, RWKV_WIDTH), lambda i: (0, 0))
    nat = lambda a: a.reshape(BATCH, SEQ, RWKV_WIDTH)
    return pl.pallas_call(
        _rwkv_out_kernel,
        grid=(SEQ // RELAYOUT_T,),
        in_specs=[_TIME_MAJOR_SPEC, _NATURAL_SPEC, _NATURAL_SPEC, vec, vec,
                  pl.BlockSpec((RWKV_WIDTH, LANES), lambda i: (0, 0)),
                  pl.BlockSpec((LANES, RWKV_WIDTH), lambda i: (0, 0))],
        out_specs=_NATURAL_SPEC,
        out_shape=jax.ShapeDtypeStruct((BATCH, SEQ, RWKV_WIDTH), F32),
        scratch_shapes=_RELAYOUT_SCRATCH,
        compiler_params=_params(("parallel",)),
        name="rwkvout",
    )(y.reshape(SEQ * N_JPAIR, LANES), nat(bonus), nat(g), lnx_w, lnx_b, head_gather, head_spread
      ).reshape(ROWS, RWKV_WIDTH)


KV_T = 512
KV_PACK = 2 * NSA_KV_WIDTH


def _kvprep_kernel(c_ref, s_ref, w_ref, craw_ref, ks_ref, vst_ref, kw_ref, vwt_ref):
    c = c_ref[...]
    for g in range(NSA_KV_HEADS):
        craw_ref[0, 0, g] = c[:, g * HEAD_DIM:(g + 1) * HEAD_DIM]
        craw_ref[1, 0, g] = c[:, NSA_KV_WIDTH + g * HEAD_DIM:NSA_KV_WIDTH + (g + 1) * HEAD_DIM]
    for src_ref, k_out, vt_out in ((s_ref, ks_ref, vst_ref), (w_ref, kw_ref, vwt_ref)):
        x = src_ref[...]
        vt = x[:, NSA_KV_WIDTH:].T
        for g in range(NSA_KV_HEADS):
            k_out[0, g] = x[:, g * HEAD_DIM:(g + 1) * HEAD_DIM].astype(BF16)
            vt_out[0, g] = vt[g * HEAD_DIM:(g + 1) * HEAD_DIM].astype(BF16)


def _kvprep(proj):
    nt = SEQ // KV_T
    G = NSA_KV_HEADS
    src = lambda col: pl.BlockSpec((KV_T, KV_PACK), lambda b, i, col=col: (b * nt + i, col // KV_PACK))
    natural = pl.BlockSpec((1, G, KV_T, HEAD_DIM), lambda b, i: (b, 0, i, 0))
    transposed = pl.BlockSpec((1, G, HEAD_DIM, KV_T), lambda b, i: (b, 0, 0, i))
    nat_shape = jax.ShapeDtypeStruct((BATCH, G, SEQ, HEAD_DIM), BF16)
    tr_shape = jax.ShapeDtypeStruct((BATCH, G, HEAD_DIM, SEQ), BF16)
    return pl.pallas_call(
        _kvprep_kernel,
        grid=(BATCH, nt),
        in_specs=[src(COL_KC), src(COL_KS), src(COL_KW)],
        out_specs=[pl.BlockSpec((2, 1, G, KV_T, HEAD_DIM), lambda b, i: (0, b, 0, i, 0)),
                   natural, transposed, natural, transposed],
        out_shape=[jax.ShapeDtypeStruct((2, BATCH, G, SEQ, HEAD_DIM), F32), nat_shape, tr_shape, nat_shape, tr_shape],
        compiler_params=_params(("parallel", "parallel")),
        name="kvprep",
    )(proj, proj, proj)


def _compress_kernel(kv_ref, w1_ref, pe_ref, w2_ref, o_ref):
    w1 = w1_ref[0]
    top = bot = None
    for l in range(CMP_STRIDE):
        rows = kv_ref[0, 0, 0, pl.ds(l, N_CMP_PAD, stride=CMP_STRIDE), :]
        t_l = _dot(rows, w1[l * HEAD_DIM:(l + 1) * HEAD_DIM])
        b_l = _dot(rows, w1[(CMP_STRIDE + l) * HEAD_DIM:(CMP_STRIDE + l + 1) * HEAD_DIM])
        top = t_l if top is None else top + t_l
        bot = b_l if bot is None else bot + b_l
    bias = _dot(jnp.broadcast_to(pe_ref[0], (8, CMP_BLOCK * HEAD_DIM)), w1)[0:1]
    hid = top + pltpu.roll(bot, N_CMP_PAD - 1, axis=0) + bias
    c0 = float(np.sqrt(2.0 / np.pi))
    act = 0.5 * hid * (1.0 + jnp.tanh(c0 * (hid + 0.044715 * (hid * hid * hid))))
    o_ref[0, 0] = _dot(act, w2_ref[0])


def _compress(craw, w1, pe, w2):
    G = NSA_KV_HEADS
    nbg = BATCH * G
    return pl.pallas_call(
        _compress_kernel,
        grid=(2, nbg),
        in_specs=[pl.BlockSpec((1, 1, 1, SEQ, HEAD_DIM), lambda c, n: (c, n // G, n % G, 0, 0)),
                  pl.BlockSpec((1, CMP_BLOCK * HEAD_DIM, CMP_HIDDEN), lambda c, n: (c, 0, 0)),
                  pl.BlockSpec((1, 1, CMP_BLOCK * HEAD_DIM), lambda c, n: (c, 0, 0)),
                  pl.BlockSpec((1, CMP_HIDDEN, HEAD_DIM), lambda c, n: (c, 0, 0))],
        out_specs=pl.BlockSpec((1, 1, N_CMP_PAD, HEAD_DIM), lambda c, n: (c, n, 0, 0)),
        out_shape=jax.ShapeDtypeStruct((2, nbg, N_CMP_PAD, HEAD_DIM), F32),
        compiler_params=_params(("parallel", "parallel")),
        name="compress",
    )(craw, w1, pe, w2)


MASK_BIG = float(2.0 ** 100)
KEY_CHUNK = 512
N_FEAT = 64


def _key_features():
    s = np.arange(SEQ)
    f = np.zeros((SEQ, N_FEAT), np.float32)
    f[s, s // SEL_BLOCK] = 1.0
    f[:, 32:35] = (SEL_BLOCK * (s // SEL_BLOCK))[:, None]
    f[:, 35:38] = (s % SEL_BLOCK)[:, None]
    return f


def _split3(x):
    a1 = x.astype(BF16)
    r1 = x - a1.astype(F32)
    a2 = r1.astype(BF16)
    a3 = (r1 - a2.astype(F32)).astype(BF16)
    return a1, a2, a3


def _nsa_kernel(q_ref, kc_ref, vc_ref, ks_ref, vst_ref, kw_ref, vwt_ref, gate_ref, ovt_ref, feat_ref,
                cmask_ref, wmask_ref, o_ref, gt_ref, slc_ref, *, tq):
    g = pl.program_id(1)
    t0 = pl.program_id(2) * tq
    hg = NSA_GROUP
    nq = hg * tq
    dot = functools.partial(jnp.dot, preferred_element_type=F32)

    qt = (q_ref[...] * HEAD_DIM ** -0.5).T
    qt = jnp.concatenate([qt[h * HEAD_DIM:(h + 1) * HEAD_DIM] for h in range(hg)], axis=1).astype(BF16)
    lane = lax.broadcasted_iota(jnp.int32, (1, nq), 1)
    slope = jnp.exp2(-0.5 * (g * hg + lane // tq + 1).astype(F32))
    qpos = t0 + lane % tq

    n_idx = lax.broadcasted_iota(jnp.int32, (N_CMP_PAD, 1), 0)
    dist_c = qpos.astype(F32) - (n_idx * CMP_STRIDE + (CMP_BLOCK - 1)).astype(F32)
    mask_c = (dist_c >= 0) & (n_idx < N_CMP)
    x_c = jnp.where(mask_c, dot(kc_ref[0, 0].astype(BF16), qt) - slope * dist_c, NEG_INF)
    p_c = jnp.where(mask_c, jnp.exp(x_c - jnp.max(x_c, axis=0, keepdims=True)), 0.0)
    den_c = jnp.maximum(jnp.sum(p_c, axis=0, keepdims=True), TINY)
    o_cmp = dot(vc_ref[0, 0].T.astype(BF16), p_c.astype(BF16)) / den_c

    p_n = p_c / den_c
    p_sum = (p_n[:, 0:tq] + p_n[:, tq:2 * tq]) + (p_n[:, 2 * tq:3 * tq] + p_n[:, 3 * tq:4 * tq])
    ovt = ovt_ref[...]
    imp = sum(dot(ovt, part) for part in _split3(p_sum))
    blk = lax.broadcasted_iota(jnp.int32, (N_SEL, tq), 0)
    cur = (t0 + lax.broadcasted_iota(jnp.int32, (N_SEL, tq), 1)) // SEL_BLOCK
    forced = (blk == 0) | (blk == cur) | (blk == cur - 1)
    score = jnp.where(blk > cur, NEG_INF, jnp.where(forced, -NEG_INF, imp))
    rank = jnp.zeros((N_SEL, tq), F32)
    for jp in range(N_SEL):
        row = score[jp:jp + 1, :]
        beats = (row > score) | ((row == score) & (blk > jp))
        rank = rank + jnp.where(beats, 1.0, 0.0)
    sel_bias = jnp.where((rank < SEL_TOPN) & (blk <= cur), 0.0, -MASK_BIG)
    sel_bias = jnp.concatenate([sel_bias] * hg, axis=1)

    s1, s2, s3 = (piece.astype(F32) for piece in _split3(slope))
    r8 = lax.broadcasted_iota(jnp.int32, (8, nq), 0)
    srow = jnp.where((r8 == 0) | (r8 == 3), s1, jnp.where((r8 == 1) | (r8 == 4), s2, s3))
    srow = jnp.where(r8 < 6, srow, 0.0)
    pad = jnp.zeros((N_FEAT - N_SEL - 8, nq), F32)
    q_slc = jnp.concatenate([qt, jnp.concatenate([sel_bias, srow, pad], axis=0).astype(BF16)], axis=0)
    q_win = jnp.concatenate([qt, jnp.concatenate([jnp.zeros_like(sel_bias), srow, pad], axis=0).astype(BF16)],
                            axis=0)

    def scores(k_ref, start, size, q_aug):
        keys = jnp.concatenate([k_ref[0, pl.ds(start, size), :], feat_ref[pl.ds(start, size), :]], axis=1)
        return dot(keys, q_aug)

    def attend(s, v_t):
        p = jnp.exp(s - jnp.max(s, axis=0, keepdims=True))
        return dot(v_t, p.astype(BF16)) / jnp.maximum(jnp.sum(p, axis=0, keepdims=True), TINY)

    def slc_variant(n_chunks):
        size = n_chunks * KEY_CHUNK
        s = scores(ks_ref, 0, size, q_slc)
        last = s[size - KEY_CHUNK:] + cmask_ref[0]
        s = last if n_chunks == 1 else jnp.concatenate([s[:size - KEY_CHUNK], last], axis=0)
        slc_ref[...] = attend(s, vst_ref[0, :, 0:size])

    for n_chunks in range(1, SEQ // KEY_CHUNK + 1):
        pl.when(t0 // KEY_CHUNK == n_chunks - 1)(functools.partial(slc_variant, n_chunks))
    o_slc = slc_ref[...]

    span = WINDOW + tq
    start = pl.multiple_of(jnp.maximum(t0 - WINDOW, 0), LANES)
    o_win = attend(scores(kw_ref, start, span, q_win) + wmask_ref[0], vwt_ref[0, :, pl.ds(start, span)])

    gt_ref[...] = _sigmoid(gate_ref[...]).T
    out = jnp.zeros((HEAD_DIM, nq), F32)
    for c, o in enumerate((o_cmp, o_slc, o_win)):
        g8 = gt_ref[pl.ds(pl.multiple_of(c * NSA_HEADS + (g // 2) * 8, 8), 8), :]
        g4 = jnp.where(g % 2 == 0, g8[0:hg], g8[hg:2 * hg])
        out = out + jnp.concatenate([g4[h:h + 1, :] for h in range(hg)], axis=1) * o
    o_ref[...] = jnp.concatenate([out[:, h * tq:(h + 1) * tq].T for h in range(hg)], axis=1)


def _nsa_masks(tq):
    qq = np.arange(NSA_GROUP * tq)[None, :] % tq
    k_c = np.arange(KEY_CHUNK)[:, None]
    causal = np.stack([k_c <= o * tq + qq for o in range(KEY_CHUNK // tq)])
    k_w = np.arange(WINDOW + tq)[:, None]
    dist = [min(o * tq, WINDOW) + qq - k_w for o in range(WINDOW // tq + 1)]
    window = np.stack([(d >= 0) & (d < WINDOW) for d in dist])
    to_bias = lambda ok: jnp.where(jnp.asarray(ok), 0.0, -MASK_BIG).astype(F32)
    return to_bias(causal), to_bias(window)


def _nsa(proj, kcvc, ks, vst, kw, vwt, overlap_t, feat):
    tq = 256
    nq = SEQ // tq
    cmask, wmask = _nsa_masks(tq)
    G = NSA_KV_HEADS
    kv_n = pl.BlockSpec((1, SEQ, HEAD_DIM), lambda b, g, i: (b * G + g, 0, 0))
    kv_t = pl.BlockSpec((1, HEAD_DIM, SEQ), lambda b, g, i: (b * G + g, 0, 0))
    cmp_spec = lambda c: pl.BlockSpec((1, 1, N_CMP_PAD, HEAD_DIM), lambda b, g, i, c=c: (c, b * G + g, 0, 0))
    return pl.pallas_call(
        functools.partial(_nsa_kernel, tq=tq),
        grid=(BATCH, G, nq),
        in_specs=[pl.BlockSpec((tq, NSA_GROUP * HEAD_DIM), lambda b, g, i: (b * nq + i, COL_Q // 256 + g)),
                  cmp_spec(0), cmp_spec(1), kv_n, kv_t, kv_n, kv_t,
                  pl.BlockSpec((tq, LANES), lambda b, g, i: (b * nq + i, COL_NG // LANES)),
                  pl.BlockSpec((N_SEL, N_CMP_PAD), lambda b, g, i: (0, 0)),
                  pl.BlockSpec((SEQ, N_FEAT), lambda b, g, i: (0, 0)),
                  pl.BlockSpec((1,) + cmask.shape[1:], lambda b, g, i: (i % (KEY_CHUNK // tq), 0, 0)),
                  pl.BlockSpec((1,) + wmask.shape[1:], lambda b, g, i: (jnp.minimum(i, WINDOW // tq), 0, 0))],
        out_specs=pl.BlockSpec((tq, NSA_GROUP * HEAD_DIM), lambda b, g, i: (b * nq + i, g)),
        out_shape=jax.ShapeDtypeStruct((ROWS, NSA_Q_WIDTH), F32),
        scratch_shapes=[pltpu.VMEM((LANES, tq), F32), pltpu.VMEM((HEAD_DIM, NSA_GROUP * tq), F32)],
        compiler_params=_params(("parallel", "parallel", "parallel")),
        name="nsa",
    )(proj, kcvc, kcvc, ks, vst, kw, vwt, proj, overlap_t, feat, cmask, wmask)


def _mix_kernel(x_ref, ya_ref, yb_ref, ga_ref, gb_ref, wa_ref, wb_ref, wo_ref, h_ref):
    ma = _dot(ya_ref[...], wa_ref[...])
    mb = _dot(yb_ref[...], wb_ref[...])
    mixed = _sigmoid(ga_ref[...]) * ma + _sigmoid(gb_ref[...]) * mb
    h_ref[...] = x_ref[...] + _dot(mixed, wo_ref[...])


def _mix(x2, ya, yb, proj, wa, wb, wo):
    tm = 256
    row = lambda w, c=0: pl.BlockSpec((tm, w), lambda i, c=c: (i, c))
    const = lambda s: pl.BlockSpec(s, lambda i: (0, 0), pipeline_mode=pl.Buffered(1))
    return pl.pallas_call(
        _mix_kernel,
        grid=(ROWS // tm,),
        in_specs=[row(D_MODEL), row(1024), row(1024), row(D_MODEL, COL_GA // D_MODEL), row(D_MODEL, COL_GB // D_MODEL),
                  const((1024, D_MODEL)), const((1024, D_MODEL)), const((D_MODEL, D_MODEL))],
        out_specs=row(D_MODEL),
        out_shape=jax.ShapeDtypeStruct((ROWS, D_MODEL), F32),
        compiler_params=_params(("parallel",)),
        name="mix",
    )(x2, ya, yb, proj, proj, wa, wb, wo)


def _mlp_kernel(h_ref, gm_ref, wu_ref, wd_ref, gf_ref, o_ref, hn_ref, acc_ref):
    f = pl.program_id(1)

    @pl.when(f == 0)
    def _():
        h = h_ref[...]
        ms = jnp.mean(h * h, axis=-1, keepdims=True)
        hn_ref[...] = (h * lax.rsqrt(ms + NORM_EPS) * gm_ref[...]).astype(BF16)
        acc_ref[...] = jnp.zeros_like(acc_ref)

    u = jnp.maximum(jnp.dot(hn_ref[...], wu_ref[...], preferred_element_type=F32), 0.0)
    acc_ref[...] += jnp.dot((u * u).astype(BF16), wd_ref[...], preferred_element_type=F32)

    @pl.when(f == pl.num_programs(1) - 1)
    def _():
        h2 = h_ref[...] + acc_ref[...]
        ms = jnp.mean(h2 * h2, axis=-1, keepdims=True)
        o_ref[...] = h2 * lax.rsqrt(ms + NORM_EPS) * gf_ref[...]


def _mlp(h, g_mlp, w_up, w_down, g_final):
    tm, tf = 512, 1024
    return pl.pallas_call(
        _mlp_kernel,
        grid=(ROWS // tm, D_FF // tf),
        in_specs=[pl.BlockSpec((tm, D_MODEL), lambda i, f: (i, 0)),
                  pl.BlockSpec((1, D_MODEL), lambda i, f: (0, 0)),
                  pl.BlockSpec((D_MODEL, tf), lambda i, f: (0, f)),
                  pl.BlockSpec((tf, D_MODEL), lambda i, f: (f, 0)),
                  pl.BlockSpec((1, D_MODEL), lambda i, f: (0, 0))],
        out_specs=pl.BlockSpec((tm, D_MODEL), lambda i, f: (i, 0)),
        out_shape=jax.ShapeDtypeStruct((ROWS, D_MODEL), F32),
        scratch_shapes=[pltpu.VMEM((tm, D_MODEL), BF16), pltpu.VMEM((tm, D_MODEL), F32)],
        compiler_params=_params(("parallel", "arbitrary")),
        name="mlp",
    )(h, g_mlp, w_up, w_down, g_final)


def _interleave_heads(a, axis=-1):
    a = jnp.moveaxis(a, axis, -1)
    a = a.reshape(a.shape[:-1] + (RWKV_HEADS, HEAD_DIM)).swapaxes(-1, -2).reshape(a.shape)
    return jnp.moveaxis(a, -1, axis)


def _overlap_matrix():
    cs = np.arange(N_CMP)[:, None] * CMP_STRIDE
    ss = np.arange(N_SEL)[None, :] * SEL_BLOCK
    ov = np.clip(np.minimum(cs + CMP_BLOCK, ss + SEL_BLOCK) - np.maximum(cs, ss), 0, None) / CMP_BLOCK
    out = np.zeros((N_CMP_PAD, N_SEL), np.float32)
    out[:N_CMP] = ov
    return out


def kernel(x, norm_mix, w_in, rwkv_mu, rwkv_w0, rwkv_w_up, rwkv_a0, rwkv_a_up, rwkv_g_up, rwkv_k_k, rwkv_k_a,
           rwkv_r_k, rwkv_lnx_w, rwkv_lnx_b, cmp_pe_k, cmp_w1_k, cmp_w2_k, cmp_pe_v, cmp_w1_v, cmp_w2_v,
           w_out_rwkv, w_out_nsa, w_o, norm_mlp, mlp_w_up, mlp_w_down, norm_final):
    assert x.shape == (BATCH, SEQ, D_MODEL) and w_in.shape[0] == 1
    l = 0
    G = NSA_KV_HEADS
    x2 = x.reshape(ROWS, D_MODEL)
    row = lambda a: a.reshape(1, -1)

    w_p = _pack_weight(w_in[l].T)
    proj = _inproj(x2, row(norm_mix[l]), w_p)

    il = _interleave_heads
    mu = rwkv_mu[l]
    mu_p = jnp.concatenate([il(mu[0:3072].reshape(3, RWKV_WIDTH)).reshape(-1), mu[3072:3168], jnp.zeros((32,), F32),
                            mu[3168:3264], jnp.zeros((32,), F32), mu[3264:3520]])
    zrows = jnp.zeros((32, RWKV_WIDTH), F32)
    head_of_lane = np.arange(RWKV_WIDTH)[:, None] % RWKV_HEADS == np.arange(LANES)[None, :]
    head_gather = jnp.asarray(head_of_lane, BF16)
    head_spread = jnp.asarray(head_of_lane.T, BF16)
    r, w, k, v, kk, b, g, bonus = _rwkvprep(
        proj, row(mu_p), row(il(rwkv_w0[l])), jnp.concatenate([il(rwkv_w_up[l]), zrows]),
        row(il(rwkv_a0[l])), jnp.concatenate([il(rwkv_a_up[l]), zrows]), il(rwkv_g_up[l]),
        row(il(rwkv_k_k[l].reshape(-1))), row(il(rwkv_k_a[l].reshape(-1))), row(il(rwkv_r_k[l].reshape(-1))),
        head_gather, head_spread)
    kkn_s, w_s, b_s = _relayout([kk, w, b], ("shift", "rows", "rows"))
    k_s, r_s, v_s = _relayout([k, r, v], ("rows", "rows", "time_major"))
    y = _scan(kkn_s, w_s, b_s, k_s, r_s, v_s)
    ya = _rwkv_out(y, bonus, g, row(il(rwkv_lnx_w[l])), row(il(rwkv_lnx_b[l])), head_gather, head_spread)

    craw, ks, vst, kw, vwt = _kvprep(proj)
    kcvc = _compress(craw, jnp.stack([cmp_w1_k[l], cmp_w1_v[l]]),
                     jnp.stack([cmp_pe_k[l].reshape(1, -1), cmp_pe_v[l].reshape(1, -1)]),
                     jnp.stack([cmp_w2_k[l], cmp_w2_v[l]]))
    per_group = lambda a: a.reshape((BATCH * G,) + a.shape[2:])
    yb = _nsa(proj, kcvc, per_group(ks), per_group(vst), per_group(kw), per_group(vwt),
              jnp.asarray(_overlap_matrix().T, BF16), jnp.asarray(_key_features(), BF16))

    h = _mix(x2, ya, yb, proj, il(w_out_rwkv[l], axis=0).astype(BF16), w_out_nsa[l].astype(BF16), w_o[l].astype(BF16))
    out = _mlp(h, row(norm_mlp[l]), mlp_w_up[l].astype(BF16), mlp_w_down[l].astype(BF16), row(norm_final))
    return out.reshape(BATCH, SEQ, D_MODEL)
```

```python
import functools

import numpy as np
import jax
import jax.numpy as jnp
from jax import lax
from jax.experimental import pallas as pl
from jax.experimental.pallas import tpu as pltpu

F32 = jnp.float32
BF16 = jnp.bfloat16

D_MODEL = 2048
BATCH = 4
SEQ = 2048
ROWS = BATCH * SEQ
RWKV_HEADS = 16
HEAD_DIM = 64
RWKV_WIDTH = RWKV_HEADS * HEAD_DIM
LORA_DECAY = 96
LORA_ICLR = 96
LORA_GATE = 256
GN_EPS = 64e-5
NSA_HEADS = 16
NSA_KV_HEADS = 4
NSA_GROUP = NSA_HEADS // NSA_KV_HEADS
NSA_Q_WIDTH = NSA_HEADS * HEAD_DIM
NSA_KV_WIDTH = NSA_KV_HEADS * HEAD_DIM
CMP_BLOCK = 32
CMP_STRIDE = 16
CMP_HIDDEN = 256
N_CMP = (SEQ - CMP_BLOCK) // CMP_STRIDE + 1
N_CMP_PAD = 128
SEL_BLOCK = 64
SEL_TOPN = 16
N_SEL = SEQ // SEL_BLOCK
WINDOW = 512
D_FF = 4 * D_MODEL
NORM_EPS = 1e-5
NEG_INF = -1e30
TINY = 1e-30

LANES = 128
VMEM_LIMIT = 56 * 1024 * 1024

COL_GA = 0
COL_GB = 2048
COL_R = 4096
COL_K = 5120
COL_V = 6144
COL_LORA = 7168
COL_Q = 7680
COL_KC = 8704
COL_VC = 8960
COL_KS = 9216
COL_VS = 9472
COL_KW = 9728
COL_VW = 9984
COL_NG = 10240
PROJ_COLS = 10752
LORA_PACK = 512


def _dot(a, b):
    return jnp.dot(a.astype(BF16), b.astype(BF16), preferred_element_type=F32)


def _sigmoid(x):
    return 1.0 / (1.0 + jnp.exp(-x))


def _params(sem, limit=VMEM_LIMIT):
    return pltpu.CompilerParams(dimension_semantics=sem, vmem_limit_bytes=limit)


def _inproj_kernel(x_ref, g_ref, w_ref, o_ref, xn_ref):
    @pl.when(pl.program_id(1) == 0)
    def _():
        x = x_ref[...]
        ms = jnp.mean(x * x, axis=-1, keepdims=True)
        xn_ref[...] = (x * lax.rsqrt(ms + NORM_EPS) * g_ref[...]).astype(BF16)

    o_ref[...] = jnp.dot(xn_ref[...], w_ref[...], preferred_element_type=F32)


def _pack_kernel(wt_ref, o_ref):
    rows = lambda a, b: wt_ref[a:b, :]
    pad = lambda n: jnp.zeros((n, wt_ref.shape[1]), F32)
    rkv = [wt_ref[pl.ds(blk * RWKV_WIDTH + q, RWKV_HEADS, stride=HEAD_DIM), :]
           for blk in range(3) for q in range(HEAD_DIM)]
    packed = jnp.concatenate(
        [rows(6128, 10224)]
        + rkv
        + [rows(3072, 3168), pad(32),
           rows(3168, 3264), pad(32),
           rows(3264, 3520),
           rows(3520, 6080),
           rows(6080, 6128), pad(464)], axis=0)
    o_ref[...] = packed.T.astype(BF16)


def _pack_weight(w_t):
    tk = 128
    return pl.pallas_call(
        _pack_kernel,
        grid=(D_MODEL // tk,),
        in_specs=[pl.BlockSpec((w_t.shape[0], tk), lambda i: (0, i))],
        out_specs=pl.BlockSpec((tk, PROJ_COLS), lambda i: (i, 0)),
        out_shape=jax.ShapeDtypeStruct((D_MODEL, PROJ_COLS), BF16),
        compiler_params=_params(("parallel",)),
        name="packw",
    )(w_t)


def _inproj(x2, g, w_p):
    tm, tn = 1024, 1536
    return pl.pallas_call(
        _inproj_kernel,
        grid=(ROWS // tm, PROJ_COLS // tn),
        in_specs=[pl.BlockSpec((tm, D_MODEL), lambda i, j: (i, 0)),
                  pl.BlockSpec((1, D_MODEL), lambda i, j: (0, 0)),
                  pl.BlockSpec((D_MODEL, tn), lambda i, j: (0, j))],
        out_specs=pl.BlockSpec((tm, tn), lambda i, j: (i, j)),
        out_shape=jax.ShapeDtypeStruct((ROWS, PROJ_COLS), F32),
        scratch_shapes=[pltpu.VMEM((tm, D_MODEL), BF16)],
        compiler_params=_params(("parallel", "arbitrary")),
        name="inproj",
    )(x2, g, w_p)


def _dot_hi_lo(x, m):
    hi = x.astype(BF16)
    lo = (x - hi.astype(F32)).astype(BF16)
    return jnp.dot(hi, m, preferred_element_type=F32) + jnp.dot(lo, m, preferred_element_type=F32)


def _head_sum(x, gather_ref, spread_ref):
    return _dot_hi_lo(_dot_hi_lo(x, gather_ref[...]), spread_ref[...])


def _rwkvprep_kernel(r_ref, k_ref, v_ref, lo_ref, rp_ref, kp_ref, vp_ref, lop_ref,
                     mu_ref, w0_ref, wup_ref, a0_ref, aup_ref, gup_ref, kk_ref, ka_ref, rk_ref, hg_ref, hs_ref,
                     ro_ref, wo_ref, ko_ref, vo_ref, kko_ref, bo_ref, go_ref, bon_ref, *, tm):
    i = pl.program_id(0)
    has_prev = jnp.where((i * tm) % SEQ == 0, 0.0, 1.0).astype(F32)

    def shift(cur_ref, prev_ref, mu):
        z = cur_ref[...]
        zp = pltpu.roll(z, 1, axis=0)
        prev_row = prev_ref[7:8, :] * has_prev
        row = lax.broadcasted_iota(jnp.int32, z.shape, 0)
        zp = jnp.where(row == 0, prev_row, zp)
        return z + (zp - z) * mu

    r = shift(r_ref, rp_ref, mu_ref[:, 0:1024])
    k = shift(k_ref, kp_ref, mu_ref[:, 1024:2048])
    v = shift(v_ref, vp_ref, mu_ref[:, 2048:3072])
    lo = shift(lo_ref, lop_ref, mu_ref[:, 3072:3584])
    w_lo, a_lo, g_lo = lo[:, 0:128], lo[:, 128:256], lo[:, 256:512]

    u = w0_ref[...] + _dot(jnp.tanh(w_lo), wup_ref[...])
    decay = jnp.exp(-(_sigmoid(u) * float(np.exp(-0.5))))
    a = _sigmoid(a0_ref[...] + _dot(a_lo, aup_ref[...]))
    g = _dot(_sigmoid(g_lo), gup_ref[...])

    kk = k * kk_ref[...]
    ss = _head_sum(kk * kk, hg_ref, hs_ref)
    kk = kk * lax.rsqrt(jnp.maximum(ss, 1e-24))
    k = k * (1.0 + (a - 1.0) * ka_ref[...])
    coef = _head_sum(r * k * rk_ref[...], hg_ref, hs_ref)

    ro_ref[...] = r
    wo_ref[...] = decay
    ko_ref[...] = k
    vo_ref[...] = v
    kko_ref[...] = kk
    bo_ref[...] = kk * a
    go_ref[...] = g
    bon_ref[...] = coef * v


def _rwkvprep(proj, mu_p, w0, wup_p, a0, aup_p, gup, k_k, k_a, r_k, head_gather, head_spread):
    tm = 256
    pb = tm // 8
    cur = lambda c: pl.BlockSpec((tm, 1024), lambda i, c=c: (i, c))
    prev = lambda c: pl.BlockSpec((8, 1024), lambda i, c=c: (jnp.maximum(i * pb - 1, 0), c))
    full = lambda s: pl.BlockSpec(s, lambda i: (0,) * len(s))
    out = pl.BlockSpec((tm, 1024), lambda i: (i, 0))
    osh = jax.ShapeDtypeStruct((ROWS, RWKV_WIDTH), F32)
    return pl.pallas_call(
        functools.partial(_rwkvprep_kernel, tm=tm),
        grid=(ROWS // tm,),
        in_specs=[cur(COL_R // 1024), cur(COL_K // 1024), cur(COL_V // 1024),
                  pl.BlockSpec((tm, LORA_PACK), lambda i: (i, COL_LORA // LORA_PACK)),
                  prev(COL_R // 1024), prev(COL_K // 1024), prev(COL_V // 1024),
                  pl.BlockSpec((8, LORA_PACK), lambda i: (jnp.maximum(i * pb - 1, 0), COL_LORA // LORA_PACK)),
                  full((1, 3584)), full((1, 1024)), full((128, 1024)), full((1, 1024)), full((128, 1024)),
                  full((256, 1024)), full((1, 1024)), full((1, 1024)), full((1, 1024)),
                  full((RWKV_WIDTH, LANES)), full((LANES, RWKV_WIDTH))],
        out_specs=[out] * 8,
        out_shape=[osh] * 8,
        compiler_params=_params(("parallel",)),
        name="rwkvprep",
    )(proj, proj, proj, proj, proj, proj, proj, proj,
      mu_p, w0, wup_p, a0, aup_p, gup, k_k, k_a, r_k, head_gather, head_spread)


N_JPAIR = HEAD_DIM // 2
N_JOPS = 5


def _scan_kernel(kkn_ref, w_ref, b_ref, k_ref, r_ref, v_ref, y_ref, s_ref, sa_ref, d_ref, *, tc):
    @pl.when(pl.program_id(0) == 0)
    def _():
        s_ref[...] = jnp.zeros_like(s_ref)
        sa_ref[...] = jnp.zeros_like(sa_ref)

    lo_half = lax.broadcasted_iota(jnp.int32, (N_JPAIR, tc, LANES), 2) < 64
    for n, ref in enumerate((kkn_ref, w_ref, b_ref, k_ref, r_ref)):
        x = ref[...]
        xr = pltpu.roll(x, 64, axis=2)
        d_ref[n, 0:N_JPAIR] = jnp.where(lo_half, x, xr)
        d_ref[n, N_JPAIR:HEAD_DIM] = jnp.where(lo_half, xr, x)

    def tree(parts):
        while len(parts) > 1:
            parts = [parts[n] + parts[n + 1] for n in range(0, len(parts), 2)]
        return parts[0]

    def step(t, sa):
        vt = v_ref[t]
        acc_y = [None] * 4
        acc_s = [None] * 4
        for j in range(HEAD_DIM):
            row = lambda n: d_ref[n, j, pl.ds(t, 1), :]
            sn = s_ref[j] * row(1) - sa * row(2) + vt * row(3)
            s_ref[j] = sn
            ty = sn * row(4)
            ts = sn * row(0)
            acc_y[j % 4] = ty if acc_y[j % 4] is None else acc_y[j % 4] + ty
            acc_s[j % 4] = ts if acc_s[j % 4] is None else acc_s[j % 4] + ts
        y_ref[t] = tree(acc_y)
        return tree(acc_s)

    sa_ref[...] = lax.fori_loop(0, tc, step, sa_ref[...])


def _scan(kkn, w, b, k, r, v):
    tc = 64
    spec = pl.BlockSpec((tc, N_JPAIR, LANES), lambda i: (i, 0, 0))
    jspec = pl.BlockSpec((N_JPAIR, tc, LANES), lambda i: (0, i, 0))
    return pl.pallas_call(
        functools.partial(_scan_kernel, tc=tc),
        grid=(SEQ // tc,),
        in_specs=[jspec] * N_JOPS + [spec],
        out_specs=spec,
        out_shape=jax.ShapeDtypeStruct((SEQ, N_JPAIR, LANES), F32),
        scratch_shapes=[pltpu.VMEM((HEAD_DIM, N_JPAIR, LANES), F32),
                        pltpu.VMEM((N_JPAIR, LANES), F32),
                        pltpu.VMEM((N_JOPS, HEAD_DIM, tc, LANES), F32)],
        compiler_params=_params(("arbitrary",)),
        name="scan",
    )(kkn, w, b, k, r, v)


RELAYOUT_T = 128
_HALF_BATCH = [(half, b) for half in range(2) for b in range(BATCH)]


def _gather_lanes(z_ref, p):
    return jnp.concatenate([z_ref[b, pl.ds((half * N_JPAIR + p) * RWKV_HEADS, RWKV_HEADS), :]
                            for half, b in _HALF_BATCH], axis=0)


def _relayout_kernel(*refs, modes):
    n_in = len(modes) + modes.count("shift")
    in_refs, out_refs, z_ref = iter(refs[:n_in]), refs[n_in:-1], refs[-1]
    has_next = jnp.where(pl.program_id(0) == pl.num_programs(0) - 1, 0.0, 1.0).astype(F32)
    last_row = lax.broadcasted_iota(jnp.int32, (RELAYOUT_T, RWKV_WIDTH), 0) == RELAYOUT_T - 1
    for mode, o_ref in zip(modes, out_refs):
        x_ref = next(in_refs)
        nxt_ref = next(in_refs) if mode == "shift" else None
        for b in range(BATCH):
            x = x_ref[b]
            if mode == "shift":
                x = jnp.where(last_row, nxt_ref[b, 0:1, :] * has_next, pltpu.roll(x, RELAYOUT_T - 1, axis=0))
            z_ref[b] = x.T
        for p in range(N_JPAIR):
            tile = _gather_lanes(z_ref, p).T
            if mode == "time_major":
                o_ref[pl.ds(p, RELAYOUT_T, stride=N_JPAIR), :] = tile
            else:
                o_ref[p] = tile


def _rwkv_out_kernel(y_ref, bon_ref, g_ref, lw_ref, lb_ref, hg_ref, hs_ref, o_ref, z_ref):
    for p in range(N_JPAIR):
        tile = y_ref[pl.ds(p, RELAYOUT_T, stride=N_JPAIR), :].T
        for n, (half, b) in enumerate(_HALF_BATCH):
            z_ref[b, pl.ds((half * N_JPAIR + p) * RWKV_HEADS, RWKV_HEADS), :] = (
                tile[n * RWKV_HEADS:(n + 1) * RWKV_HEADS])
    for b in range(BATCH):
        y = z_ref[b].T
        mean = _head_sum(y, hg_ref, hs_ref) * (1.0 / HEAD_DIM)
        d = y - mean
        var = _head_sum(d * d, hg_ref, hs_ref) * (1.0 / HEAD_DIM)
        yn = d * lax.rsqrt(var + GN_EPS) * lw_ref[...] + lb_ref[...]
        o_ref[b] = (yn + bon_ref[b]) * g_ref[b]


_NATURAL_SPEC = pl.BlockSpec((BATCH, RELAYOUT_T, RWKV_WIDTH), lambda i: (0, i, 0))
_TIME_MAJOR_SPEC = pl.BlockSpec((RELAYOUT_T * N_JPAIR, LANES), lambda i: (i, 0))
_RELAYOUT_SCRATCH = [pltpu.VMEM((BATCH, RWKV_WIDTH, RELAYOUT_T), F32)]


def _relayout(arrays, modes):
    blocks_per_step = RELAYOUT_T // 8
    nxt_spec = pl.BlockSpec((BATCH, 8, RWKV_WIDTH),
                            lambda i: (0, jnp.minimum((i + 1) * blocks_per_step, SEQ // 8 - 1), 0))
    rows_spec = pl.BlockSpec((N_JPAIR, RELAYOUT_T, LANES), lambda i: (0, i, 0))
    operands, in_specs, out_specs, out_shapes = [], [], [], []
    for a, mode in zip(arrays, modes):
        a = a.reshape(BATCH, SEQ, RWKV_WIDTH)
        operands += [a, a] if mode == "shift" else [a]
        in_specs += [_NATURAL_SPEC, nxt_spec] if mode == "shift" else [_NATURAL_SPEC]
        out_specs.append(_TIME_MAJOR_SPEC if mode == "time_major" else rows_spec)
        out_shapes.append(jax.ShapeDtypeStruct(
            (SEQ * N_JPAIR, LANES) if mode == "time_major" else (N_JPAIR, SEQ, LANES), F32))
    outs = pl.pallas_call(
        functools.partial(_relayout_kernel, modes=tuple(modes)),
        grid=(SEQ // RELAYOUT_T,),
        in_specs=in_specs,
        out_specs=out_specs,
        out_shape=out_shapes,
        scratch_shapes=_RELAYOUT_SCRATCH,
        compiler_params=_params(("parallel",)),
        name="relayout",
    )(*operands)
    return [o.reshape(SEQ, N_JPAIR, LANES) if mode == "time_major" else o for o, mode in zip(outs, modes)]


def _rwkv_out(y, bonus, g, lnx_w, lnx_b, head_gather, head_spread):
    vec = pl.BlockSpec((1, RWKV_WIDTH), lambda i: (0, 0))
    nat = lambda a: a.reshape(BATCH, SEQ, RWKV_WIDTH)
    return pl.pallas_call(
        _rwkv_out_kernel,
        grid=(SEQ // RELAYOUT_T,),
        in_specs=[_TIME_MAJOR_SPEC, _NATURAL_SPEC, _NATURAL_SPEC, vec, vec,
                  pl.BlockSpec((RWKV_WIDTH, LANES), lambda i: (0, 0)),
                  pl.BlockSpec((LANES, RWKV_WIDTH), lambda i: (0, 0))],
        out_specs=_NATURAL_SPEC,
        out_shape=jax.ShapeDtypeStruct((BATCH, SEQ, RWKV_WIDTH), F32),
        scratch_shapes=_RELAYOUT_SCRATCH,
        compiler_params=_params(("parallel",)),
        name="rwkvout",
    )(y.reshape(SEQ * N_JPAIR, LANES), nat(bonus), nat(g), lnx_w, lnx_b, head_gather, head_spread
      ).reshape(ROWS, RWKV_WIDTH)


KV_T = 512
KV_PACK = 2 * NSA_KV_WIDTH


def _kvprep_kernel(c_ref, s_ref, w_ref, craw_ref, ks_ref, vst_ref, kw_ref, vwt_ref):
    c = c_ref[...]
    for g in range(NSA_KV_HEADS):
        craw_ref[0, 0, g] = c[:, g * HEAD_DIM:(g + 1) * HEAD_DIM]
        craw_ref[1, 0, g] = c[:, NSA_KV_WIDTH + g * HEAD_DIM:NSA_KV_WIDTH + (g + 1) * HEAD_DIM]
    for src_ref, k_out, vt_out in ((s_ref, ks_ref, vst_ref), (w_ref, kw_ref, vwt_ref)):
        x = src_ref[...]
        vt = x[:, NSA_KV_WIDTH:].T
        for g in range(NSA_KV_HEADS):
            k_out[0, g] = x[:, g * HEAD_DIM:(g + 1) * HEAD_DIM].astype(BF16)
            vt_out[0, g] = vt[g * HEAD_DIM:(g + 1) * HEAD_DIM].astype(BF16)


def _kvprep(proj):
    nt = SEQ // KV_T
    G = NSA_KV_HEADS
    src = lambda col: pl.BlockSpec((KV_T, KV_PACK), lambda b, i, col=col: (b * nt + i, col // KV_PACK))
    natural = pl.BlockSpec((1, G, KV_T, HEAD_DIM), lambda b, i: (b, 0, i, 0))
    transposed = pl.BlockSpec((1, G, HEAD_DIM, KV_T), lambda b, i: (b, 0, 0, i))
    nat_shape = jax.ShapeDtypeStruct((BATCH, G, SEQ, HEAD_DIM), BF16)
    tr_shape = jax.ShapeDtypeStruct((BATCH, G, HEAD_DIM, SEQ), BF16)
    return pl.pallas_call(
        _kvprep_kernel,
        grid=(BATCH, nt),
        in_specs=[src(COL_KC), src(COL_KS), src(COL_KW)],
        out_specs=[pl.BlockSpec((2, 1, G, KV_T, HEAD_DIM), lambda b, i: (0, b, 0, i, 0)),
                   natural, transposed, natural, transposed],
        out_shape=[jax.ShapeDtypeStruct((2, BATCH, G, SEQ, HEAD_DIM), F32), nat_shape, tr_shape, nat_shape, tr_shape],
        compiler_params=_params(("parallel", "parallel")),
        name="kvprep",
    )(proj, proj, proj)


def _compress_kernel(kv_ref, w1_ref, pe_ref, w2_ref, o_ref):
    w1 = w1_ref[0]
    top = bot = None
    for l in range(CMP_STRIDE):
        rows = kv_ref[0, 0, 0, pl.ds(l, N_CMP_PAD, stride=CMP_STRIDE), :]
        t_l = _dot(rows, w1[l * HEAD_DIM:(l + 1) * HEAD_DIM])
        b_l = _dot(rows, w1[(CMP_STRIDE + l) * HEAD_DIM:(CMP_STRIDE + l + 1) * HEAD_DIM])
        top = t_l if top is None else top + t_l
        bot = b_l if bot is None else bot + b_l
    bias = _dot(jnp.broadcast_to(pe_ref[0], (8, CMP_BLOCK * HEAD_DIM)), w1)[0:1]
    hid = top + pltpu.roll(bot, N_CMP_PAD - 1, axis=0) + bias
    c0 = float(np.sqrt(2.0 / np.pi))
    act = 0.5 * hid * (1.0 + jnp.tanh(c0 * (hid + 0.044715 * (hid * hid * hid))))
    o_ref[0, 0] = _dot(act, w2_ref[0])


def _compress(craw, w1, pe, w2):
    G = NSA_KV_HEADS
    nbg = BATCH * G
    return pl.pallas_call(
        _compress_kernel,
        grid=(2, nbg),
        in_specs=[pl.BlockSpec((1, 1, 1, SEQ, HEAD_DIM), lambda c, n: (c, n // G, n % G, 0, 0)),
                  pl.BlockSpec((1, CMP_BLOCK * HEAD_DIM, CMP_HIDDEN), lambda c, n: (c, 0, 0)),
                  pl.BlockSpec((1, 1, CMP_BLOCK * HEAD_DIM), lambda c, n: (c, 0, 0)),
                  pl.BlockSpec((1, CMP_HIDDEN, HEAD_DIM), lambda c, n: (c, 0, 0))],
        out_specs=pl.BlockSpec((1, 1, N_CMP_PAD, HEAD_DIM), lambda c, n: (c, n, 0, 0)),
        out_shape=jax.ShapeDtypeStruct((2, nbg, N_CMP_PAD, HEAD_DIM), F32),
        compiler_params=_params(("parallel", "parallel")),
        name="compress",
    )(craw, w1, pe, w2)


MASK_BIG = float(2.0 ** 100)
KEY_CHUNK = 256
N_FEAT = 64


def _key_features():
    s = np.arange(SEQ)
    f = np.zeros((SEQ, N_FEAT), np.float32)
    f[s, s // SEL_BLOCK] = 1.0
    f[:, 32:35] = (SEL_BLOCK * (s // SEL_BLOCK))[:, None]
    f[:, 35:38] = (s % SEL_BLOCK)[:, None]
    return f


def _split3(x):
    a1 = x.astype(BF16)
    r1 = x - a1.astype(F32)
    a2 = r1.astype(BF16)
    a3 = (r1 - a2.astype(F32)).astype(BF16)
    return a1, a2, a3


def _nsa_kernel(q_ref, kc_ref, vc_ref, ks_ref, vst_ref, kw_ref, vwt_ref, gate_ref, ovt_ref, feat_ref,
                cmask_ref, wmask_ref, o_ref, gt_ref, slc_ref, *, tq):
    g = pl.program_id(1)
    t0 = pl.program_id(2) * tq
    hg = NSA_GROUP
    nq = hg * tq
    dot = functools.partial(jnp.dot, preferred_element_type=F32)

    qt = (q_ref[...] * HEAD_DIM ** -0.5).T
    qt = jnp.concatenate([qt[h * HEAD_DIM:(h + 1) * HEAD_DIM] for h in range(hg)], axis=1).astype(BF16)
    lane = lax.broadcasted_iota(jnp.int32, (1, nq), 1)
    slope = jnp.exp2(-0.5 * (g * hg + lane // tq + 1).astype(F32))
    qpos = t0 + lane % tq

    n_idx = lax.broadcasted_iota(jnp.int32, (N_CMP_PAD, 1), 0)
    dist_c = qpos.astype(F32) - (n_idx * CMP_STRIDE + (CMP_BLOCK - 1)).astype(F32)
    mask_c = (dist_c >= 0) & (n_idx < N_CMP)
    x_c = jnp.where(mask_c, dot(kc_ref[0, 0].astype(BF16), qt) - slope * dist_c, NEG_INF)
    p_c = jnp.where(mask_c, jnp.exp(x_c - jnp.max(x_c, axis=0, keepdims=True)), 0.0)
    den_c = jnp.maximum(jnp.sum(p_c, axis=0, keepdims=True), TINY)
    o_cmp = dot(vc_ref[0, 0].T.astype(BF16), p_c.astype(BF16)) / den_c

    p_n = p_c / den_c
    p_sum = (p_n[:, 0:tq] + p_n[:, tq:2 * tq]) + (p_n[:, 2 * tq:3 * tq] + p_n[:, 3 * tq:4 * tq])
    ovt = ovt_ref[...]
    imp = sum(dot(ovt, part) for part in _split3(p_sum))
    blk = lax.broadcasted_iota(jnp.int32, (N_SEL, tq), 0)
    cur = (t0 + lax.broadcasted_iota(jnp.int32, (N_SEL, tq), 1)) // SEL_BLOCK
    forced = (blk == 0) | (blk == cur) | (blk == cur - 1)
    score = jnp.where(blk > cur, NEG_INF, jnp.where(forced, -NEG_INF, imp))
    rank = jnp.zeros((N_SEL, tq), F32)
    for jp in range(N_SEL):
        row = score[jp:jp + 1, :]
        beats = (row > score) | ((row == score) & (blk > jp))
        rank = rank + jnp.where(beats, 1.0, 0.0)
    sel_bias = jnp.where((rank < SEL_TOPN) & (blk <= cur), 0.0, -MASK_BIG)
    sel_bias = jnp.concatenate([sel_bias] * hg, axis=1)

    s1, s2, s3 = (piece.astype(F32) for piece in _split3(slope))
    r8 = lax.broadcasted_iota(jnp.int32, (8, nq), 0)
    srow = jnp.where((r8 == 0) | (r8 == 3), s1, jnp.where((r8 == 1) | (r8 == 4), s2, s3))
    srow = jnp.where(r8 < 6, srow, 0.0)
    pad = jnp.zeros((N_FEAT - N_SEL - 8, nq), F32)
    q_slc = jnp.concatenate([qt, jnp.concatenate([sel_bias, srow, pad], axis=0).astype(BF16)], axis=0)
    q_win = jnp.concatenate([qt, jnp.concatenate([jnp.zeros_like(sel_bias), srow, pad], axis=0).astype(BF16)],
                            axis=0)

    def scores(k_ref, start, size, q_aug):
        keys = jnp.concatenate([k_ref[0, pl.ds(start, size), :], feat_ref[pl.ds(start, size), :]], axis=1)
        return dot(keys, q_aug)

    def attend(s, v_t):
        p = jnp.exp(s - jnp.max(s, axis=0, keepdims=True))
        return dot(v_t, p.astype(BF16)) / jnp.maximum(jnp.sum(p, axis=0, keepdims=True), TINY)

    def slc_variant(n_chunks):
        size = n_chunks * KEY_CHUNK
        s = scores(ks_ref, 0, size, q_slc)
        last = s[size - KEY_CHUNK:] + cmask_ref[0]
        s = last if n_chunks == 1 else jnp.concatenate([s[:size - KEY_CHUNK], last], axis=0)
        slc_ref[...] = attend(s, vst_ref[0, :, 0:size])

    for n_chunks in range(1, SEQ // KEY_CHUNK + 1):
        pl.when(t0 // KEY_CHUNK == n_chunks - 1)(functools.partial(slc_variant, n_chunks))
    o_slc = slc_ref[...]

    span = WINDOW + tq
    start = pl.multiple_of(jnp.maximum(t0 - WINDOW, 0), LANES)
    o_win = attend(scores(kw_ref, start, span, q_win) + wmask_ref[0], vwt_ref[0, :, pl.ds(start, span)])

    gt_ref[...] = _sigmoid(gate_ref[...]).T
    out = jnp.zeros((HEAD_DIM, nq), F32)
    for c, o in enumerate((o_cmp, o_slc, o_win)):
        g8 = gt_ref[pl.ds(pl.multiple_of(c * NSA_HEADS + (g // 2) * 8, 8), 8), :]
        g4 = jnp.where(g % 2 == 0, g8[0:hg], g8[hg:2 * hg])
        out = out + jnp.concatenate([g4[h:h + 1, :] for h in range(hg)], axis=1) * o
    o_ref[...] = jnp.concatenate([out[:, h * tq:(h + 1) * tq].T for h in range(hg)], axis=1)


def _nsa_masks(tq):
    qq = np.arange(NSA_GROUP * tq)[None, :] % tq
    k_c = np.arange(KEY_CHUNK)[:, None]
    causal = np.stack([k_c <= o * tq + qq for o in range(KEY_CHUNK // tq)])
    k_w = np.arange(WINDOW + tq)[:, None]
    dist = [min(o * tq, WINDOW) + qq - k_w for o in range(WINDOW // tq + 1)]
    window = np.stack([(d >= 0) & (d < WINDOW) for d in dist])
    to_bias = lambda ok: jnp.where(jnp.asarray(ok), 0.0, -MASK_BIG).astype(F32)
    return to_bias(causal), to_bias(window)


def _nsa(proj, kcvc, ks, vst, kw, vwt, overlap_t, feat):
    tq = 256
    nq = SEQ // tq
    cmask, wmask = _nsa_masks(tq)
    G = NSA_KV_HEADS
    kv_n = pl.BlockSpec((1, SEQ, HEAD_DIM), lambda b, g, i: (b * G + g, 0, 0))
    kv_t = pl.BlockSpec((1, HEAD_DIM, SEQ), lambda b, g, i: (b * G + g, 0, 0))
    cmp_spec = lambda c: pl.BlockSpec((1, 1, N_CMP_PAD, HEAD_DIM), lambda b, g, i, c=c: (c, b * G + g, 0, 0))
    return pl.pallas_call(
        functools.partial(_nsa_kernel, tq=tq),
        grid=(BATCH, G, nq),
        in_specs=[pl.BlockSpec((tq, NSA_GROUP * HEAD_DIM), lambda b, g, i: (b * nq + i, COL_Q // 256 + g)),
                  cmp_spec(0), cmp_spec(1), kv_n, kv_t, kv_n, kv_t,
                  pl.BlockSpec((tq, LANES), lambda b, g, i: (b * nq + i, COL_NG // LANES)),
                  pl.BlockSpec((N_SEL, N_CMP_PAD), lambda b, g, i: (0, 0)),
                  pl.BlockSpec((SEQ, N_FEAT), lambda b, g, i: (0, 0)),
                  pl.BlockSpec((1,) + cmask.shape[1:], lambda b, g, i: (i % (KEY_CHUNK // tq), 0, 0)),
                  pl.BlockSpec((1,) + wmask.shape[1:], lambda b, g, i: (jnp.minimum(i, WINDOW // tq), 0, 0))],
        out_specs=pl.BlockSpec((tq, NSA_GROUP * HEAD_DIM), lambda b, g, i: (b * nq + i, g)),
        out_shape=jax.ShapeDtypeStruct((ROWS, NSA_Q_WIDTH), F32),
        scratch_shapes=[pltpu.VMEM((LANES, tq), F32), pltpu.VMEM((HEAD_DIM, NSA_GROUP * tq), F32)],
        compiler_params=_params(("parallel", "parallel", "parallel")),
        name="nsa",
    )(proj, kcvc, kcvc, ks, vst, kw, vwt, proj, overlap_t, feat, cmask, wmask)


def _mix_kernel(x_ref, ya_ref, yb_ref, ga_ref, gb_ref, wa_ref, wb_ref, wo_ref, h_ref):
    ma = _dot(ya_ref[...], wa_ref[...])
    mb = _dot(yb_ref[...], wb_ref[...])
    mixed = _sigmoid(ga_ref[...]) * ma + _sigmoid(gb_ref[...]) * mb
    h_ref[...] = x_ref[...] + _dot(mixed, wo_ref[...])


def _mix(x2, ya, yb, proj, wa, wb, wo):
    tm = 256
    row = lambda w, c=0: pl.BlockSpec((tm, w), lambda i, c=c: (i, c))
    const = lambda s: pl.BlockSpec(s, lambda i: (0, 0), pipeline_mode=pl.Buffered(1))
    return pl.pallas_call(
        _mix_kernel,
        grid=(ROWS // tm,),
        in_specs=[row(D_MODEL), row(1024), row(1024), row(D_MODEL, COL_GA // D_MODEL), row(D_MODEL, COL_GB // D_MODEL),
                  const((1024, D_MODEL)), const((1024, D_MODEL)), const((D_MODEL, D_MODEL))],
        out_specs=row(D_MODEL),
        out_shape=jax.ShapeDtypeStruct((ROWS, D_MODEL), F32),
        compiler_params=_params(("parallel",)),
        name="mix",
    )(x2, ya, yb, proj, proj, wa, wb, wo)


def _mlp_kernel(h_ref, gm_ref, wu_ref, wd_ref, gf_ref, o_ref, hn_ref, acc_ref):
    f = pl.program_id(1)

    @pl.when(f == 0)
    def _():
        h = h_ref[...]
        ms = jnp.mean(h * h, axis=-1, keepdims=True)
        hn_ref[...] = (h * lax.rsqrt(ms + NORM_EPS) * gm_ref[...]).astype(BF16)
        acc_ref[...] = jnp.zeros_like(acc_ref)

    u = jnp.maximum(jnp.dot(hn_ref[...], wu_ref[...], preferred_element_type=F32), 0.0)
    acc_ref[...] += jnp.dot((u * u).astype(BF16), wd_ref[...], preferred_element_type=F32)

    @pl.when(f == pl.num_programs(1) - 1)
    def _():
        h2 = h_ref[...] + acc_ref[...]
        ms = jnp.mean(h2 * h2, axis=-1, keepdims=True)
        o_ref[...] = h2 * lax.rsqrt(ms + NORM_EPS) * gf_ref[...]


def _mlp(h, g_mlp, w_up, w_down, g_final):
    tm, tf = 512, 1024
    return pl.pallas_call(
        _mlp_kernel,
        grid=(ROWS // tm, D_FF // tf),
        in_specs=[pl.BlockSpec((tm, D_MODEL), lambda i, f: (i, 0)),
                  pl.BlockSpec((1, D_MODEL), lambda i, f: (0, 0)),
                  pl.BlockSpec((D_MODEL, tf), lambda i, f: (0, f)),
                  pl.BlockSpec((tf, D_MODEL), lambda i, f: (f, 0)),
                  pl.BlockSpec((1, D_MODEL), lambda i, f: (0, 0))],
        out_specs=pl.BlockSpec((tm, D_MODEL), lambda i, f: (i, 0)),
        out_shape=jax.ShapeDtypeStruct((ROWS, D_MODEL), F32),
        scratch_shapes=[pltpu.VMEM((tm, D_MODEL), BF16), pltpu.VMEM((tm, D_MODEL), F32)],
        compiler_params=_params(("parallel", "arbitrary")),
        name="mlp",
    )(h, g_mlp, w_up, w_down, g_final)


def _interleave_heads(a, axis=-1):
    a = jnp.moveaxis(a, axis, -1)
    a = a.reshape(a.shape[:-1] + (RWKV_HEADS, HEAD_DIM)).swapaxes(-1, -2).reshape(a.shape)
    return jnp.moveaxis(a, -1, axis)


def _overlap_matrix():
    cs = np.arange(N_CMP)[:, None] * CMP_STRIDE
    ss = np.arange(N_SEL)[None, :] * SEL_BLOCK
    ov = np.clip(np.minimum(cs + CMP_BLOCK, ss + SEL_BLOCK) - np.maximum(cs, ss), 0, None) / CMP_BLOCK
    out = np.zeros((N_CMP_PAD, N_SEL), np.float32)
    out[:N_CMP] = ov
    return out


def kernel(x, norm_mix, w_in, rwkv_mu, rwkv_w0, rwkv_w_up, rwkv_a0, rwkv_a_up, rwkv_g_up, rwkv_k_k, rwkv_k_a,
           rwkv_r_k, rwkv_lnx_w, rwkv_lnx_b, cmp_pe_k, cmp_w1_k, cmp_w2_k, cmp_pe_v, cmp_w1_v, cmp_w2_v,
           w_out_rwkv, w_out_nsa, w_o, norm_mlp, mlp_w_up, mlp_w_down, norm_final):
    assert x.shape == (BATCH, SEQ, D_MODEL) and w_in.shape[0] == 1
    l = 0
    G = NSA_KV_HEADS
    x2 = x.reshape(ROWS, D_MODEL)
    row = lambda a: a.reshape(1, -1)

    w_p = _pack_weight(w_in[l].T)
    proj = _inproj(x2, row(norm_mix[l]), w_p)

    il = _interleave_heads
    mu = rwkv_mu[l]
    mu_p = jnp.concatenate([il(mu[0:3072].reshape(3, RWKV_WIDTH)).reshape(-1), mu[3072:3168], jnp.zeros((32,), F32),
                            mu[3168:3264], jnp.zeros((32,), F32), mu[3264:3520]])
    zrows = jnp.zeros((32, RWKV_WIDTH), F32)
    head_of_lane = np.arange(RWKV_WIDTH)[:, None] % RWKV_HEADS == np.arange(LANES)[None, :]
    head_gather = jnp.asarray(head_of_lane, BF16)
    head_spread = jnp.asarray(head_of_lane.T, BF16)
    r, w, k, v, kk, b, g, bonus = _rwkvprep(
        proj, row(mu_p), row(il(rwkv_w0[l])), jnp.concatenate([il(rwkv_w_up[l]), zrows]),
        row(il(rwkv_a0[l])), jnp.concatenate([il(rwkv_a_up[l]), zrows]), il(rwkv_g_up[l]),
        row(il(rwkv_k_k[l].reshape(-1))), row(il(rwkv_k_a[l].reshape(-1))), row(il(rwkv_r_k[l].reshape(-1))),
        head_gather, head_spread)
    kkn_s, w_s, b_s = _relayout([kk, w, b], ("shift", "rows", "rows"))
    k_s, r_s, v_s = _relayout([k, r, v], ("rows", "rows", "time_major"))
    y = _scan(kkn_s, w_s, b_s, k_s, r_s, v_s)
    ya = _rwkv_out(y, bonus, g, row(il(rwkv_lnx_w[l])), row(il(rwkv_lnx_b[l])), head_gather, head_spread)

    craw, ks, vst, kw, vwt = _kvprep(proj)
    kcvc = _compress(craw, jnp.stack([cmp_w1_k[l], cmp_w1_v[l]]),
                     jnp.stack([cmp_pe_k[l].reshape(1, -1), cmp_pe_v[l].reshape(1, -1)]),
                     jnp.stack([cmp_w2_k[l], cmp_w2_v[l]]))
    per_group = lambda a: a.reshape((BATCH * G,) + a.shape[2:])
    yb = _nsa(proj, kcvc, per_group(ks), per_group(vst), per_group(kw), per_group(vwt),
              jnp.asarray(_overlap_matrix().T, BF16), jnp.asarray(_key_features(), BF16))

    h = _mix(x2, ya, yb, proj, il(w_out_rwkv[l], axis=0).astype(BF16), w_out_nsa[l].astype(BF16), w_o[l].astype(BF16))
    out = _mlp(h, row(norm_mlp[l]), mlp_w_up[l].astype(BF16), mlp_w_down[l].astype(BF16), row(norm_final))
    return out.reshape(BATCH, SEQ, D_MODEL)
```

```python
import functools

import numpy as np
import jax
import jax.numpy as jnp
from jax import lax
from jax.experimental import pallas as pl
from jax.experimental.pallas import tpu as pltpu

F32 = jnp.float32
BF16 = jnp.bfloat16

D_MODEL = 2048
BATCH = 4
SEQ = 2048
ROWS = BATCH * SEQ
RWKV_HEADS = 16
HEAD_DIM = 64
RWKV_WIDTH = RWKV_HEADS * HEAD_DIM
LORA_DECAY = 96
LORA_ICLR = 96
LORA_GATE = 256
GN_EPS = 64e-5
NSA_HEADS = 16
NSA_KV_HEADS = 4
NSA_GROUP = NSA_HEADS // NSA_KV_HEADS
NSA_Q_WIDTH = NSA_HEADS * HEAD_DIM
NSA_KV_WIDTH = NSA_KV_HEADS * HEAD_DIM
CMP_BLOCK = 32
CMP_STRIDE = 16
CMP_HIDDEN = 256
N_CMP = (SEQ - CMP_BLOCK) // CMP_STRIDE + 1
N_CMP_PAD = 128
SEL_BLOCK = 64
SEL_TOPN = 16
N_SEL = SEQ // SEL_BLOCK
WINDOW = 512
D_FF = 4 * D_MODEL
NORM_EPS = 1e-5
NEG_INF = -1e30
TINY = 1e-30

LANES = 128
VMEM_LIMIT = 56 * 1024 * 1024

COL_GA = 0
COL_GB = 2048
COL_R = 4096
COL_K = 5120
COL_V = 6144
COL_LORA = 7168
COL_Q = 7680
COL_KC = 8704
COL_VC = 8960
COL_KS = 9216
COL_VS = 9472
COL_KW = 9728
COL_VW = 9984
COL_NG = 10240
PROJ_COLS = 10752
LORA_PACK = 512


def _dot(a, b):
    return jnp.dot(a.astype(BF16), b.astype(BF16), preferred_element_type=F32)


def _sigmoid(x):
    return 1.0 / (1.0 + jnp.exp(-x))


def _params(sem, limit=VMEM_LIMIT):
    return pltpu.CompilerParams(dimension_semantics=sem, vmem_limit_bytes=limit)


def _inproj_kernel(x_ref, g_ref, w_ref, o_ref, xn_ref):
    @pl.when(pl.program_id(1) == 0)
    def _():
        x = x_ref[...]
        ms = jnp.mean(x * x, axis=-1, keepdims=True)
        xn_ref[...] = (x * lax.rsqrt(ms + NORM_EPS) * g_ref[...]).astype(BF16)

    o_ref[...] = jnp.dot(xn_ref[...], w_ref[...], preferred_element_type=F32)


def _pack_kernel(wt_ref, o_ref):
    rows = lambda a, b: wt_ref[a:b, :]
    pad = lambda n: jnp.zeros((n, wt_ref.shape[1]), F32)
    rkv = [wt_ref[pl.ds(blk * RWKV_WIDTH + q, RWKV_HEADS, stride=HEAD_DIM), :]
           for blk in range(3) for q in range(HEAD_DIM)]
    packed = jnp.concatenate(
        [rows(6128, 10224)]
        + rkv
        + [rows(3072, 3168), pad(32),
           rows(3168, 3264), pad(32),
           rows(3264, 3520),
           rows(3520, 6080),
           rows(6080, 6128), pad(464)], axis=0)
    o_ref[...] = packed.T.astype(BF16)


def _pack_weight(w_t):
    tk = 128
    return pl.pallas_call(
        _pack_kernel,
        grid=(D_MODEL // tk,),
        in_specs=[pl.BlockSpec((w_t.shape[0], tk), lambda i: (0, i))],
        out_specs=pl.BlockSpec((tk, PROJ_COLS), lambda i: (i, 0)),
        out_shape=jax.ShapeDtypeStruct((D_MODEL, PROJ_COLS), BF16),
        compiler_params=_params(("parallel",)),
        name="packw",
    )(w_t)


def _inproj(x2, g, w_p):
    tm, tn = 1024, 1536
    return pl.pallas_call(
        _inproj_kernel,
        grid=(ROWS // tm, PROJ_COLS // tn),
        in_specs=[pl.BlockSpec((tm, D_MODEL), lambda i, j: (i, 0)),
                  pl.BlockSpec((1, D_MODEL), lambda i, j: (0, 0)),
                  pl.BlockSpec((D_MODEL, tn), lambda i, j: (0, j))],
        out_specs=pl.BlockSpec((tm, tn), lambda i, j: (i, j)),
        out_shape=jax.ShapeDtypeStruct((ROWS, PROJ_COLS), F32),
        scratch_shapes=[pltpu.VMEM((tm, D_MODEL), BF16)],
        compiler_params=_params(("parallel", "arbitrary")),
        name="inproj",
    )(x2, g, w_p)


def _dot_hi_lo(x, m):
    hi = x.astype(BF16)
    lo = (x - hi.astype(F32)).astype(BF16)
    return jnp.dot(hi, m, preferred_element_type=F32) + jnp.dot(lo, m, preferred_element_type=F32)


def _head_sum(x, gather_ref, spread_ref):
    return _dot_hi_lo(_dot_hi_lo(x, gather_ref[...]), spread_ref[...])


def _rwkvprep_kernel(r_ref, k_ref, v_ref, lo_ref, rp_ref, kp_ref, vp_ref, lop_ref,
                     mu_ref, w0_ref, wup_ref, a0_ref, aup_ref, gup_ref, kk_ref, ka_ref, rk_ref, hg_ref, hs_ref,
                     ro_ref, wo_ref, ko_ref, vo_ref, kko_ref, bo_ref, go_ref, bon_ref, *, tm):
    i = pl.program_id(0)
    has_prev = jnp.where((i * tm) % SEQ == 0, 0.0, 1.0).astype(F32)

    def shift(cur_ref, prev_ref, mu):
        z = cur_ref[...]
        zp = pltpu.roll(z, 1, axis=0)
        prev_row = prev_ref[7:8, :] * has_prev
        row = lax.broadcasted_iota(jnp.int32, z.shape, 0)
        zp = jnp.where(row == 0, prev_row, zp)
        return z + (zp - z) * mu

    r = shift(r_ref, rp_ref, mu_ref[:, 0:1024])
    k = shift(k_ref, kp_ref, mu_ref[:, 1024:2048])
    v = shift(v_ref, vp_ref, mu_ref[:, 2048:3072])
    lo = shift(lo_ref, lop_ref, mu_ref[:, 3072:3584])
    w_lo, a_lo, g_lo = lo[:, 0:128], lo[:, 128:256], lo[:, 256:512]

    u = w0_ref[...] + _dot(jnp.tanh(w_lo), wup_ref[...])
    decay = jnp.exp(-(_sigmoid(u) * float(np.exp(-0.5))))
    a = _sigmoid(a0_ref[...] + _dot(a_lo, aup_ref[...]))
    g = _dot(_sigmoid(g_lo), gup_ref[...])

    kk = k * kk_ref[...]
    ss = _head_sum(kk * kk, hg_ref, hs_ref)
    kk = kk * lax.rsqrt(jnp.maximum(ss, 1e-24))
    k = k * (1.0 + (a - 1.0) * ka_ref[...])
    coef = _head_sum(r * k * rk_ref[...], hg_ref, hs_ref)

    ro_ref[...] = r
    wo_ref[...] = decay
    ko_ref[...] = k
    vo_ref[...] = v
    kko_ref[...] = kk
    bo_ref[...] = kk * a
    go_ref[...] = g
    bon_ref[...] = coef * v


def _rwkvprep(proj, mu_p, w0, wup_p, a0, aup_p, gup, k_k, k_a, r_k, head_gather, head_spread):
    tm = 256
    pb = tm // 8
    cur = lambda c: pl.BlockSpec((tm, 1024), lambda i, c=c: (i, c))
    prev = lambda c: pl.BlockSpec((8, 1024), lambda i, c=c: (jnp.maximum(i * pb - 1, 0), c))
    full = lambda s: pl.BlockSpec(s, lambda i: (0,) * len(s))
    out = pl.BlockSpec((tm, 1024), lambda i: (i, 0))
    osh = jax.ShapeDtypeStruct((ROWS, RWKV_WIDTH), F32)
    return pl.pallas_call(
        functools.partial(_rwkvprep_kernel, tm=tm),
        grid=(ROWS // tm,),
        in_specs=[cur(COL_R // 1024), cur(COL_K // 1024), cur(COL_V // 1024),
                  pl.BlockSpec((tm, LORA_PACK), lambda i: (i, COL_LORA // LORA_PACK)),
                  prev(COL_R // 1024), prev(COL_K // 1024), prev(COL_V // 1024),
                  pl.BlockSpec((8, LORA_PACK), lambda i: (jnp.maximum(i * pb - 1, 0), COL_LORA // LORA_PACK)),
                  full((1, 3584)), full((1, 1024)), full((128, 1024)), full((1, 1024)), full((128, 1024)),
                  full((256, 1024)), full((1, 1024)), full((1, 1024)), full((1, 1024)),
                  full((RWKV_WIDTH, LANES)), full((LANES, RWKV_WIDTH))],
        out_specs=[out] * 8,
        out_shape=[osh] * 8,
        compiler_params=_params(("parallel",)),
        name="rwkvprep",
    )(proj, proj, proj, proj, proj, proj, proj, proj,
      mu_p, w0, wup_p, a0, aup_p, gup, k_k, k_a, r_k, head_gather, head_spread)


N_JPAIR = HEAD_DIM // 2
N_JOPS = 5


def _scan_kernel(kkn_ref, w_ref, b_ref, k_ref, r_ref, v_ref, y_ref, s_ref, sa_ref, d_ref, *, tc):
    @pl.when(pl.program_id(0) == 0)
    def _():
        s_ref[...] = jnp.zeros_like(s_ref)
        sa_ref[...] = jnp.zeros_like(sa_ref)

    lo_half = lax.broadcasted_iota(jnp.int32, (N_JPAIR, tc, LANES), 2) < 64
    for n, ref in enumerate((kkn_ref, w_ref, b_ref, k_ref, r_ref)):
        x = ref[...]
        xr = pltpu.roll(x, 64, axis=2)
        d_ref[n, 0:N_JPAIR, 0:tc, :] = jnp.where(lo_half, x, xr)
        d_ref[n, N_JPAIR:HEAD_DIM, 0:tc, :] = jnp.where(lo_half, xr, x)

    def tree(parts):
        while len(parts) > 1:
            parts = [parts[n] + parts[n + 1] for n in range(0, len(parts), 2)]
        return parts[0]

    def step(t, sa):
        vt = v_ref[t]
        acc_y = [None] * 4
        acc_s = [None] * 4
        for j in range(HEAD_DIM):
            row = lambda n: d_ref[n, j, pl.ds(t, 1), :]
            sn = s_ref[j] * row(1) - sa * row(2) + vt * row(3)
            s_ref[j] = sn
            ty = sn * row(4)
            ts = sn * row(0)
            acc_y[j % 4] = ty if acc_y[j % 4] is None else acc_y[j % 4] + ty
            acc_s[j % 4] = ts if acc_s[j % 4] is None else acc_s[j % 4] + ts
        y_ref[t] = tree(acc_y)
        return tree(acc_s)

    sa_ref[...] = lax.fori_loop(0, tc, step, sa_ref[...])


def _scan(kkn, w, b, k, r, v):
    tc = 64
    spec = pl.BlockSpec((tc, N_JPAIR, LANES), lambda i: (i, 0, 0))
    jspec = pl.BlockSpec((N_JPAIR, tc, LANES), lambda i: (0, i, 0))
    return pl.pallas_call(
        functools.partial(_scan_kernel, tc=tc),
        grid=(SEQ // tc,),
        in_specs=[jspec] * N_JOPS + [spec],
        out_specs=spec,
        out_shape=jax.ShapeDtypeStruct((SEQ, N_JPAIR, LANES), F32),
        scratch_shapes=[pltpu.VMEM((HEAD_DIM, N_JPAIR, LANES), F32),
                        pltpu.VMEM((N_JPAIR, LANES), F32),
                        pltpu.VMEM((N_JOPS, HEAD_DIM + 1, tc + 8, LANES), F32)],
        compiler_params=_params(("arbitrary",)),
        name="scan",
    )(kkn, w, b, k, r, v)


RELAYOUT_T = 128
_HALF_BATCH = [(half, b) for half in range(2) for b in range(BATCH)]


def _gather_lanes(z_ref, p):
    return jnp.concatenate([z_ref[b, pl.ds((half * N_JPAIR + p) * RWKV_HEADS, RWKV_HEADS), :]
                            for half, b in _HALF_BATCH], axis=0)


def _relayout_kernel(*refs, modes):
    n_in = len(modes) + modes.count("shift")
    in_refs, out_refs, z_ref = iter(refs[:n_in]), refs[n_in:-1], refs[-1]
    has_next = jnp.where(pl.program_id(0) == pl.num_programs(0) - 1, 0.0, 1.0).astype(F32)
    last_row = lax.broadcasted_iota(jnp.int32, (RELAYOUT_T, RWKV_WIDTH), 0) == RELAYOUT_T - 1
    for mode, o_ref in zip(modes, out_refs):
        x_ref = next(in_refs)
        nxt_ref = next(in_refs) if mode == "shift" else None
        for b in range(BATCH):
            x = x_ref[b]
            if mode == "shift":
                x = jnp.where(last_row, nxt_ref[b, 0:1, :] * has_next, pltpu.roll(x, RELAYOUT_T - 1, axis=0))
            z_ref[b] = x.T
        for p in range(N_JPAIR):
            tile = _gather_lanes(z_ref, p).T
            if mode == "time_major":
                o_ref[pl.ds(p, RELAYOUT_T, stride=N_JPAIR), :] = tile
            else:
                o_ref[p] = tile


def _rwkv_out_kernel(y_ref, bon_ref, g_ref, lw_ref, lb_ref, hg_ref, hs_ref, o_ref, z_ref):
    for p in range(N_JPAIR):
        tile = y_ref[pl.ds(p, RELAYOUT_T, stride=N_JPAIR), :].T
        for n, (half, b) in enumerate(_HALF_BATCH):
            z_ref[b, pl.ds((half * N_JPAIR + p) * RWKV_HEADS, RWKV_HEADS), :] = (
                tile[n * RWKV_HEADS:(n + 1) * RWKV_HEADS])
    for b in range(BATCH):
        y = z_ref[b].T
        mean = _head_sum(y, hg_ref, hs_ref) * (1.0 / HEAD_DIM)
        d = y - mean
        var = _head_sum(d * d, hg_ref, hs_ref) * (1.0 / HEAD_DIM)
        yn = d * lax.rsqrt(var + GN_EPS) * lw_ref[...] + lb_ref[...]
        o_ref[b] = (yn + bon_ref[b]) * g_ref[b]


_NATURAL_SPEC = pl.BlockSpec((BATCH, RELAYOUT_T, RWKV_WIDTH), lambda i: (0, i, 0))
_TIME_MAJOR_SPEC = pl.BlockSpec((RELAYOUT_T * N_JPAIR, LANES), lambda i: (i, 0))
_RELAYOUT_SCRATCH = [pltpu.VMEM((BATCH, RWKV_WIDTH, RELAYOUT_T), F32)]


def _relayout(arrays, modes):
    blocks_per_step = RELAYOUT_T // 8
    nxt_spec = pl.BlockSpec((BATCH, 8, RWKV_WIDTH),
                            lambda i: (0, jnp.minimum((i + 1) * blocks_per_step, SEQ // 8 - 1), 0))
    rows_spec = pl.BlockSpec((N_JPAIR, RELAYOUT_T, LANES), lambda i: (0, i, 0))
    operands, in_specs, out_specs, out_shapes = [], [], [], []
    for a, mode in zip(arrays, modes):
        a = a.reshape(BATCH, SEQ, RWKV_WIDTH)
        operands += [a, a] if mode == "shift" else [a]
        in_specs += [_NATURAL_SPEC, nxt_spec] if mode == "shift" else [_NATURAL_SPEC]
        out_specs.append(_TIME_MAJOR_SPEC if mode == "time_major" else rows_spec)
        out_shapes.append(jax.ShapeDtypeStruct(
            (SEQ * N_JPAIR, LANES) if mode == "time_major" else (N_JPAIR, SEQ, LANES), F32))
    outs = pl.pallas_call(
        functools.partial(_relayout_kernel, modes=tuple(modes)),
        grid=(SEQ // RELAYOUT_T,),
        in_specs=in_specs,
        out_specs=out_specs,
        out_shape=out_shapes,
        scratch_shapes=_RELAYOUT_SCRATCH,
        compiler_params=_params(("parallel",)),
        name="relayout",
    )(*operands)
    return [o.reshape(SEQ, N_JPAIR, LANES) if mode == "time_major" else o for o, mode in zip(outs, modes)]


def _rwkv_out(y, bonus, g, lnx_w, lnx_b, head_gather, head_spread):
    vec = pl.BlockSpec((1, RWKV_WIDTH), lambda i: (0, 0))
    nat = lambda a: a.reshape(BATCH, SEQ, RWKV_WIDTH)
    return pl.pallas_call(
        _rwkv_out_kernel,
        grid=(SEQ // RELAYOUT_T,),
        in_specs=[_TIME_MAJOR_SPEC, _NATURAL_SPEC, _NATURAL_SPEC, vec, vec,
                  pl.BlockSpec((RWKV_WIDTH, LANES), lambda i: (0, 0)),
                  pl.BlockSpec((LANES, RWKV_WIDTH), lambda i: (0, 0))],
        out_specs=_NATURAL_SPEC,
        out_shape=jax.ShapeDtypeStruct((BATCH, SEQ, RWKV_WIDTH), F32),
        scratch_shapes=_RELAYOUT_SCRATCH,
        compiler_params=_params(("parallel",)),
        name="rwkvout",
    )(y.reshape(SEQ * N_JPAIR, LANES), nat(bonus), nat(g), lnx_w, lnx_b, head_gather, head_spread
      ).reshape(ROWS, RWKV_WIDTH)


KV_T = 512
KV_PACK = 2 * NSA_KV_WIDTH


def _kvprep_kernel(c_ref, s_ref, w_ref, craw_ref, ks_ref, vst_ref, kw_ref, vwt_ref):
    c = c_ref[...]
    for g in range(NSA_KV_HEADS):
        craw_ref[0, 0, g] = c[:, g * HEAD_DIM:(g + 1) * HEAD_DIM]
        craw_ref[1, 0, g] = c[:, NSA_KV_WIDTH + g * HEAD_DIM:NSA_KV_WIDTH + (g + 1) * HEAD_DIM]
    for src_ref, k_out, vt_out in ((s_ref, ks_ref, vst_ref), (w_ref, kw_ref, vwt_ref)):
        x = src_ref[...]
        vt = x[:, NSA_KV_WIDTH:].T
        for g in range(NSA_KV_HEADS):
            k_out[0, g] = x[:, g * HEAD_DIM:(g + 1) * HEAD_DIM].astype(BF16)
            vt_out[0, g] = vt[g * HEAD_DIM:(g + 1) * HEAD_DIM].astype(BF16)


def _kvprep(proj):
    nt = SEQ // KV_T
    G = NSA_KV_HEADS
    src = lambda col: pl.BlockSpec((KV_T, KV_PACK), lambda b, i, col=col: (b * nt + i, col // KV_PACK))
    natural = pl.BlockSpec((1, G, KV_T, HEAD_DIM), lambda b, i: (b, 0, i, 0))
    transposed = pl.BlockSpec((1, G, HEAD_DIM, KV_T), lambda b, i: (b, 0, 0, i))
    nat_shape = jax.ShapeDtypeStruct((BATCH, G, SEQ, HEAD_DIM), BF16)
    tr_shape = jax.ShapeDtypeStruct((BATCH, G, HEAD_DIM, SEQ), BF16)
    return pl.pallas_call(
        _kvprep_kernel,
        grid=(BATCH, nt),
        in_specs=[src(COL_KC), src(COL_KS), src(COL_KW)],
        out_specs=[pl.BlockSpec((2, 1, G, KV_T, HEAD_DIM), lambda b, i: (0, b, 0, i, 0)),
                   natural, transposed, natural, transposed],
        out_shape=[jax.ShapeDtypeStruct((2, BATCH, G, SEQ, HEAD_DIM), F32), nat_shape, tr_shape, nat_shape, tr_shape],
        compiler_params=_params(("parallel", "parallel")),
        name="kvprep",
    )(proj, proj, proj)


def _compress_kernel(kv_ref, w1_ref, pe_ref, w2_ref, o_ref):
    w1 = w1_ref[0]
    top = bot = None
    for l in range(CMP_STRIDE):
        rows = kv_ref[0, 0, 0, pl.ds(l, N_CMP_PAD, stride=CMP_STRIDE), :]
        t_l = _dot(rows, w1[l * HEAD_DIM:(l + 1) * HEAD_DIM])
        b_l = _dot(rows, w1[(CMP_STRIDE + l) * HEAD_DIM:(CMP_STRIDE + l + 1) * HEAD_DIM])
        top = t_l if top is None else top + t_l
        bot = b_l if bot is None else bot + b_l
    bias = _dot(jnp.broadcast_to(pe_ref[0], (8, CMP_BLOCK * HEAD_DIM)), w1)[0:1]
    hid = top + pltpu.roll(bot, N_CMP_PAD - 1, axis=0) + bias
    c0 = float(np.sqrt(2.0 / np.pi))
    act = 0.5 * hid * (1.0 + jnp.tanh(c0 * (hid + 0.044715 * (hid * hid * hid))))
    o_ref[0, 0] = _dot(act, w2_ref[0])


def _compress(craw, w1, pe, w2):
    G = NSA_KV_HEADS
    nbg = BATCH * G
    return pl.pallas_call(
        _compress_kernel,
        grid=(2, nbg),
        in_specs=[pl.BlockSpec((1, 1, 1, SEQ, HEAD_DIM), lambda c, n: (c, n // G, n % G, 0, 0)),
                  pl.BlockSpec((1, CMP_BLOCK * HEAD_DIM, CMP_HIDDEN), lambda c, n: (c, 0, 0)),
                  pl.BlockSpec((1, 1, CMP_BLOCK * HEAD_DIM), lambda c, n: (c, 0, 0)),
                  pl.BlockSpec((1, CMP_HIDDEN, HEAD_DIM), lambda c, n: (c, 0, 0))],
        out_specs=pl.BlockSpec((1, 1, N_CMP_PAD, HEAD_DIM), lambda c, n: (c, n, 0, 0)),
        out_shape=jax.ShapeDtypeStruct((2, nbg, N_CMP_PAD, HEAD_DIM), F32),
        compiler_params=_params(("parallel", "parallel")),
        name="compress",
    )(craw, w1, pe, w2)


MASK_BIG = float(2.0 ** 100)
KEY_CHUNK = 256
N_FEAT = 64


def _key_features():
    s = np.arange(SEQ)
    f = np.zeros((SEQ, N_FEAT), np.float32)
    f[s, s // SEL_BLOCK] = 1.0
    f[:, 32:35] = (SEL_BLOCK * (s // SEL_BLOCK))[:, None]
    f[:, 35:38] = (s % SEL_BLOCK)[:, None]
    return f


def _split3(x):
    a1 = x.astype(BF16)
    r1 = x - a1.astype(F32)
    a2 = r1.astype(BF16)
    a3 = (r1 - a2.astype(F32)).astype(BF16)
    return a1, a2, a3


def _nsa_kernel(q_ref, kc_ref, vc_ref, ks_ref, vst_ref, kw_ref, vwt_ref, gate_ref, ovt_ref, feat_ref,
                cmask_ref, wmask_ref, o_ref, gt_ref, slc_ref, *, tq):
    g = pl.program_id(1)
    t0 = pl.program_id(2) * tq
    hg = NSA_GROUP
    nq = hg * tq
    dot = functools.partial(jnp.dot, preferred_element_type=F32)

    qt = (q_ref[...] * HEAD_DIM ** -0.5).T
    qt = jnp.concatenate([qt[h * HEAD_DIM:(h + 1) * HEAD_DIM] for h in range(hg)], axis=1).astype(BF16)
    lane = lax.broadcasted_iota(jnp.int32, (1, nq), 1)
    slope = jnp.exp2(-0.5 * (g * hg + lane // tq + 1).astype(F32))
    qpos = t0 + lane % tq

    n_idx = lax.broadcasted_iota(jnp.int32, (N_CMP_PAD, 1), 0)
    dist_c = qpos.astype(F32) - (n_idx * CMP_STRIDE + (CMP_BLOCK - 1)).astype(F32)
    mask_c = (dist_c >= 0) & (n_idx < N_CMP)
    x_c = jnp.where(mask_c, dot(kc_ref[0, 0].astype(BF16), qt) - slope * dist_c, NEG_INF)
    p_c = jnp.where(mask_c, jnp.exp(x_c - jnp.max(x_c, axis=0, keepdims=True)), 0.0)
    den_c = jnp.maximum(jnp.sum(p_c, axis=0, keepdims=True), TINY)
    o_cmp = dot(vc_ref[0, 0].T.astype(BF16), p_c.astype(BF16)) / den_c

    p_n = p_c / den_c
    p_sum = (p_n[:, 0:tq] + p_n[:, tq:2 * tq]) + (p_n[:, 2 * tq:3 * tq] + p_n[:, 3 * tq:4 * tq])
    ovt = ovt_ref[...]
    imp = sum(dot(ovt, part) for part in _split3(p_sum))
    blk = lax.broadcasted_iota(jnp.int32, (N_SEL, tq), 0)
    cur = (t0 + lax.broadcasted_iota(jnp.int32, (N_SEL, tq), 1)) // SEL_BLOCK
    forced = (blk == 0) | (blk == cur) | (blk == cur - 1)
    score = jnp.where(blk > cur, NEG_INF, jnp.where(forced, -NEG_INF, imp))
    rank = jnp.zeros((N_SEL, tq), F32)
    for jp in range(N_SEL):
        row = score[jp:jp + 1, :]
        beats = (row > score) | ((row == score) & (blk > jp))
        rank = rank + jnp.where(beats, 1.0, 0.0)
    sel_bias = jnp.where((rank < SEL_TOPN) & (blk <= cur), 0.0, -MASK_BIG)
    sel_bias = jnp.concatenate([sel_bias] * hg, axis=1)

    s1, s2, s3 = (piece.astype(F32) for piece in _split3(slope))
    r8 = lax.broadcasted_iota(jnp.int32, (8, nq), 0)
    srow = jnp.where((r8 == 0) | (r8 == 3), s1, jnp.where((r8 == 1) | (r8 == 4), s2, s3))
    srow = jnp.where(r8 < 6, srow, 0.0)
    pad = jnp.zeros((N_FEAT - N_SEL - 8, nq), F32)
    q_slc = jnp.concatenate([qt, jnp.concatenate([sel_bias, srow, pad], axis=0).astype(BF16)], axis=0)
    q_win = jnp.concatenate([qt, jnp.concatenate([jnp.zeros_like(sel_bias), srow, pad], axis=0).astype(BF16)],
                            axis=0)

    def scores(k_ref, start, size, q_aug):
        keys = jnp.concatenate([k_ref[0, pl.ds(start, size), :], feat_ref[pl.ds(start, size), :]], axis=1)
        return dot(keys, q_aug)

    def attend(s, v_t):
        p = jnp.exp(s - jnp.max(s, axis=0, keepdims=True))
        return dot(v_t, p.astype(BF16)) / jnp.maximum(jnp.sum(p, axis=0, keepdims=True), TINY)

    def slc_variant(n_chunks):
        size = n_chunks * KEY_CHUNK
        s = scores(ks_ref, 0, size, q_slc)
        last = s[size - KEY_CHUNK:] + cmask_ref[0]
        s = last if n_chunks == 1 else jnp.concatenate([s[:size - KEY_CHUNK], last], axis=0)
        slc_ref[...] = attend(s, vst_ref[0, :, 0:size])

    for n_chunks in range(1, SEQ // KEY_CHUNK + 1):
        pl.when(t0 // KEY_CHUNK == n_chunks - 1)(functools.partial(slc_variant, n_chunks))
    o_slc = slc_ref[...]

    span = WINDOW + tq
    start = pl.multiple_of(jnp.maximum(t0 - WINDOW, 0), LANES)
    o_win = attend(scores(kw_ref, start, span, q_win) + wmask_ref[0], vwt_ref[0, :, pl.ds(start, span)])

    gt_ref[...] = _sigmoid(gate_ref[...]).T
    out = jnp.zeros((HEAD_DIM, nq), F32)
    for c, o in enumerate((o_cmp, o_slc, o_win)):
        g8 = gt_ref[pl.ds(pl.multiple_of(c * NSA_HEADS + (g // 2) * 8, 8), 8), :]
        g4 = jnp.where(g % 2 == 0, g8[0:hg], g8[hg:2 * hg])
        out = out + jnp.concatenate([g4[h:h + 1, :] for h in range(hg)], axis=1) * o
    o_ref[...] = jnp.concatenate([out[:, h * tq:(h + 1) * tq].T for h in range(hg)], axis=1)


def _nsa_masks(tq):
    qq = np.arange(NSA_GROUP * tq)[None, :] % tq
    k_c = np.arange(KEY_CHUNK)[:, None]
    causal = np.stack([k_c <= o * tq + qq for o in range(KEY_CHUNK // tq)])
    k_w = np.arange(WINDOW + tq)[:, None]
    dist = [min(o * tq, WINDOW) + qq - k_w for o in range(WINDOW // tq + 1)]
    window = np.stack([(d >= 0) & (d < WINDOW) for d in dist])
    to_bias = lambda ok: jnp.where(jnp.asarray(ok), 0.0, -MASK_BIG).astype(F32)
    return to_bias(causal), to_bias(window)


def _nsa(proj, kcvc, ks, vst, kw, vwt, overlap_t, feat):
    tq = 256
    nq = SEQ // tq
    cmask, wmask = _nsa_masks(tq)
    G = NSA_KV_HEADS
    kv_n = pl.BlockSpec((1, SEQ, HEAD_DIM), lambda b, g, i: (b * G + g, 0, 0))
    kv_t = pl.BlockSpec((1, HEAD_DIM, SEQ), lambda b, g, i: (b * G + g, 0, 0))
    cmp_spec = lambda c: pl.BlockSpec((1, 1, N_CMP_PAD, HEAD_DIM), lambda b, g, i, c=c: (c, b * G + g, 0, 0))
    return pl.pallas_call(
        functools.partial(_nsa_kernel, tq=tq),
        grid=(BATCH, G, nq),
        in_specs=[pl.BlockSpec((tq, NSA_GROUP * HEAD_DIM), lambda b, g, i: (b * nq + i, COL_Q // 256 + g)),
                  cmp_spec(0), cmp_spec(1), kv_n, kv_t, kv_n, kv_t,
                  pl.BlockSpec((tq, LANES), lambda b, g, i: (b * nq + i, COL_NG // LANES)),
                  pl.BlockSpec((N_SEL, N_CMP_PAD), lambda b, g, i: (0, 0)),
                  pl.BlockSpec((SEQ, N_FEAT), lambda b, g, i: (0, 0)),
                  pl.BlockSpec((1,) + cmask.shape[1:], lambda b, g, i: (i % (KEY_CHUNK // tq), 0, 0)),
                  pl.BlockSpec((1,) + wmask.shape[1:], lambda b, g, i: (jnp.minimum(i, WINDOW // tq), 0, 0))],
        out_specs=pl.BlockSpec((tq, NSA_GROUP * HEAD_DIM), lambda b, g, i: (b * nq + i, g)),
        out_shape=jax.ShapeDtypeStruct((ROWS, NSA_Q_WIDTH), F32),
        scratch_shapes=[pltpu.VMEM((LANES, tq), F32), pltpu.VMEM((HEAD_DIM, NSA_GROUP * tq), F32)],
        compiler_params=_params(("parallel", "parallel", "parallel")),
        name="nsa",
    )(proj, kcvc, kcvc, ks, vst, kw, vwt, proj, overlap_t, feat, cmask, wmask)


def _mix_kernel(x_ref, ya_ref, yb_ref, ga_ref, gb_ref, wa_ref, wb_ref, wo_ref, h_ref):
    ma = _dot(ya_ref[...], wa_ref[...])
    mb = _dot(yb_ref[...], wb_ref[...])
    mixed = _sigmoid(ga_ref[...]) * ma + _sigmoid(gb_ref[...]) * mb
    h_ref[...] = x_ref[...] + _dot(mixed, wo_ref[...])


def _mix(x2, ya, yb, proj, wa, wb, wo):
    tm = 256
    row = lambda w, c=0: pl.BlockSpec((tm, w), lambda i, c=c: (i, c))
    const = lambda s: pl.BlockSpec(s, lambda i: (0, 0), pipeline_mode=pl.Buffered(1))
    return pl.pallas_call(
        _mix_kernel,
        grid=(ROWS // tm,),
        in_specs=[row(D_MODEL), row(1024), row(1024), row(D_MODEL, COL_GA // D_MODEL), row(D_MODEL, COL_GB // D_MODEL),
                  const((1024, D_MODEL)), const((1024, D_MODEL)), const((D_MODEL, D_MODEL))],
        out_specs=row(D_MODEL),
        out_shape=jax.ShapeDtypeStruct((ROWS, D_MODEL), F32),
        compiler_params=_params(("parallel",)),
        name="mix",
    )(x2, ya, yb, proj, proj, wa, wb, wo)


def _mlp_kernel(h_ref, gm_ref, wu_ref, wd_ref, gf_ref, o_ref, hn_ref, acc_ref):
    f = pl.program_id(1)

    @pl.when(f == 0)
    def _():
        h = h_ref[...]
        ms = jnp.mean(h * h, axis=-1, keepdims=True)
        hn_ref[...] = (h * lax.rsqrt(ms + NORM_EPS) * gm_ref[...]).astype(BF16)
        acc_ref[...] = jnp.zeros_like(acc_ref)

    u = jnp.maximum(jnp.dot(hn_ref[...], wu_ref[...], preferred_element_type=F32), 0.0)
    acc_ref[...] += jnp.dot((u * u).astype(BF16), wd_ref[...], preferred_element_type=F32)

    @pl.when(f == pl.num_programs(1) - 1)
    def _():
        h2 = h_ref[...] + acc_ref[...]
        ms = jnp.mean(h2 * h2, axis=-1, keepdims=True)
        o_ref[...] = h2 * lax.rsqrt(ms + NORM_EPS) * gf_ref[...]


def _mlp(h, g_mlp, w_up, w_down, g_final):
    tm, tf = 512, 1024
    return pl.pallas_call(
        _mlp_kernel,
        grid=(ROWS // tm, D_FF // tf),
        in_specs=[pl.BlockSpec((tm, D_MODEL), lambda i, f: (i, 0)),
                  pl.BlockSpec((1, D_MODEL), lambda i, f: (0, 0)),
                  pl.BlockSpec((D_MODEL, tf), lambda i, f: (0, f)),
                  pl.BlockSpec((tf, D_MODEL), lambda i, f: (f, 0)),
                  pl.BlockSpec((1, D_MODEL), lambda i, f: (0, 0))],
        out_specs=pl.BlockSpec((tm, D_MODEL), lambda i, f: (i, 0)),
        out_shape=jax.ShapeDtypeStruct((ROWS, D_MODEL), F32),
        scratch_shapes=[pltpu.VMEM((tm, D_MODEL), BF16), pltpu.VMEM((tm, D_MODEL), F32)],
        compiler_params=_params(("parallel", "arbitrary")),
        name="mlp",
    )(h, g_mlp, w_up, w_down, g_final)


def _interleave_heads(a, axis=-1):
    a = jnp.moveaxis(a, axis, -1)
    a = a.reshape(a.shape[:-1] + (RWKV_HEADS, HEAD_DIM)).swapaxes(-1, -2).reshape(a.shape)
    return jnp.moveaxis(a, -1, axis)


def _overlap_matrix():
    cs = np.arange(N_CMP)[:, None] * CMP_STRIDE
    ss = np.arange(N_SEL)[None, :] * SEL_BLOCK
    ov = np.clip(np.minimum(cs + CMP_BLOCK, ss + SEL_BLOCK) - np.maximum(cs, ss), 0, None) / CMP_BLOCK
    out = np.zeros((N_CMP_PAD, N_SEL), np.float32)
    out[:N_CMP] = ov
    return out


def kernel(x, norm_mix, w_in, rwkv_mu, rwkv_w0, rwkv_w_up, rwkv_a0, rwkv_a_up, rwkv_g_up, rwkv_k_k, rwkv_k_a,
           rwkv_r_k, rwkv_lnx_w, rwkv_lnx_b, cmp_pe_k, cmp_w1_k, cmp_w2_k, cmp_pe_v, cmp_w1_v, cmp_w2_v,
           w_out_rwkv, w_out_nsa, w_o, norm_mlp, mlp_w_up, mlp_w_down, norm_final):
    assert x.shape == (BATCH, SEQ, D_MODEL) and w_in.shape[0] == 1
    l = 0
    G = NSA_KV_HEADS
    x2 = x.reshape(ROWS, D_MODEL)
    row = lambda a: a.reshape(1, -1)

    w_p = _pack_weight(w_in[l].T)
    proj = _inproj(x2, row(norm_mix[l]), w_p)

    il = _interleave_heads
    mu = rwkv_mu[l]
    mu_p = jnp.concatenate([il(mu[0:3072].reshape(3, RWKV_WIDTH)).reshape(-1), mu[3072:3168], jnp.zeros((32,), F32),
                            mu[3168:3264], jnp.zeros((32,), F32), mu[3264:3520]])
    zrows = jnp.zeros((32, RWKV_WIDTH), F32)
    head_of_lane = np.arange(RWKV_WIDTH)[:, None] % RWKV_HEADS == np.arange(LANES)[None, :]
    head_gather = jnp.asarray(head_of_lane, BF16)
    head_spread = jnp.asarray(head_of_lane.T, BF16)
    r, w, k, v, kk, b, g, bonus = _rwkvprep(
        proj, row(mu_p), row(il(rwkv_w0[l])), jnp.concatenate([il(rwkv_w_up[l]), zrows]),
        row(il(rwkv_a0[l])), jnp.concatenate([il(rwkv_a_up[l]), zrows]), il(rwkv_g_up[l]),
        row(il(rwkv_k_k[l].reshape(-1))), row(il(rwkv_k_a[l].reshape(-1))), row(il(rwkv_r_k[l].reshape(-1))),
        head_gather, head_spread)
    kkn_s, w_s, b_s = _relayout([kk, w, b], ("shift", "rows", "rows"))
    k_s, r_s, v_s = _relayout([k, r, v], ("rows", "rows", "time_major"))
    y = _scan(kkn_s, w_s, b_s, k_s, r_s, v_s)
    ya = _rwkv_out(y, bonus, g, row(il(rwkv_lnx_w[l])), row(il(rwkv_lnx_b[l])), head_gather, head_spread)

    craw, ks, vst, kw, vwt = _kvprep(proj)
    kcvc = _compress(craw, jnp.stack([cmp_w1_k[l], cmp_w1_v[l]]),
                     jnp.stack([cmp_pe_k[l].reshape(1, -1), cmp_pe_v[l].reshape(1, -1)]),
                     jnp.stack([cmp_w2_k[l], cmp_w2_v[l]]))
    per_group = lambda a: a.reshape((BATCH * G,) + a.shape[2:])
    yb = _nsa(proj, kcvc, per_group(ks), per_group(vst), per_group(kw), per_group(vwt),
              jnp.asarray(_overlap_matrix().T, BF16), jnp.asarray(_key_features(), BF16))

    h = _mix(x2, ya, yb, proj, il(w_out_rwkv[l], axis=0).astype(BF16), w_out_nsa[l].astype(BF16), w_o[l].astype(BF16))
    out = _mlp(h, row(norm_mlp[l]), mlp_w_up[l].astype(BF16), mlp_w_down[l].astype(BF16), row(norm_final))
    return out.reshape(BATCH, SEQ, D_MODEL)
```

```python
import functools

import numpy as np
import jax
import jax.numpy as jnp
from jax import lax
from jax.experimental import pallas as pl
from jax.experimental.pallas import tpu as pltpu

F32 = jnp.float32
BF16 = jnp.bfloat16

D_MODEL = 2048
BATCH = 4
SEQ = 2048
ROWS = BATCH * SEQ
RWKV_HEADS = 16
HEAD_DIM = 64
RWKV_WIDTH = RWKV_HEADS * HEAD_DIM
LORA_DECAY = 96
LORA_ICLR = 96
LORA_GATE = 256
GN_EPS = 64e-5
NSA_HEADS = 16
NSA_KV_HEADS = 4
NSA_GROUP = NSA_HEADS // NSA_KV_HEADS
NSA_Q_WIDTH = NSA_HEADS * HEAD_DIM
NSA_KV_WIDTH = NSA_KV_HEADS * HEAD_DIM
CMP_BLOCK = 32
CMP_STRIDE = 16
CMP_HIDDEN = 256
N_CMP = (SEQ - CMP_BLOCK) // CMP_STRIDE + 1
N_CMP_PAD = 128
SEL_BLOCK = 64
SEL_TOPN = 16
N_SEL = SEQ // SEL_BLOCK
WINDOW = 512
D_FF = 4 * D_MODEL
NORM_EPS = 1e-5
NEG_INF = -1e30
TINY = 1e-30

LANES = 128
VMEM_LIMIT = 56 * 1024 * 1024

COL_GA = 0
COL_GB = 2048
COL_R = 4096
COL_K = 5120
COL_V = 6144
COL_LORA = 7168
COL_Q = 7680
COL_KC = 8704
COL_VC = 8960
COL_KS = 9216
COL_VS = 9472
COL_KW = 9728
COL_VW = 9984
COL_NG = 10240
PROJ_COLS = 10752
LORA_PACK = 512


def _dot(a, b):
    return jnp.dot(a.astype(BF16), b.astype(BF16), preferred_element_type=F32)


def _sigmoid(x):
    return 1.0 / (1.0 + jnp.exp(-x))


def _params(sem, limit=VMEM_LIMIT):
    return pltpu.CompilerParams(dimension_semantics=sem, vmem_limit_bytes=limit)


def _inproj_kernel(x_ref, g_ref, w_ref, o_ref, xn_ref):
    @pl.when(pl.program_id(1) == 0)
    def _():
        x = x_ref[...]
        ms = jnp.mean(x * x, axis=-1, keepdims=True)
        xn_ref[...] = (x * lax.rsqrt(ms + NORM_EPS) * g_ref[...]).astype(BF16)

    o_ref[...] = jnp.dot(xn_ref[...], w_ref[...], preferred_element_type=F32)


def _pack_kernel(wt_ref, o_ref):
    rows = lambda a, b: wt_ref[a:b, :]
    pad = lambda n: jnp.zeros((n, wt_ref.shape[1]), F32)
    rkv = [wt_ref[pl.ds(blk * RWKV_WIDTH + q, RWKV_HEADS, stride=HEAD_DIM), :]
           for blk in range(3) for q in range(HEAD_DIM)]
    packed = jnp.concatenate(
        [rows(6128, 10224)]
        + rkv
        + [rows(3072, 3168), pad(32),
           rows(3168, 3264), pad(32),
           rows(3264, 3520),
           rows(3520, 6080),
           rows(6080, 6128), pad(464)], axis=0)
    o_ref[...] = packed.T.astype(BF16)


def _pack_weight(w_t):
    tk = 128
    return pl.pallas_call(
        _pack_kernel,
        grid=(D_MODEL // tk,),
        in_specs=[pl.BlockSpec((w_t.shape[0], tk), lambda i: (0, i))],
        out_specs=pl.BlockSpec((tk, PROJ_COLS), lambda i: (i, 0)),
        out_shape=jax.ShapeDtypeStruct((D_MODEL, PROJ_COLS), BF16),
        compiler_params=_params(("parallel",)),
        name="packw",
    )(w_t)


def _inproj(x2, g, w_p):
    tm, tn = 1024, 1536
    return pl.pallas_call(
        _inproj_kernel,
        grid=(ROWS // tm, PROJ_COLS // tn),
        in_specs=[pl.BlockSpec((tm, D_MODEL), lambda i, j: (i, 0)),
                  pl.BlockSpec((1, D_MODEL), lambda i, j: (0, 0)),
                  pl.BlockSpec((D_MODEL, tn), lambda i, j: (0, j))],
        out_specs=pl.BlockSpec((tm, tn), lambda i, j: (i, j)),
        out_shape=jax.ShapeDtypeStruct((ROWS, PROJ_COLS), F32),
        scratch_shapes=[pltpu.VMEM((tm, D_MODEL), BF16)],
        compiler_params=_params(("parallel", "arbitrary")),
        name="inproj",
    )(x2, g, w_p)


def _dot_hi_lo(x, m):
    hi = x.astype(BF16)
    lo = (x - hi.astype(F32)).astype(BF16)
    return jnp.dot(hi, m, preferred_element_type=F32) + jnp.dot(lo, m, preferred_element_type=F32)


def _head_sum(x, gather_ref, spread_ref):
    return _dot_hi_lo(_dot_hi_lo(x, gather_ref[...]), spread_ref[...])


def _rwkvprep_kernel(r_ref, k_ref, v_ref, lo_ref, rp_ref, kp_ref, vp_ref, lop_ref,
                     mu_ref, w0_ref, wup_ref, a0_ref, aup_ref, gup_ref, kk_ref, ka_ref, rk_ref, hg_ref, hs_ref,
                     ro_ref, wo_ref, ko_ref, vo_ref, kko_ref, bo_ref, go_ref, bon_ref, *, tm):
    i = pl.program_id(0)
    has_prev = jnp.where((i * tm) % SEQ == 0, 0.0, 1.0).astype(F32)

    def shift(cur_ref, prev_ref, mu):
        z = cur_ref[...]
        zp = pltpu.roll(z, 1, axis=0)
        prev_row = prev_ref[7:8, :] * has_prev
        row = lax.broadcasted_iota(jnp.int32, z.shape, 0)
        zp = jnp.where(row == 0, prev_row, zp)
        return z + (zp - z) * mu

    r = shift(r_ref, rp_ref, mu_ref[:, 0:1024])
    k = shift(k_ref, kp_ref, mu_ref[:, 1024:2048])
    v = shift(v_ref, vp_ref, mu_ref[:, 2048:3072])
    lo = shift(lo_ref, lop_ref, mu_ref[:, 3072:3584])
    w_lo, a_lo, g_lo = lo[:, 0:128], lo[:, 128:256], lo[:, 256:512]

    u = w0_ref[...] + _dot(jnp.tanh(w_lo), wup_ref[...])
    decay = jnp.exp(-(_sigmoid(u) * float(np.exp(-0.5))))
    a = _sigmoid(a0_ref[...] + _dot(a_lo, aup_ref[...]))
    g = _dot(_sigmoid(g_lo), gup_ref[...])

    kk = k * kk_ref[...]
    ss = _head_sum(kk * kk, hg_ref, hs_ref)
    kk = kk * lax.rsqrt(jnp.maximum(ss, 1e-24))
    k = k * (1.0 + (a - 1.0) * ka_ref[...])
    coef = _head_sum(r * k * rk_ref[...], hg_ref, hs_ref)

    ro_ref[...] = r
    wo_ref[...] = decay
    ko_ref[...] = k
    vo_ref[...] = v
    kko_ref[...] = kk
    bo_ref[...] = kk * a
    go_ref[...] = g
    bon_ref[...] = coef * v


def _rwkvprep(proj, mu_p, w0, wup_p, a0, aup_p, gup, k_k, k_a, r_k, head_gather, head_spread):
    tm = 256
    pb = tm // 8
    cur = lambda c: pl.BlockSpec((tm, 1024), lambda i, c=c: (i, c))
    prev = lambda c: pl.BlockSpec((8, 1024), lambda i, c=c: (jnp.maximum(i * pb - 1, 0), c))
    full = lambda s: pl.BlockSpec(s, lambda i: (0,) * len(s))
    out = pl.BlockSpec((tm, 1024), lambda i: (i, 0))
    osh = jax.ShapeDtypeStruct((ROWS, RWKV_WIDTH), F32)
    return pl.pallas_call(
        functools.partial(_rwkvprep_kernel, tm=tm),
        grid=(ROWS // tm,),
        in_specs=[cur(COL_R // 1024), cur(COL_K // 1024), cur(COL_V // 1024),
                  pl.BlockSpec((tm, LORA_PACK), lambda i: (i, COL_LORA // LORA_PACK)),
                  prev(COL_R // 1024), prev(COL_K // 1024), prev(COL_V // 1024),
                  pl.BlockSpec((8, LORA_PACK), lambda i: (jnp.maximum(i * pb - 1, 0), COL_LORA // LORA_PACK)),
                  full((1, 3584)), full((1, 1024)), full((128, 1024)), full((1, 1024)), full((128, 1024)),
                  full((256, 1024)), full((1, 1024)), full((1, 1024)), full((1, 1024)),
                  full((RWKV_WIDTH, LANES)), full((LANES, RWKV_WIDTH))],
        out_specs=[out] * 8,
        out_shape=[osh] * 8,
        compiler_params=_params(("parallel",)),
        name="rwkvprep",
    )(proj, proj, proj, proj, proj, proj, proj, proj,
      mu_p, w0, wup_p, a0, aup_p, gup, k_k, k_a, r_k, head_gather, head_spread)


N_JPAIR = HEAD_DIM // 2
N_JOPS = 5
SCAN_T = 64


def _scan_kernel(kkn_ref, w_ref, b_ref, k_ref, r_ref, v_ref, y_ref, s_ref, sa_ref, d_ref, *, tc):
    @pl.when(pl.program_id(0) == 0)
    def _():
        s_ref[...] = jnp.zeros_like(s_ref)
        sa_ref[...] = jnp.zeros_like(sa_ref)

    lo_half = lax.broadcasted_iota(jnp.int32, (N_JPAIR, tc, LANES), 2) < 64
    for n, ref in enumerate((kkn_ref, w_ref, b_ref, k_ref, r_ref)):
        x = ref[0]
        xr = pltpu.roll(x, 64, axis=2)
        d_ref[n, 0:N_JPAIR] = jnp.where(lo_half, x, xr)
        d_ref[n, N_JPAIR:HEAD_DIM] = jnp.where(lo_half, xr, x)

    def tree(parts):
        while len(parts) > 1:
            parts = [parts[n] + parts[n + 1] for n in range(0, len(parts), 2)]
        return parts[0]

    def step(t, sa):
        vt = v_ref[t]
        acc_y = [None] * 4
        acc_s = [None] * 4
        for j in range(HEAD_DIM):
            row = lambda n: d_ref[n, j, pl.ds(t, 1), :]
            sn = s_ref[j] * row(1) - sa * row(2) + vt * row(3)
            s_ref[j] = sn
            ty = sn * row(4)
            ts = sn * row(0)
            acc_y[j % 4] = ty if acc_y[j % 4] is None else acc_y[j % 4] + ty
            acc_s[j % 4] = ts if acc_s[j % 4] is None else acc_s[j % 4] + ts
        y_ref[t] = tree(acc_y)
        return tree(acc_s)

    sa_ref[...] = lax.fori_loop(0, tc, step, sa_ref[...])


def _scan(kkn, w, b, k, r, v):
    tc = SCAN_T
    spec = pl.BlockSpec((tc, N_JPAIR, LANES), lambda i: (i, 0, 0))
    jspec = pl.BlockSpec((1, N_JPAIR, tc, LANES), lambda i: (i, 0, 0, 0))
    return pl.pallas_call(
        functools.partial(_scan_kernel, tc=tc),
        grid=(SEQ // tc,),
        in_specs=[jspec] * N_JOPS + [spec],
        out_specs=spec,
        out_shape=jax.ShapeDtypeStruct((SEQ, N_JPAIR, LANES), F32),
        scratch_shapes=[pltpu.VMEM((HEAD_DIM, N_JPAIR, LANES), F32),
                        pltpu.VMEM((N_JPAIR, LANES), F32),
                        pltpu.VMEM((N_JOPS, HEAD_DIM, tc, LANES), F32)],
        compiler_params=_params(("arbitrary",)),
        name="scan",
    )(kkn, w, b, k, r, v)


RELAYOUT_T = 128
_HALF_BATCH = [(half, b) for half in range(2) for b in range(BATCH)]


def _gather_lanes(z_ref, p):
    return jnp.concatenate([z_ref[b, pl.ds((half * N_JPAIR + p) * RWKV_HEADS, RWKV_HEADS), :]
                            for half, b in _HALF_BATCH], axis=0)


def _relayout_kernel(*refs, modes):
    n_in = len(modes) + modes.count("shift")
    in_refs, out_refs, z_ref = iter(refs[:n_in]), refs[n_in:-1], refs[-1]
    has_next = jnp.where(pl.program_id(0) == pl.num_programs(0) - 1, 0.0, 1.0).astype(F32)
    last_row = lax.broadcasted_iota(jnp.int32, (RELAYOUT_T, RWKV_WIDTH), 0) == RELAYOUT_T - 1
    for mode, o_ref in zip(modes, out_refs):
        x_ref = next(in_refs)
        nxt_ref = next(in_refs) if mode == "shift" else None
        for b in range(BATCH):
            x = x_ref[b]
            if mode == "shift":
                x = jnp.where(last_row, nxt_ref[b, 0:1, :] * has_next, pltpu.roll(x, RELAYOUT_T - 1, axis=0))
            z_ref[b] = x.T
        for p in range(N_JPAIR):
            tile = _gather_lanes(z_ref, p).T
            if mode == "time_major":
                o_ref[pl.ds(p, RELAYOUT_T, stride=N_JPAIR), :] = tile
            else:
                for c in range(RELAYOUT_T // SCAN_T):
                    o_ref[c, p] = tile[c * SCAN_T:(c + 1) * SCAN_T]


def _rwkv_out_kernel(y_ref, bon_ref, g_ref, lw_ref, lb_ref, hg_ref, hs_ref, o_ref, z_ref):
    for p in range(N_JPAIR):
        tile = y_ref[pl.ds(p, RELAYOUT_T, stride=N_JPAIR), :].T
        for n, (half, b) in enumerate(_HALF_BATCH):
            z_ref[b, pl.ds((half * N_JPAIR + p) * RWKV_HEADS, RWKV_HEADS), :] = (
                tile[n * RWKV_HEADS:(n + 1) * RWKV_HEADS])
    for b in range(BATCH):
        y = z_ref[b].T
        mean = _head_sum(y, hg_ref, hs_ref) * (1.0 / HEAD_DIM)
        d = y - mean
        var = _head_sum(d * d, hg_ref, hs_ref) * (1.0 / HEAD_DIM)
        yn = d * lax.rsqrt(var + GN_EPS) * lw_ref[...] + lb_ref[...]
        o_ref[b] = (yn + bon_ref[b]) * g_ref[b]


_NATURAL_SPEC = pl.BlockSpec((BATCH, RELAYOUT_T, RWKV_WIDTH), lambda i: (0, i, 0))
_TIME_MAJOR_SPEC = pl.BlockSpec((RELAYOUT_T * N_JPAIR, LANES), lambda i: (i, 0))
_RELAYOUT_SCRATCH = [pltpu.VMEM((BATCH, RWKV_WIDTH, RELAYOUT_T), F32)]


def _relayout(arrays, modes):
    blocks_per_step = RELAYOUT_T // 8
    nxt_spec = pl.BlockSpec((BATCH, 8, RWKV_WIDTH),
                            lambda i: (0, jnp.minimum((i + 1) * blocks_per_step, SEQ // 8 - 1), 0))
    rows_spec = pl.BlockSpec((RELAYOUT_T // SCAN_T, N_JPAIR, SCAN_T, LANES), lambda i: (i, 0, 0, 0))
    operands, in_specs, out_specs, out_shapes = [], [], [], []
    for a, mode in zip(arrays, modes):
        a = a.reshape(BATCH, SEQ, RWKV_WIDTH)
        operands += [a, a] if mode == "shift" else [a]
        in_specs += [_NATURAL_SPEC, nxt_spec] if mode == "shift" else [_NATURAL_SPEC]
        out_specs.append(_TIME_MAJOR_SPEC if mode == "time_major" else rows_spec)
        out_shapes.append(jax.ShapeDtypeStruct(
            (SEQ * N_JPAIR, LANES) if mode == "time_major" else (SEQ // SCAN_T, N_JPAIR, SCAN_T, LANES), F32))
    outs = pl.pallas_call(
        functools.partial(_relayout_kernel, modes=tuple(modes)),
        grid=(SEQ // RELAYOUT_T,),
        in_specs=in_specs,
        out_specs=out_specs,
        out_shape=out_shapes,
        scratch_shapes=_RELAYOUT_SCRATCH,
        compiler_params=_params(("parallel",)),
        name="relayout",
    )(*operands)
    return [o.reshape(SEQ, N_JPAIR, LANES) if mode == "time_major" else o for o, mode in zip(outs, modes)]


def _rwkv_out(y, bonus, g, lnx_w, lnx_b, head_gather, head_spread):
    vec = pl.BlockSpec((1, RWKV_WIDTH), lambda i: (0, 0))
    nat = lambda a: a.reshape(BATCH, SEQ, RWKV_WIDTH)
    return pl.pallas_call(
        _rwkv_out_kernel,
        grid=(SEQ // RELAYOUT_T,),
        in_specs=[_TIME_MAJOR_SPEC, _NATURAL_SPEC, _NATURAL_SPEC, vec, vec,
                  pl.BlockSpec((RWKV_WIDTH, LANES), lambda i: (0, 0)),
                  pl.BlockSpec((LANES, RWKV_WIDTH), lambda i: (0, 0))],
        out_specs=_NATURAL_SPEC,
        out_shape=jax.ShapeDtypeStruct((BATCH, SEQ, RWKV_WIDTH), F32),
        scratch_shapes=_RELAYOUT_SCRATCH,
        compiler_params=_params(("parallel",)),
        name="rwkvout",
    )(y.reshape(SEQ * N_JPAIR, LANES), nat(bonus), nat(g), lnx_w, lnx_b, head_gather, head_spread
      ).reshape(ROWS, RWKV_WIDTH)


KV_T = 512
KV_PACK = 2 * NSA_KV_WIDTH


def _kvprep_kernel(c_ref, s_ref, w_ref, craw_ref, ks_ref, vst_ref, kw_ref, vwt_ref):
    c = c_ref[...]
    for g in range(NSA_KV_HEADS):
        craw_ref[0, 0, g] = c[:, g * HEAD_DIM:(g + 1) * HEAD_DIM]
        craw_ref[1, 0, g] = c[:, NSA_KV_WIDTH + g * HEAD_DIM:NSA_KV_WIDTH + (g + 1) * HEAD_DIM]
    for src_ref, k_out, vt_out in ((s_ref, ks_ref, vst_ref), (w_ref, kw_ref, vwt_ref)):
        x = src_ref[...]
        vt = x[:, NSA_KV_WIDTH:].T
        for g in range(NSA_KV_HEADS):
            k_out[0, g] = x[:, g * HEAD_DIM:(g + 1) * HEAD_DIM].astype(BF16)
            vt_out[0, g] = vt[g * HEAD_DIM:(g + 1) * HEAD_DIM].astype(BF16)


def _kvprep(proj):
    nt = SEQ // KV_T
    G = NSA_KV_HEADS
    src = lambda col: pl.BlockSpec((KV_T, KV_PACK), lambda b, i, col=col: (b * nt + i, col // KV_PACK))
    natural = pl.BlockSpec((1, G, KV_T, HEAD_DIM), lambda b, i: (b, 0, i, 0))
    transposed = pl.BlockSpec((1, G, HEAD_DIM, KV_T), lambda b, i: (b, 0, 0, i))
    nat_shape = jax.ShapeDtypeStruct((BATCH, G, SEQ, HEAD_DIM), BF16)
    tr_shape = jax.ShapeDtypeStruct((BATCH, G, HEAD_DIM, SEQ), BF16)
    return pl.pallas_call(
        _kvprep_kernel,
        grid=(BATCH, nt),
        in_specs=[src(COL_KC), src(COL_KS), src(COL_KW)],
        out_specs=[pl.BlockSpec((2, 1, G, KV_T, HEAD_DIM), lambda b, i: (0, b, 0, i, 0)),
                   natural, transposed, natural, transposed],
        out_shape=[jax.ShapeDtypeStruct((2, BATCH, G, SEQ, HEAD_DIM), F32), nat_shape, tr_shape, nat_shape, tr_shape],
        compiler_params=_params(("parallel", "parallel")),
        name="kvprep",
    )(proj, proj, proj)


def _compress_kernel(kv_ref, w1_ref, pe_ref, w2_ref, o_ref):
    w1 = w1_ref[0]
    top = bot = None
    for l in range(CMP_STRIDE):
        rows = kv_ref[0, 0, 0, pl.ds(l, N_CMP_PAD, stride=CMP_STRIDE), :]
        t_l = _dot(rows, w1[l * HEAD_DIM:(l + 1) * HEAD_DIM])
        b_l = _dot(rows, w1[(CMP_STRIDE + l) * HEAD_DIM:(CMP_STRIDE + l + 1) * HEAD_DIM])
        top = t_l if top is None else top + t_l
        bot = b_l if bot is None else bot + b_l
    bias = _dot(jnp.broadcast_to(pe_ref[0], (8, CMP_BLOCK * HEAD_DIM)), w1)[0:1]
    hid = top + pltpu.roll(bot, N_CMP_PAD - 1, axis=0) + bias
    c0 = float(np.sqrt(2.0 / np.pi))
    act = 0.5 * hid * (1.0 + jnp.tanh(c0 * (hid + 0.044715 * (hid * hid * hid))))
    o_ref[0, 0] = _dot(act, w2_ref[0])


def _compress(craw, w1, pe, w2):
    G = NSA_KV_HEADS
    nbg = BATCH * G
    return pl.pallas_call(
        _compress_kernel,
        grid=(2, nbg),
        in_specs=[pl.BlockSpec((1, 1, 1, SEQ, HEAD_DIM), lambda c, n: (c, n // G, n % G, 0, 0)),
                  pl.BlockSpec((1, CMP_BLOCK * HEAD_DIM, CMP_HIDDEN), lambda c, n: (c, 0, 0)),
                  pl.BlockSpec((1, 1, CMP_BLOCK * HEAD_DIM), lambda c, n: (c, 0, 0)),
                  pl.BlockSpec((1, CMP_HIDDEN, HEAD_DIM), lambda c, n: (c, 0, 0))],
        out_specs=pl.BlockSpec((1, 1, N_CMP_PAD, HEAD_DIM), lambda c, n: (c, n, 0, 0)),
        out_shape=jax.ShapeDtypeStruct((2, nbg, N_CMP_PAD, HEAD_DIM), F32),
        compiler_params=_params(("parallel", "parallel")),
        name="compress",
    )(craw, w1, pe, w2)


MASK_BIG = float(2.0 ** 100)
KEY_CHUNK = 256
N_FEAT = 64


def _key_features():
    s = np.arange(SEQ)
    f = np.zeros((SEQ, N_FEAT), np.float32)
    f[s, s // SEL_BLOCK] = 1.0
    f[:, 32:35] = (SEL_BLOCK * (s // SEL_BLOCK))[:, None]
    f[:, 35:38] = (s % SEL_BLOCK)[:, None]
    return f


def _split3(x):
    a1 = x.astype(BF16)
    r1 = x - a1.astype(F32)
    a2 = r1.astype(BF16)
    a3 = (r1 - a2.astype(F32)).astype(BF16)
    return a1, a2, a3


def _nsa_kernel(q_ref, kc_ref, vc_ref, ks_ref, vst_ref, kw_ref, vwt_ref, gate_ref, ovt_ref, feat_ref,
                cmask_ref, wmask_ref, o_ref, gt_ref, slc_ref, *, tq):
    g = pl.program_id(1)
    t0 = pl.program_id(2) * tq
    hg = NSA_GROUP
    nq = hg * tq
    dot = functools.partial(jnp.dot, preferred_element_type=F32)

    qt = (q_ref[...] * HEAD_DIM ** -0.5).T
    qt = jnp.concatenate([qt[h * HEAD_DIM:(h + 1) * HEAD_DIM] for h in range(hg)], axis=1).astype(BF16)
    lane = lax.broadcasted_iota(jnp.int32, (1, nq), 1)
    slope = jnp.exp2(-0.5 * (g * hg + lane // tq + 1).astype(F32))
    qpos = t0 + lane % tq

    n_idx = lax.broadcasted_iota(jnp.int32, (N_CMP_PAD, 1), 0)
    dist_c = qpos.astype(F32) - (n_idx * CMP_STRIDE + (CMP_BLOCK - 1)).astype(F32)
    mask_c = (dist_c >= 0) & (n_idx < N_CMP)
    x_c = jnp.where(mask_c, dot(kc_ref[0, 0].astype(BF16), qt) - slope * dist_c, NEG_INF)
    p_c = jnp.where(mask_c, jnp.exp(x_c - jnp.max(x_c, axis=0, keepdims=True)), 0.0)
    den_c = jnp.maximum(jnp.sum(p_c, axis=0, keepdims=True), TINY)
    o_cmp = dot(vc_ref[0, 0].T.astype(BF16), p_c.astype(BF16)) / den_c

    p_n = p_c / den_c
    p_sum = (p_n[:, 0:tq] + p_n[:, tq:2 * tq]) + (p_n[:, 2 * tq:3 * tq] + p_n[:, 3 * tq:4 * tq])
    ovt = ovt_ref[...]
    imp = sum(dot(ovt, part) for part in _split3(p_sum))
    blk = lax.broadcasted_iota(jnp.int32, (N_SEL, tq), 0)
    cur = (t0 + lax.broadcasted_iota(jnp.int32, (N_SEL, tq), 1)) // SEL_BLOCK
    forced = (blk == 0) | (blk == cur) | (blk == cur - 1)
    score = jnp.where(blk > cur, NEG_INF, jnp.where(forced, -NEG_INF, imp))
    rank = jnp.zeros((N_SEL, tq), F32)
    for jp in range(N_SEL):
        row = score[jp:jp + 1, :]
        beats = (row > score) | ((row == score) & (blk > jp))
        rank = rank + jnp.where(beats, 1.0, 0.0)
    sel_bias = jnp.where((rank < SEL_TOPN) & (blk <= cur), 0.0, -MASK_BIG)
    sel_bias = jnp.concatenate([sel_bias] * hg, axis=1)

    s1, s2, s3 = (piece.astype(F32) for piece in _split3(slope))
    r8 = lax.broadcasted_iota(jnp.int32, (8, nq), 0)
    srow = jnp.where((r8 == 0) | (r8 == 3), s1, jnp.where((r8 == 1) | (r8 == 4), s2, s3))
    srow = jnp.where(r8 < 6, srow, 0.0)
    pad = jnp.zeros((N_FEAT - N_SEL - 8, nq), F32)
    q_slc = jnp.concatenate([qt, jnp.concatenate([sel_bias, srow, pad], axis=0).astype(BF16)], axis=0)
    q_win = jnp.concatenate([qt, jnp.concatenate([jnp.zeros_like(sel_bias), srow, pad], axis=0).astype(BF16)],
                            axis=0)

    def scores(k_ref, start, size, q_aug):
        keys = jnp.concatenate([k_ref[0, pl.ds(start, size), :], feat_ref[pl.ds(start, size), :]], axis=1)
        return dot(keys, q_aug)

    def attend(s, v_t):
        p = jnp.exp(s - jnp.max(s, axis=0, keepdims=True))
        return dot(v_t, p.astype(BF16)) / jnp.maximum(jnp.sum(p, axis=0, keepdims=True), TINY)

    def slc_variant(n_chunks):
        size = n_chunks * KEY_CHUNK
        s = scores(ks_ref, 0, size, q_slc)
        last = s[size - KEY_CHUNK:] + cmask_ref[0]
        s = last if n_chunks == 1 else jnp.concatenate([s[:size - KEY_CHUNK], last], axis=0)
        slc_ref[...] = attend(s, vst_ref[0, :, 0:size])

    for n_chunks in range(1, SEQ // KEY_CHUNK + 1):
        pl.when(t0 // KEY_CHUNK == n_chunks - 1)(functools.partial(slc_variant, n_chunks))
    o_slc = slc_ref[...]

    span = WINDOW + tq
    start = pl.multiple_of(jnp.maximum(t0 - WINDOW, 0), LANES)
    o_win = attend(scores(kw_ref, start, span, q_win) + wmask_ref[0], vwt_ref[0, :, pl.ds(start, span)])

    gt_ref[...] = _sigmoid(gate_ref[...]).T
    out = jnp.zeros((HEAD_DIM, nq), F32)
    for c, o in enumerate((o_cmp, o_slc, o_win)):
        g8 = gt_ref[pl.ds(pl.multiple_of(c * NSA_HEADS + (g // 2) * 8, 8), 8), :]
        g4 = jnp.where(g % 2 == 0, g8[0:hg], g8[hg:2 * hg])
        out = out + jnp.concatenate([g4[h:h + 1, :] for h in range(hg)], axis=1) * o
    o_ref[...] = jnp.concatenate([out[:, h * tq:(h + 1) * tq].T for h in range(hg)], axis=1)


def _nsa_masks(tq):
    qq = np.arange(NSA_GROUP * tq)[None, :] % tq
    k_c = np.arange(KEY_CHUNK)[:, None]
    causal = np.stack([k_c <= o * tq + qq for o in range(KEY_CHUNK // tq)])
    k_w = np.arange(WINDOW + tq)[:, None]
    dist = [min(o * tq, WINDOW) + qq - k_w for o in range(WINDOW // tq + 1)]
    window = np.stack([(d >= 0) & (d < WINDOW) for d in dist])
    to_bias = lambda ok: jnp.where(jnp.asarray(ok), 0.0, -MASK_BIG).astype(F32)
    return to_bias(causal), to_bias(window)


def _nsa(proj, kcvc, ks, vst, kw, vwt, overlap_t, feat):
    tq = 256
    nq = SEQ // tq
    cmask, wmask = _nsa_masks(tq)
    G = NSA_KV_HEADS
    kv_n = pl.BlockSpec((1, SEQ, HEAD_DIM), lambda b, g, i: (b * G + g, 0, 0))
    kv_t = pl.BlockSpec((1, HEAD_DIM, SEQ), lambda b, g, i: (b * G + g, 0, 0))
    cmp_spec = lambda c: pl.BlockSpec((1, 1, N_CMP_PAD, HEAD_DIM), lambda b, g, i, c=c: (c, b * G + g, 0, 0))
    return pl.pallas_call(
        functools.partial(_nsa_kernel, tq=tq),
        grid=(BATCH, G, nq),
        in_specs=[pl.BlockSpec((tq, NSA_GROUP * HEAD_DIM), lambda b, g, i: (b * nq + i, COL_Q // 256 + g)),
                  cmp_spec(0), cmp_spec(1), kv_n, kv_t, kv_n, kv_t,
                  pl.BlockSpec((tq, LANES), lambda b, g, i: (b * nq + i, COL_NG // LANES)),
                  pl.BlockSpec((N_SEL, N_CMP_PAD), lambda b, g, i: (0, 0)),
                  pl.BlockSpec((SEQ, N_FEAT), lambda b, g, i: (0, 0)),
                  pl.BlockSpec((1,) + cmask.shape[1:], lambda b, g, i: (i % (KEY_CHUNK // tq), 0, 0)),
                  pl.BlockSpec((1,) + wmask.shape[1:], lambda b, g, i: (jnp.minimum(i, WINDOW // tq), 0, 0))],
        out_specs=pl.BlockSpec((tq, NSA_GROUP * HEAD_DIM), lambda b, g, i: (b * nq + i, g)),
        out_shape=jax.ShapeDtypeStruct((ROWS, NSA_Q_WIDTH), F32),
        scratch_shapes=[pltpu.VMEM((LANES, tq), F32), pltpu.VMEM((HEAD_DIM, NSA_GROUP * tq), F32)],
        compiler_params=_params(("parallel", "parallel", "parallel")),
        name="nsa",
    )(proj, kcvc, kcvc, ks, vst, kw, vwt, proj, overlap_t, feat, cmask, wmask)


def _mix_kernel(x_ref, ya_ref, yb_ref, ga_ref, gb_ref, wa_ref, wb_ref, wo_ref, h_ref):
    ma = _dot(ya_ref[...], wa_ref[...])
    mb = _dot(yb_ref[...], wb_ref[...])
    mixed = _sigmoid(ga_ref[...]) * ma + _sigmoid(gb_ref[...]) * mb
    h_ref[...] = x_ref[...] + _dot(mixed, wo_ref[...])


def _mix(x2, ya, yb, proj, wa, wb, wo):
    tm = 256
    row = lambda w, c=0: pl.BlockSpec((tm, w), lambda i, c=c: (i, c))
    const = lambda s: pl.BlockSpec(s, lambda i: (0, 0), pipeline_mode=pl.Buffered(1))
    return pl.pallas_call(
        _mix_kernel,
        grid=(ROWS // tm,),
        in_specs=[row(D_MODEL), row(1024), row(1024), row(D_MODEL, COL_GA // D_MODEL), row(D_MODEL, COL_GB // D_MODEL),
                  const((1024, D_MODEL)), const((1024, D_MODEL)), const((D_MODEL, D_MODEL))],
        out_specs=row(D_MODEL),
        out_shape=jax.ShapeDtypeStruct((ROWS, D_MODEL), F32),
        compiler_params=_params(("parallel",)),
        name="mix",
    )(x2, ya, yb, proj, proj, wa, wb, wo)


def _mlp_kernel(h_ref, gm_ref, wu_ref, wd_ref, gf_ref, o_ref, hn_ref, acc_ref):
    f = pl.program_id(1)

    @pl.when(f == 0)
    def _():
        h = h_ref[...]
        ms = jnp.mean(h * h, axis=-1, keepdims=True)
        hn_ref[...] = (h * lax.rsqrt(ms + NORM_EPS) * gm_ref[...]).astype(BF16)
        acc_ref[...] = jnp.zeros_like(acc_ref)

    u = jnp.maximum(jnp.dot(hn_ref[...], wu_ref[...], preferred_element_type=F32), 0.0)
    acc_ref[...] += jnp.dot((u * u).astype(BF16), wd_ref[...], preferred_element_type=F32)

    @pl.when(f == pl.num_programs(1) - 1)
    def _():
        h2 = h_ref[...] + acc_ref[...]
        ms = jnp.mean(h2 * h2, axis=-1, keepdims=True)
        o_ref[...] = h2 * lax.rsqrt(ms + NORM_EPS) * gf_ref[...]


def _mlp(h, g_mlp, w_up, w_down, g_final):
    tm, tf = 512, 1024
    return pl.pallas_call(
        _mlp_kernel,
        grid=(ROWS // tm, D_FF // tf),
        in_specs=[pl.BlockSpec((tm, D_MODEL), lambda i, f: (i, 0)),
                  pl.BlockSpec((1, D_MODEL), lambda i, f: (0, 0)),
                  pl.BlockSpec((D_MODEL, tf), lambda i, f: (0, f)),
                  pl.BlockSpec((tf, D_MODEL), lambda i, f: (f, 0)),
                  pl.BlockSpec((1, D_MODEL), lambda i, f: (0, 0))],
        out_specs=pl.BlockSpec((tm, D_MODEL), lambda i, f: (i, 0)),
        out_shape=jax.ShapeDtypeStruct((ROWS, D_MODEL), F32),
        scratch_shapes=[pltpu.VMEM((tm, D_MODEL), BF16), pltpu.VMEM((tm, D_MODEL), F32)],
        compiler_params=_params(("parallel", "arbitrary")),
        name="mlp",
    )(h, g_mlp, w_up, w_down, g_final)


def _interleave_heads(a, axis=-1):
    a = jnp.moveaxis(a, axis, -1)
    a = a.reshape(a.shape[:-1] + (RWKV_HEADS, HEAD_DIM)).swapaxes(-1, -2).reshape(a.shape)
    return jnp.moveaxis(a, -1, axis)


def _overlap_matrix():
    cs = np.arange(N_CMP)[:, None] * CMP_STRIDE
    ss = np.arange(N_SEL)[None, :] * SEL_BLOCK
    ov = np.clip(np.minimum(cs + CMP_BLOCK, ss + SEL_BLOCK) - np.maximum(cs, ss), 0, None) / CMP_BLOCK
    out = np.zeros((N_CMP_PAD, N_SEL), np.float32)
    out[:N_CMP] = ov
    return out


def kernel(x, norm_mix, w_in, rwkv_mu, rwkv_w0, rwkv_w_up, rwkv_a0, rwkv_a_up, rwkv_g_up, rwkv_k_k, rwkv_k_a,
           rwkv_r_k, rwkv_lnx_w, rwkv_lnx_b, cmp_pe_k, cmp_w1_k, cmp_w2_k, cmp_pe_v, cmp_w1_v, cmp_w2_v,
           w_out_rwkv, w_out_nsa, w_o, norm_mlp, mlp_w_up, mlp_w_down, norm_final):
    assert x.shape == (BATCH, SEQ, D_MODEL) and w_in.shape[0] == 1
    l = 0
    G = NSA_KV_HEADS
    x2 = x.reshape(ROWS, D_MODEL)
    row = lambda a: a.reshape(1, -1)

    w_p = _pack_weight(w_in[l].T)
    proj = _inproj(x2, row(norm_mix[l]), w_p)

    il = _interleave_heads
    mu = rwkv_mu[l]
    mu_p = jnp.concatenate([il(mu[0:3072].reshape(3, RWKV_WIDTH)).reshape(-1), mu[3072:3168], jnp.zeros((32,), F32),
                            mu[3168:3264], jnp.zeros((32,), F32), mu[3264:3520]])
    zrows = jnp.zeros((32, RWKV_WIDTH), F32)
    head_of_lane = np.arange(RWKV_WIDTH)[:, None] % RWKV_HEADS == np.arange(LANES)[None, :]
    head_gather = jnp.asarray(head_of_lane, BF16)
    head_spread = jnp.asarray(head_of_lane.T, BF16)
    r, w, k, v, kk, b, g, bonus = _rwkvprep(
        proj, row(mu_p), row(il(rwkv_w0[l])), jnp.concatenate([il(rwkv_w_up[l]), zrows]),
        row(il(rwkv_a0[l])), jnp.concatenate([il(rwkv_a_up[l]), zrows]), il(rwkv_g_up[l]),
        row(il(rwkv_k_k[l].reshape(-1))), row(il(rwkv_k_a[l].reshape(-1))), row(il(rwkv_r_k[l].reshape(-1))),
        head_gather, head_spread)
    kkn_s, w_s, b_s = _relayout([kk, w, b], ("shift", "rows", "rows"))
    k_s, r_s, v_s = _relayout([k, r, v], ("rows", "rows", "time_major"))
    y = _scan(kkn_s, w_s, b_s, k_s, r_s, v_s)
    ya = _rwkv_out(y, bonus, g, row(il(rwkv_lnx_w[l])), row(il(rwkv_lnx_b[l])), head_gather, head_spread)

    craw, ks, vst, kw, vwt = _kvprep(proj)
    kcvc = _compress(craw, jnp.stack([cmp_w1_k[l], cmp_w1_v[l]]),
                     jnp.stack([cmp_pe_k[l].reshape(1, -1), cmp_pe_v[l].reshape(1, -1)]),
                     jnp.stack([cmp_w2_k[l], cmp_w2_v[l]]))
    per_group = lambda a: a.reshape((BATCH * G,) + a.shape[2:])
    yb = _nsa(proj, kcvc, per_group(ks), per_group(vst), per_group(kw), per_group(vwt),
              jnp.asarray(_overlap_matrix().T, BF16), jnp.asarray(_key_features(), BF16))

    h = _mix(x2, ya, yb, proj, il(w_out_rwkv[l], axis=0).astype(BF16), w_out_nsa[l].astype(BF16), w_o[l].astype(BF16))
    out = _mlp(h, row(norm_mlp[l]), mlp_w_up[l].astype(BF16), mlp_w_down[l].astype(BF16), row(norm_final))
    return out.reshape(BATCH, SEQ, D_MODEL)
```
